```python
import jax
import jax.numpy as jnp
from jax import lax
import numpy as np

D_MODEL = 1024
BATCH = 4
SEQ = 4096
DEPTH = 1
DEC_BATCH = 4
DEC_SEQ = 8192
PAST_LEN = 128

GRID_W = 64
EPS = 1e-6
HG_HEADS = 4
HG_DK = 128
HG_DV = 128
HG_WIDTH = HG_HEADS * HG_DV
HG_CHUNK = 64
NA_HEADS = 8
NA_HEAD_DIM = 64
NA_WIDTH = NA_HEADS * NA_HEAD_DIM
NA_WIN_R_MAX = 8
NA_WIN_C = 16
PEER_HEADS = 8
PEER_N_KEYS = 128
PEER_N_EXPERTS = PEER_N_KEYS * PEER_N_KEYS
PEER_D_QUERY = 256
PEER_D_HALF = PEER_D_QUERY // 2
PEER_TOPK_HALF = 16
PEER_TOPK = 16
PEER_TOKEN_BLOCK = 128

IN_SPLITS = (NA_WIDTH, NA_WIDTH, NA_WIDTH, 2 * HG_HEADS * HG_DK, 2 * HG_HEADS * HG_DK, 2 * HG_HEADS * HG_DV, HG_WIDTH, D_MODEL, D_MODEL)
D_IN = sum(IN_SPLITS)

kernel_name = 'hybrid_hgrn2_natten_peer_encoder'


def rms_norm(x, gain):
    xf = x.astype(jnp.float32)
    y = xf * lax.rsqrt(jnp.mean(xf * xf, axis=-1, keepdims=True) + EPS)
    return (y * gain.astype(jnp.float32)).astype(x.dtype)


def hgrn2_bidirectional(q, f_logit, v, gate, lb, out_gain):
    B, T = q.shape[:2]
    nc = T // HG_CHUNK
    f32 = jnp.float32
    dtype = q.dtype

    def to_scan(a):
        a = jnp.stack([a[:, :, 0], jnp.flip(a[:, :, 1], axis=1)], axis=2).astype(f32)
        a = a.reshape(B, nc, HG_CHUNK, 2, HG_HEADS, a.shape[-1])
        return a.transpose(1, 0, 3, 4, 2, 5)

    forget = lb.astype(f32) + (1.0 - lb.astype(f32)) * jax.nn.sigmoid(f_logit.astype(f32))
    qs = jax.nn.silu(to_scan(q))
    fs = to_scan(forget)
    ks = 1.0 - fs
    vs = to_scan(v)
    bs = jnp.cumsum(jnp.log(fs), axis=-2)
    tri = jnp.tril(jnp.ones((HG_CHUNK, HG_CHUNK), dtype=bool))

    def step(state, inp):
        qc, kc, vc, bc = inp
        diff = bc[..., :, None, :] - bc[..., None, :, :]
        decay = jnp.exp(jnp.where(tri[:, :, None], diff, -jnp.inf))
        scores = jnp.einsum('bghtd,bghsd,bghtsd->bghts', qc, kc, decay)
        o = (jnp.einsum('bghts,bghse->bghte', scores, vc)
             + jnp.einsum('bghtd,bghde->bghte', qc * jnp.exp(bc), state))
        b_end = bc[..., -1:, :]
        state = (state * jnp.exp(b_end)[..., 0, :, None]
                 + jnp.einsum('bghsd,bghse->bghde', kc * jnp.exp(b_end - bc), vc))
        return state, o

    s0 = jnp.zeros((B, 2, HG_HEADS, HG_DK, HG_DV), f32)
    _, o = lax.scan(step, s0, (qs, ks, vs, bs))
    o = o.transpose(1, 0, 4, 2, 3, 5).reshape(B, T, 2, HG_HEADS, HG_DV)
    o = o[:, :, 0] + jnp.flip(o[:, :, 1], axis=1)
    o = rms_norm(o, out_gain) * jax.nn.silu(gate.astype(f32))
    return o.reshape(B, T, HG_WIDTH).astype(dtype)


def neighborhood_attention(q, k, v, rpb):
    B, T, H, Dh = q.shape
    rows = T // GRID_W
    kr = min(NA_WIN_R_MAX, rows)
    q = q.reshape(B, rows, GRID_W, H, Dh)
    k = k.reshape(B, rows, GRID_W, H, Dh)
    v = v.reshape(B, rows, GRID_W, H, Dh)
    col = jnp.arange(GRID_W)
    col_start = jnp.clip(col - NA_WIN_C // 2, 0, GRID_W - NA_WIN_C)
    col_idx = col_start[:, None] + jnp.arange(NA_WIN_C)[None, :]
    col_off = col_idx - col[:, None] + NA_WIN_C - 1
    bias_c = rpb[:, :, col_off]
    scale = Dh ** -0.5

    def one_row(r):
        r0 = jnp.clip(r - kr // 2, 0, rows - kr)
        k_rows = lax.dynamic_slice_in_dim(k, r0, kr, axis=1)
        v_rows = lax.dynamic_slice_in_dim(v, r0, kr, axis=1)
        k_win = k_rows[:, :, col_idx]
        v_win = v_rows[:, :, col_idx]
        row_off = r0 + jnp.arange(kr) - r + NA_WIN_R_MAX - 1
        bias = bias_c[:, row_off].transpose(0, 2, 1, 3)
        qr = lax.dynamic_index_in_dim(q, r, axis=1, keepdims=False)
        s = (jnp.einsum('bchd,brcjhd->bhcrj', qr, k_win).astype(jnp.float32) * scale
             + bias.astype(jnp.float32)[None])
        p = jax.nn.softmax(s.reshape(B, H, GRID_W, kr * NA_WIN_C), axis=-1)
        p = p.reshape(B, H, GRID_W, kr, NA_WIN_C).astype(v.dtype)
        return jnp.einsum('bhcrj,brcjhd->bchd', p, v_win)

    out = lax.map(one_row, jnp.arange(rows))
    return out.transpose(1, 0, 2, 3, 4).reshape(B, T, H * Dh)


def peer_ffn(x, w_query, sub_keys, expert_u, expert_v):
    B, T, D = x.shape
    n = B * T
    xf = x.reshape(n, D)
    qry = (xf @ w_query).reshape(n, PEER_HEADS, 2, PEER_D_HALF)
    s = jnp.einsum('nhpd,hpkd->nhpk', qry, sub_keys).astype(jnp.float32)
    s_top, i_top = lax.top_k(s, PEER_TOPK_HALF)
    cand = s_top[:, :, 0, :, None] + s_top[:, :, 1, None, :]
    cand_idx = i_top[:, :, 0, :, None] * PEER_N_KEYS + i_top[:, :, 1, None, :]
    n_cand = PEER_TOPK_HALF * PEER_TOPK_HALF
    best, pos = lax.top_k(cand.reshape(n, PEER_HEADS, n_cand), PEER_TOPK)
    idx = jnp.take_along_axis(cand_idx.reshape(n, PEER_HEADS, n_cand), pos, axis=-1)
    gate = jax.nn.softmax(best, axis=-1).astype(x.dtype)
    nb = n // PEER_TOKEN_BLOCK

    def block(args):
        xb, ib, gb = args
        u = expert_u[ib]
        w = expert_v[ib]
        h = jax.nn.gelu(jnp.einsum('nd,nhkd->nhk', xb, u), approximate=False) * gb
        return jnp.einsum('nhk,nhkd->nd', h, w)

    y = lax.map(block, (xf.reshape(nb, PEER_TOKEN_BLOCK, D),
                        idx.reshape(nb, PEER_TOKEN_BLOCK, PEER_HEADS, PEER_TOPK),
                        gate.reshape(nb, PEER_TOKEN_BLOCK, PEER_HEADS, PEER_TOPK)))
    return y.reshape(B, T, D)


def encoder_layer(x, lb, norm_mix, w_in, hg_out_norm, q_norm, k_norm, rel_pos_bias,
                  w_proj_a, w_proj_b, w_out, norm_ffn, w_query, sub_keys, expert_u, expert_v):
    B, T, _ = x.shape
    h = rms_norm(x, norm_mix)
    z = h @ w_in
    cuts = [int(c) for c in np.cumsum(IN_SPLITS)[:-1]]
    na_q, na_k, na_v, hg_q, hg_f, hg_i, hg_g, gate_a, gate_b = jnp.split(z, cuts, axis=-1)
    o_a = hgrn2_bidirectional(hg_q.reshape(B, T, 2, HG_HEADS, HG_DK),
                              hg_f.reshape(B, T, 2, HG_HEADS, HG_DK),
                              hg_i.reshape(B, T, 2, HG_HEADS, HG_DV),
                              hg_g.reshape(B, T, HG_HEADS, HG_DV), lb, hg_out_norm)
    qb = rms_norm(na_q.reshape(B, T, NA_HEADS, NA_HEAD_DIM), q_norm)
    kb = rms_norm(na_k.reshape(B, T, NA_HEADS, NA_HEAD_DIM), k_norm)
    vb = na_v.reshape(B, T, NA_HEADS, NA_HEAD_DIM)
    o_b = neighborhood_attention(qb, kb, vb, rel_pos_bias)
    mix = jax.nn.sigmoid(gate_a) * (o_a @ w_proj_a) + jax.nn.sigmoid(gate_b) * (o_b @ w_proj_b)
    x = x + mix @ w_out
    x = x + peer_ffn(rms_norm(x, norm_ffn), w_query, sub_keys, expert_u, expert_v)
    return x


def run_trunk(x, norm_mix, w_in, lb_logits, hg_out_norm, q_norm, k_norm, rel_pos_bias,
              w_proj_a, w_proj_b, w_out, norm_ffn, w_query, sub_keys, expert_u, expert_v):
    lb_all = jnp.cumsum(jax.nn.softmax(lb_logits.astype(jnp.float32), axis=0), axis=0)
    for l in range(DEPTH):
        lb = lb_all[l].reshape(2, HG_HEADS, HG_DK)
        x = encoder_layer(x, lb, norm_mix[l], w_in[l], hg_out_norm[l], q_norm[l], k_norm[l],
                          rel_pos_bias[l], w_proj_a[l], w_proj_b[l], w_out[l], norm_ffn[l],
                          w_query[l], sub_keys[l], expert_u[l], expert_v[l])
    return x


def setup_inputs(seed: int = 0) -> dict:
    key = jax.random.key(seed)
    ks = jax.random.split(key, 18)
    f32 = jnp.float32

    def nrm(k, shape, scale):
        return jax.random.normal(k, shape, f32) * scale

    return {
        'x_prompt': nrm(ks[0], (BATCH, SEQ, D_MODEL), 1.0),
        'x_sample': nrm(ks[1], (DEC_BATCH, DEC_SEQ, D_MODEL), 1.0),
        'norm_mix': 1.0 + nrm(ks[2], (DEPTH, D_MODEL), 0.02),
        'w_in': nrm(ks[3], (DEPTH, D_MODEL, D_IN), D_MODEL ** -0.5),
        'lb_logits': nrm(ks[4], (DEPTH + 1, 2, HG_HEADS * HG_DK), 0.5),
        'hg_out_norm': 1.0 + nrm(ks[5], (DEPTH, HG_DV), 0.02),
        'q_norm': 1.0 + nrm(ks[6], (DEPTH, NA_HEAD_DIM), 0.02),
        'k_norm': 1.0 + nrm(ks[7], (DEPTH, NA_HEAD_DIM), 0.02),
        'rel_pos_bias': nrm(ks[8], (DEPTH, NA_HEADS, 2 * NA_WIN_R_MAX - 1, 2 * NA_WIN_C - 1), 0.1),
        'w_proj_a': nrm(ks[9], (DEPTH, HG_WIDTH, D_MODEL), HG_WIDTH ** -0.5),
        'w_proj_b': nrm(ks[10], (DEPTH, NA_WIDTH, D_MODEL), NA_WIDTH ** -0.5),
        'w_out': nrm(ks[11], (DEPTH, D_MODEL, D_MODEL), D_MODEL ** -0.5),
        'norm_ffn': 1.0 + nrm(ks[12], (DEPTH, D_MODEL), 0.02),
        'w_query': nrm(ks[13], (DEPTH, D_MODEL, PEER_HEADS * PEER_D_QUERY), D_MODEL ** -0.5),
        'sub_keys': nrm(ks[14], (DEPTH, PEER_HEADS, 2, PEER_N_KEYS, PEER_D_HALF), PEER_D_HALF ** -0.5),
        'expert_u': nrm(ks[15], (DEPTH, PEER_N_EXPERTS, D_MODEL), D_MODEL ** -0.5),
        'expert_v': nrm(ks[16], (DEPTH, PEER_N_EXPERTS, D_MODEL), PEER_HEADS ** -0.5),
    }


def reference(x_prompt, x_sample, norm_mix, w_in, lb_logits, hg_out_norm, q_norm, k_norm,
              rel_pos_bias, w_proj_a, w_proj_b, w_out, norm_ffn, w_query, sub_keys,
              expert_u, expert_v):
    y_prompt = run_trunk(x_prompt, norm_mix, w_in, lb_logits, hg_out_norm, q_norm, k_norm,
                         rel_pos_bias, w_proj_a, w_proj_b, w_out, norm_ffn, w_query,
                         sub_keys, expert_u, expert_v)
    y_sample = run_trunk(x_sample, norm_mix, w_in, lb_logits, hg_out_norm, q_norm, k_norm,
                         rel_pos_bias, w_proj_a, w_proj_b, w_out, norm_ffn, w_query,
                         sub_keys, expert_u, expert_v)
    return (y_prompt, y_sample)
```

```python
import functools

import numpy as np
import jax
import jax.numpy as jnp
from jax import lax
from jax.experimental import pallas as pl
from jax.experimental.pallas import tpu as pltpu

F32 = jnp.float32
BF16 = jnp.bfloat16

D_MODEL = 1024
GRID_W = 64
EPS = 1e-6
HG_HEADS = 4
HG_DK = 128
HG_DV = 128
HG_WIDTH = HG_HEADS * HG_DV
HG_CHUNK = 64
HG_SUB = 16
NA_HEADS = 8
NA_HEAD_DIM = 64
NA_WIDTH = NA_HEADS * NA_HEAD_DIM
NA_WIN_R = 8
NA_WIN_C = 16
PEER_HEADS = 8
PEER_N_KEYS = 128
PEER_N_EXPERTS = PEER_N_KEYS * PEER_N_KEYS
PEER_D_HALF = 128
PEER_TOPK = 16
D_IN = 3 * NA_WIDTH + 3 * 2 * HG_HEADS * HG_DK + HG_WIDTH + 2 * D_MODEL

_COL_HG_Q = (3 * NA_WIDTH) // 128
_COL_HG_F = _COL_HG_Q + 8
_COL_HG_I = _COL_HG_F + 8
_COL_HG_G = _COL_HG_I + 8

V7X_VMEM_LIMIT = 56 * 1024 * 1024
NEG_BIG = -1e30

_CAND_PAIRS = tuple((p, q) for p in range(PEER_TOPK) for q in range(PEER_TOPK)
                    if (p + 1) * (q + 1) <= PEER_TOPK)


def _dot(a, b):
    return jnp.dot(a, b, preferred_element_type=F32)


def _dot_nt(a, b):
    return lax.dot_general(a, b, (((1,), (1,)), ((), ())), preferred_element_type=F32)


def _dot_tn(a, b):
    return lax.dot_general(a, b, (((0,), (0,)), ((), ())), preferred_element_type=F32)


def _split_bf16(x):
    hi = x.astype(BF16)
    lo = (x - hi.astype(F32)).astype(BF16)
    return hi, lo


IN_TM = 512
IN_TN = 1024


def _in_proj_kernel(x_ref, g_ref, w_ref, qkg_ref, bd_ref, z_ref, xn_ref):
    j = pl.program_id(1)

    @pl.when(j == 0)
    def _():
        x = x_ref[...]
        ms = jnp.mean(x * x, axis=-1, keepdims=True)
        xn_ref[...] = (x * lax.rsqrt(ms + EPS) * g_ref[...]).astype(BF16)

    z = _dot(xn_ref[...], w_ref[...])

    @pl.when(j == 0)
    def _():
        hi, lo = _split_bf16(z * z)
        ss = _dot(hi, bd_ref[...]) + _dot(lo, bd_ref[...])
        z_ref[...] = z * lax.rsqrt(ss * (1.0 / NA_HEAD_DIM) + EPS) * qkg_ref[...]

    @pl.when(j != 0)
    def _():
        z_ref[...] = z


def _in_proj(x2d, gain, w_bf16, qk_gain, bd):
    n = x2d.shape[0]
    grid = (n // IN_TM, D_IN // IN_TN)
    return pl.pallas_call(
        _in_proj_kernel,
        grid=grid,
        in_specs=[
            pl.BlockSpec((IN_TM, D_MODEL), lambda i, j: (i, 0)),
            pl.BlockSpec((1, D_MODEL), lambda i, j: (0, 0)),
            pl.BlockSpec((D_MODEL, IN_TN), lambda i, j: (0, j)),
            pl.BlockSpec((1, IN_TN), lambda i, j: (0, 0)),
            pl.BlockSpec((IN_TN, IN_TN), lambda i, j: (0, 0)),
        ],
        out_specs=pl.BlockSpec((IN_TM, IN_TN), lambda i, j: (i, j)),
        out_shape=jax.ShapeDtypeStruct((n, D_IN), F32),
        scratch_shapes=[pltpu.VMEM((IN_TM, D_MODEL), BF16)],
        compiler_params=pltpu.CompilerParams(
            dimension_semantics=("parallel", "arbitrary"), vmem_limit_bytes=V7X_VMEM_LIMIT),
        name="in_proj",
    )(x2d, gain, w_bf16, qk_gain, bd)


HG_TT = 512


def _hgrn_kernel(*refs, reverse, finish):
    if finish:
        q_ref, f_ref, v_ref, lb_ref, of_ref, g_ref, gain_ref, o_ref, st_ref = refs
    else:
        q_ref, f_ref, v_ref, lb_ref, o_ref, st_ref = refs
    C, SB = HG_CHUNK, HG_SUB
    nchunk = HG_TT // C

    @pl.when(pl.program_id(2) == 0)
    def _():
        st_ref[...] = jnp.zeros_like(st_ref)

    lb = lb_ref[0]
    row = lax.broadcasted_iota(jnp.int32, (C, C), 0)
    col = lax.broadcasted_iota(jnp.int32, (C, C), 1)
    tri = ((col >= row) if reverse else (col <= row)).astype(BF16)
    t_i = lax.broadcasted_iota(jnp.int32, (SB, SB, HG_DK), 0)
    s_i = lax.broadcasted_iota(jnp.int32, (SB, SB, HG_DK), 1)
    pair_mask = (s_i >= t_i) if reverse else (s_i <= t_i)
    ones = jnp.ones((HG_DK, HG_DV), BF16)

    def chunk(ci, carry):
        c = (nchunk - 1 - ci) if reverse else ci
        sl = pl.ds(pl.multiple_of(c * C, C), C)
        q = q_ref[sl, :]
        v = v_ref[sl, :]
        f = lb + (1.0 - lb) * jax.nn.sigmoid(f_ref[sl, :])
        k = 1.0 - f
        qs = q * jax.nn.sigmoid(q)
        hi, lo = _split_bf16(jnp.log(f))
        b = _dot(tri, hi) + _dot(tri, lo)
        vb = v.astype(BF16)
        state = st_ref[...]
        o_inter = _dot_nt((qs * jnp.exp(b)).astype(BF16), state.astype(BF16))

        outs = []
        for blk in range(C // SB):
            lo_r, hi_r = blk * SB, (blk + 1) * SB
            b_i, q_i, k_i, v_i = b[lo_r:hi_r], qs[lo_r:hi_r], k[lo_r:hi_r], v[lo_r:hi_r]
            diff = b_i[:, None, :] - b_i[None, :, :]
            dec = jnp.exp(jnp.where(pair_mask, diff, -jnp.inf))
            prod = q_i[:, None, :] * k_i[None, :, :] * dec
            rs = _dot(prod.reshape(SB * SB, HG_DK).astype(BF16), ones)
            o_i = jnp.sum(rs.reshape(SB, SB, HG_DV) * v_i[None, :, :], axis=1)
            if reverse and blk < C // SB - 1:
                prev, ref = slice(hi_r, C), b[hi_r:hi_r + 1]
            elif (not reverse) and blk > 0:
                prev, ref = slice(0, lo_r), b[lo_r - 1:lo_r]
            else:
                prev = None
            if prev is not None:
                q_s = (q_i * jnp.exp(b_i - ref)).astype(BF16)
                k_s = (k[prev] * jnp.exp(ref - b[prev])).astype(BF16)
                s = _dot_nt(q_s, k_s)
                o_i = o_i + _dot(s.astype(BF16), vb[prev])
            outs.append(o_i)
        o = o_inter + jnp.concatenate(outs, axis=0)

        b_end = b[0:1] if reverse else b[C - 1:C]
        k_d = (k * jnp.exp(b_end - b)).astype(BF16)
        st_ref[...] = state * jnp.exp(b_end) + _dot_tn(vb, k_d)

        if finish:
            tot = of_ref[sl, :] + o
            ms = jnp.mean(tot * tot, axis=-1, keepdims=True)
            g = g_ref[sl, :]
            y = tot * lax.rsqrt(ms + EPS) * gain_ref[...] * (g * jax.nn.sigmoid(g))
            o_ref[sl, :] = y.astype(o_ref.dtype)
        else:
            o_ref[sl, :] = o
        return carry

    lax.fori_loop(0, nchunk, chunk, 0)


def _hgrn_pass(z, lb, batch, seq, *, reverse, o_fwd=None, gain=None):
    n = z.shape[0]
    nt = seq // HG_TT
    finish = o_fwd is not None
    d = 1 if reverse else 0

    def tmap(t):
        return (nt - 1 - t) if reverse else t

    def zspec(col0):
        return pl.BlockSpec((HG_TT, 128), lambda b, h, t: (b * nt + tmap(t), col0 + d * HG_HEADS + h))

    ospec = pl.BlockSpec((HG_TT, 128), lambda b, h, t: (b * nt + tmap(t), h))
    in_specs = [zspec(_COL_HG_Q), zspec(_COL_HG_F), zspec(_COL_HG_I),
                pl.BlockSpec((1, 1, HG_DK), lambda b, h, t: (d * HG_HEADS + h, 0, 0))]
    args = [z, z, z, lb]
    if finish:
        in_specs += [ospec,
                     pl.BlockSpec((HG_TT, 128), lambda b, h, t: (b * nt + tmap(t), _COL_HG_G + h)),
                     pl.BlockSpec((1, HG_DV), lambda b, h, t: (0, 0))]
        args += [o_fwd, z, gain]
    return pl.pallas_call(
        functools.partial(_hgrn_kernel, reverse=reverse, finish=finish),
        grid=(batch, HG_HEADS, nt),
        in_specs=in_specs,
        out_specs=ospec,
        out_shape=jax.ShapeDtypeStruct((n, HG_WIDTH), BF16 if finish else F32),
        scratch_shapes=[pltpu.VMEM((HG_DV, HG_DK), F32)],
        compiler_params=pltpu.CompilerParams(
            dimension_semantics=("parallel", "parallel", "arbitrary"),
            vmem_limit_bytes=V7X_VMEM_LIMIT),
        name="hgrn_bwd" if reverse else "hgrn_fwd",
    )(*args)


NA_BAND = 8
NA_BT = NA_BAND * GRID_W


def _na_bias_table(rpb):
    c = jnp.arange(GRID_W)
    c0 = jnp.clip(c - NA_WIN_C // 2, 0, GRID_W - NA_WIN_C)
    kc = jnp.arange(GRID_W)
    valid = (kc[None, :] >= c0[:, None]) & (kc[None, :] < c0[:, None] + NA_WIN_C)
    off = jnp.clip(kc[None, :] - c[:, None] + NA_WIN_C - 1, 0, 2 * NA_WIN_C - 2)
    t = jnp.where(valid[None, None], rpb.astype(F32)[:, :, off], NEG_BIG)
    t = t.reshape(NA_HEADS // 2, 2, 2 * NA_WIN_R - 1, GRID_W, GRID_W)
    t = t.transpose(0, 2, 4, 1, 3)
    return t.reshape(NA_HEADS // 2, (2 * NA_WIN_R - 1) * GRID_W, 2 * GRID_W)


def _na_kernel(q_ref, k0_ref, k1_ref, k2_ref, v0_ref, v1_ref, v2_ref, bias_ref, o_ref,
               kcat_ref, vcat_ref, *, rows):
    g = pl.program_id(1)
    for i, (kr, vr) in enumerate(((k0_ref, v0_ref), (k1_ref, v1_ref), (k2_ref, v2_ref))):
        kcat_ref[i * NA_BT:(i + 1) * NA_BT, :] = kr[...].astype(BF16)
        vcat_ref[i * NA_BT:(i + 1) * NA_BT, :] = vr[...].astype(BF16)
    lane = lax.broadcasted_iota(jnp.int32, (1, 2 * NA_HEAD_DIM), 1)
    left = lane < NA_HEAD_DIM
    nkeys = NA_WIN_R * GRID_W
    scale = NA_HEAD_DIM ** -0.5

    def body(j, carry):
        r = g * NA_BAND + j
        r0 = jnp.clip(r - NA_WIN_R // 2, 0, rows - NA_WIN_R)
        koff = pl.multiple_of((r0 - g * NA_BAND + NA_BAND) * GRID_W, GRID_W)
        boff = pl.multiple_of((r0 - r + NA_WIN_R - 1) * GRID_W, GRID_W)
        qsl = pl.ds(pl.multiple_of(j * GRID_W, GRID_W), GRID_W)
        qb = (q_ref[qsl, :] * scale).astype(BF16)
        for hp in range(NA_HEADS // 2):
            cs = slice(hp * 128, (hp + 1) * 128)
            q2 = qb[:, cs]
            zero = jnp.zeros_like(q2)
            rhs_t = jnp.concatenate([jnp.where(left, q2, zero), jnp.where(left, zero, q2)], axis=0)
            s_t = _dot_nt(kcat_ref[pl.ds(koff, nkeys), cs], rhs_t)
            s_t = s_t + bias_ref[hp, pl.ds(boff, nkeys), :]
            m = jnp.max(s_t, axis=0, keepdims=True)
            p = jnp.exp(s_t - m)
            l = jnp.sum(p, axis=0, keepdims=True)
            p = (p * (1.0 / l)).astype(BF16)
            o2 = _dot_tn(p, vcat_ref[pl.ds(koff, nkeys), cs])
            o = jnp.where(left, o2[:GRID_W], o2[GRID_W:])
            o_ref[qsl, cs] = o.astype(o_ref.dtype)
        return carry

    lax.fori_loop(0, NA_BAND, body, 0)


def _na(z, bias_tbl, batch, seq):
    n = z.shape[0]
    rows = seq // GRID_W
    nb = rows // NA_BAND

    def kv(colblk, shift):
        return pl.BlockSpec(
            (NA_BT, NA_WIDTH),
            lambda b, g: (b * nb + jnp.clip(g + shift, 0, nb - 1), colblk))

    return pl.pallas_call(
        functools.partial(_na_kernel, rows=rows),
        grid=(batch, nb),
        in_specs=[pl.BlockSpec((NA_BT, NA_WIDTH), lambda b, g: (b * nb + g, 0)),
                  kv(1, -1), kv(1, 0), kv(1, 1), kv(2, -1), kv(2, 0), kv(2, 1),
                  pl.BlockSpec(bias_tbl.shape, lambda b, g: (0, 0, 0))],
        out_specs=pl.BlockSpec((NA_BT, NA_WIDTH), lambda b, g: (b * nb + g, 0)),
        out_shape=jax.ShapeDtypeStruct((n, NA_WIDTH), BF16),
        scratch_shapes=[pltpu.VMEM((3 * NA_BT, NA_WIDTH), BF16),
                        pltpu.VMEM((3 * NA_BT, NA_WIDTH), BF16)],
        compiler_params=pltpu.CompilerParams(
            dimension_semantics=("parallel", "arbitrary"), vmem_limit_bytes=V7X_VMEM_LIMIT),
        name="natten",
    )(z, z, z, z, z, z, z, bias_tbl)


MG_TM = 512


def _merge_kernel(x_ref, oa_ref, ob_ref, ga_ref, gb_ref, wa_ref, wb_ref, wo_ref, nf_ref,
                  x2_ref, xn_ref):
    a = _dot(oa_ref[...], wa_ref[...])
    b = _dot(ob_ref[...], wb_ref[...])
    mix = jax.nn.sigmoid(ga_ref[...]) * a + jax.nn.sigmoid(gb_ref[...]) * b
    x2 = x_ref[...] + _dot(mix.astype(BF16), wo_ref[...])
    x2_ref[...] = x2
    ms = jnp.mean(x2 * x2, axis=-1, keepdims=True)
    xn_ref[...] = (x2 * lax.rsqrt(ms + EPS) * nf_ref[...]).astype(BF16)


def _merge(x2d, o_a, o_b, z, wa, wb, wo, norm_ffn):
    n = x2d.shape[0]
    col_ga = (_COL_HG_G * 128 + HG_WIDTH) // D_MODEL
    full = lambda shape: pl.BlockSpec(shape, lambda i: (0, 0))
    tok = lambda w, c=0: pl.BlockSpec((MG_TM, w), lambda i: (i, c))
    return pl.pallas_call(
        _merge_kernel,
        grid=(n // MG_TM,),
        in_specs=[tok(D_MODEL), tok(HG_WIDTH), tok(NA_WIDTH), tok(D_MODEL, col_ga), tok(D_MODEL, col_ga + 1),
                  full((HG_WIDTH, D_MODEL)), full((NA_WIDTH, D_MODEL)), full((D_MODEL, D_MODEL)),
                  full((1, D_MODEL))],
        out_specs=[tok(D_MODEL), tok(D_MODEL)],
        out_shape=[jax.ShapeDtypeStruct((n, D_MODEL), F32), jax.ShapeDtypeStruct((n, D_MODEL), BF16)],
        compiler_params=pltpu.CompilerParams(
            dimension_semantics=("parallel",), vmem_limit_bytes=V7X_VMEM_LIMIT),
        name="merge",
    )(x2d, o_a, o_b, z, z, wa, wb, wo, norm_ffn)


PEER_TB = 512
PEER_EB = 1024
PEER_LANES = 128


def _peer_route(xn_ref, wq_ref, keys_ref, s_ref, cur_ref, top_ref, cand_ref, thr_ref, coef_ref, e1_ref):
    xn = xn_ref[...]
    K = PEER_TOPK

    def score(hp, carry):
        rows = pl.ds(pl.multiple_of(hp * PEER_D_HALF, PEER_D_HALF), PEER_D_HALF)
        q_t = _dot_nt(wq_ref[rows, :], xn)
        s_ref[hp] = _dot(keys_ref[hp], q_t.astype(BF16))
        return carry

    lax.fori_loop(0, 2 * PEER_HEADS, score, 0)

    def extract(hp, carry):
        cur_ref[...] = s_ref[hp]
        h, half = hp // 2, hp % 2

        def step(it, carry2):
            cur = cur_ref[...]
            m = jnp.max(cur, axis=0, keepdims=True)
            top_ref[half, it, pl.ds(h, 1), :] = m
            cur_ref[...] = jnp.where(cur == m, -jnp.inf, cur)
            return carry2

        return lax.fori_loop(0, K, step, carry)

    lax.fori_loop(0, 2 * PEER_HEADS, extract, 0)

    for c, (p, q) in enumerate(_CAND_PAIRS):
        cand_ref[c] = top_ref[0, p] + top_ref[1, q]
    ncand = len(_CAND_PAIRS)

    def tau_step(c, tau):
        vc = cand_ref[c]

        def count(c2, cnt):
            return cnt + (cand_ref[c2] >= vc).astype(F32)

        cnt = lax.fori_loop(0, ncand, count, jnp.zeros_like(vc))
        return jnp.maximum(tau, jnp.where(cnt >= float(K), vc, -jnp.inf))

    tau = lax.fori_loop(0, ncand, tau_step, jnp.full(cand_ref.shape[1:], -jnp.inf, F32))
    best = cand_ref[0]

    def z_step(c, zsum):
        vc = cand_ref[c]
        return zsum + jnp.where(vc >= tau, jnp.exp(vc - best), 0.0)

    zsum = lax.fori_loop(0, ncand, z_step, jnp.zeros_like(tau))
    inv_z = 1.0 / zsum

    for h in range(PEER_HEADS):
        tau_h = tau[h:h + 1]
        s0 = s_ref[2 * h]
        thr = jnp.full(s0.shape, jnp.inf, F32)
        for q in range(K):
            b_q = top_ref[1, q, h:h + 1, :]
            thr = jnp.minimum(thr, jnp.where(s0 + b_q >= tau_h, b_q, jnp.inf))
        thr_ref[h] = thr
        coef_ref[h] = jnp.exp(s0 - top_ref[0, 0, h:h + 1, :])
        e1_ref[h] = jnp.exp(s_ref[2 * h + 1] - top_ref[1, 0, h:h + 1, :]) * inv_z[h:h + 1]


def _peer_kernel(xn_ref, x2_ref, wq_ref, keys_ref, u_ref, vt_ref, o_ref,
                 s_ref, cur_ref, top_ref, cand_ref, thr_ref, coef_ref, e1_ref, g_ref, acc_ref):
    e = pl.program_id(1)
    ne = pl.num_programs(1)

    @pl.when(e == 0)
    def _():
        _peer_route(xn_ref, wq_ref, keys_ref, s_ref, cur_ref, top_ref, cand_ref, thr_ref, coef_ref, e1_ref)
        acc_ref[...] = jnp.zeros_like(acc_ref)

    n_i = PEER_EB // PEER_N_KEYS

    i_rows = pl.ds(pl.multiple_of(e * n_i, n_i), n_i)
    for nc in range(PEER_TB // PEER_LANES):
        ls = slice(nc * PEER_LANES, (nc + 1) * PEER_LANES)
        thr_rows = [thr_ref[h, i_rows, ls] for h in range(PEER_HEADS)]
        coef_rows = [coef_ref[h, i_rows, ls] for h in range(PEER_HEADS)]
        for ii in range(n_i):
            acc = jnp.zeros((PEER_N_KEYS, PEER_LANES), F32)
            for h in range(PEER_HEADS):
                sel = s_ref[2 * h + 1, :, ls] >= thr_rows[h][ii:ii + 1]
                acc = acc + jnp.where(sel, e1_ref[h, :, ls], 0.0) * coef_rows[h][ii:ii + 1]
            g_ref[ii * PEER_N_KEYS:(ii + 1) * PEER_N_KEYS, ls] = acc

    h_t = _dot_nt(u_ref[...], xn_ref[...])
    act = 0.5 * h_t * (1.0 + lax.erf(h_t * (2.0 ** -0.5)))
    p_t = (act * g_ref[...]).astype(BF16)
    acc_ref[...] += _dot(vt_ref[...], p_t)

    @pl.when(e == ne - 1)
    def _():
        o_ref[...] = x2_ref[...] + acc_ref[...].T


def _peer(xn, x2, wq_t, keys, u_bf16, vt_bf16):
    n = xn.shape[0]
    tb, eb = PEER_TB, PEER_EB
    return pl.pallas_call(
        _peer_kernel,
        grid=(n // tb, PEER_N_EXPERTS // eb),
        in_specs=[
            pl.BlockSpec((tb, D_MODEL), lambda i, e: (i, 0)),
            pl.BlockSpec((tb, D_MODEL), lambda i, e: (i, 0)),
            pl.BlockSpec(wq_t.shape, lambda i, e: (0, 0)),
            pl.BlockSpec(keys.shape, lambda i, e: (0, 0, 0)),
            pl.BlockSpec((eb, D_MODEL), lambda i, e: (e, 0)),
            pl.BlockSpec((D_MODEL, eb), lambda i, e: (0, e)),
        ],
        out_specs=pl.BlockSpec((tb, D_MODEL), lambda i, e: (i, 0)),
        out_shape=jax.ShapeDtypeStruct((n, D_MODEL), F32),
        scratch_shapes=[
            pltpu.VMEM((2 * PEER_HEADS, PEER_N_KEYS, tb), F32),
            pltpu.VMEM((PEER_N_KEYS, tb), F32),
            pltpu.VMEM((2, PEER_TOPK, PEER_HEADS, tb), F32),
            pltpu.VMEM((len(_CAND_PAIRS), PEER_HEADS, tb), F32),
            pltpu.VMEM((PEER_HEADS, PEER_N_KEYS, tb), F32),
            pltpu.VMEM((PEER_HEADS, PEER_N_KEYS, tb), F32),
            pltpu.VMEM((PEER_HEADS, PEER_N_KEYS, tb), F32),
            pltpu.VMEM((eb, tb), F32),
            pltpu.VMEM((D_MODEL, tb), F32),
        ],
        compiler_params=pltpu.CompilerParams(
            dimension_semantics=("parallel", "arbitrary"), vmem_limit_bytes=V7X_VMEM_LIMIT),
        name="peer",
    )(xn, x2, wq_t, keys, u_bf16, vt_bf16)


def _trunk(x, p):
    batch, seq, _ = x.shape
    x2d = x.reshape(batch * seq, D_MODEL)
    z = _in_proj(x2d, p["norm_mix"], p["w_in"], p["qk_gain"], p["bd"])
    o_f = _hgrn_pass(z, p["lb"], batch, seq, reverse=False)
    o_a = _hgrn_pass(z, p["lb"], batch, seq, reverse=True, o_fwd=o_f, gain=p["hg_out_norm"])
    o_b = _na(z, p["na_bias"], batch, seq)
    x2, xn = _merge(x2d, o_a, o_b, z, p["w_proj_a"], p["w_proj_b"], p["w_out"], p["norm_ffn"])
    y = _peer(xn, x2, p["wq_t"], p["keys"], p["expert_u"], p["expert_vt"])
    return y.reshape(batch, seq, D_MODEL)


def kernel(x_prompt, x_sample, norm_mix, w_in, lb_logits, hg_out_norm, q_norm, k_norm, rel_pos_bias,
           w_proj_a, w_proj_b, w_out, norm_ffn, w_query, sub_keys, expert_u, expert_v):
    l = 0
    lb_all = jnp.cumsum(jax.nn.softmax(lb_logits.astype(F32), axis=0), axis=0)
    head_id = np.arange(2 * NA_WIDTH) // NA_HEAD_DIM
    p = {
        "norm_mix": norm_mix[l].reshape(1, D_MODEL),
        "w_in": w_in[l].astype(BF16),
        "qk_gain": jnp.concatenate([jnp.tile(q_norm[l], NA_HEADS), jnp.tile(k_norm[l], NA_HEADS)]).reshape(1, -1),
        "bd": jnp.asarray(head_id[:, None] == head_id[None, :], BF16),
        "lb": lb_all[l].reshape(2 * HG_HEADS, 1, HG_DK),
        "hg_out_norm": hg_out_norm[l].reshape(1, HG_DV),
        "na_bias": _na_bias_table(rel_pos_bias[l]),
        "w_proj_a": w_proj_a[l].astype(BF16),
        "w_proj_b": w_proj_b[l].astype(BF16),
        "w_out": w_out[l].astype(BF16),
        "norm_ffn": norm_ffn[l].reshape(1, D_MODEL),
        "wq_t": w_query[l].T.astype(BF16),
        "keys": sub_keys[l].reshape(2 * PEER_HEADS, PEER_N_KEYS, PEER_D_HALF).astype(BF16),
        "expert_u": expert_u[l].astype(BF16),
        "expert_vt": expert_v[l].T.astype(BF16),
    }
    return (_trunk(x_prompt, p), _trunk(x_sample, p))
```

```python
import functools

import numpy as np
import jax
import jax.numpy as jnp
from jax import lax
from jax.experimental import pallas as pl
from jax.experimental.pallas import tpu as pltpu

F32 = jnp.float32
BF16 = jnp.bfloat16

D_MODEL = 1024
GRID_W = 64
EPS = 1e-6
HG_HEADS = 4
HG_DK = 128
HG_DV = 128
HG_WIDTH = HG_HEADS * HG_DV
HG_CHUNK = 64
HG_SUB = 16
NA_HEADS = 8
NA_HEAD_DIM = 64
NA_WIDTH = NA_HEADS * NA_HEAD_DIM
NA_WIN_R = 8
NA_WIN_C = 16
PEER_HEADS = 8
PEER_N_KEYS = 128
PEER_N_EXPERTS = PEER_N_KEYS * PEER_N_KEYS
PEER_D_HALF = 128
PEER_TOPK = 16
D_IN = 3 * NA_WIDTH + 3 * 2 * HG_HEADS * HG_DK + HG_WIDTH + 2 * D_MODEL

_COL_HG_Q = (3 * NA_WIDTH) // 128
_COL_HG_F = _COL_HG_Q + 8
_COL_HG_I = _COL_HG_F + 8
_COL_HG_G = _COL_HG_I + 8

V7X_VMEM_LIMIT = 56 * 1024 * 1024
NEG_BIG = -1e30

_CAND_PAIRS = tuple((p, q) for p in range(PEER_TOPK) for q in range(PEER_TOPK)
                    if (p + 1) * (q + 1) <= PEER_TOPK)


def _dot(a, b):
    return jnp.dot(a, b, preferred_element_type=F32)


def _dot_nt(a, b):
    return lax.dot_general(a, b, (((1,), (1,)), ((), ())), preferred_element_type=F32)


def _dot_tn(a, b):
    return lax.dot_general(a, b, (((0,), (0,)), ((), ())), preferred_element_type=F32)


def _split_bf16(x):
    hi = x.astype(BF16)
    lo = (x - hi.astype(F32)).astype(BF16)
    return hi, lo


IN_TM = 512
IN_TN = 1024


def _in_proj_kernel(x_ref, g_ref, w_ref, qkg_ref, bd_ref, z_ref, xn_ref):
    j = pl.program_id(1)

    @pl.when(j == 0)
    def _():
        x = x_ref[...]
        ms = jnp.mean(x * x, axis=-1, keepdims=True)
        xn_ref[...] = (x * lax.rsqrt(ms + EPS) * g_ref[...]).astype(BF16)

    z = _dot(xn_ref[...], w_ref[...])

    @pl.when(j == 0)
    def _():
        hi, lo = _split_bf16(z * z)
        ss = _dot(hi, bd_ref[...]) + _dot(lo, bd_ref[...])
        z_ref[...] = z * lax.rsqrt(ss * (1.0 / NA_HEAD_DIM) + EPS) * qkg_ref[...]

    @pl.when(j != 0)
    def _():
        z_ref[...] = z


def _in_proj(x2d, gain, w_bf16, qk_gain, bd):
    n = x2d.shape[0]
    grid = (n // IN_TM, D_IN // IN_TN)
    return pl.pallas_call(
        _in_proj_kernel,
        grid=grid,
        in_specs=[
            pl.BlockSpec((IN_TM, D_MODEL), lambda i, j: (i, 0)),
            pl.BlockSpec((1, D_MODEL), lambda i, j: (0, 0)),
            pl.BlockSpec((D_MODEL, IN_TN), lambda i, j: (0, j)),
            pl.BlockSpec((1, IN_TN), lambda i, j: (0, 0)),
            pl.BlockSpec((IN_TN, IN_TN), lambda i, j: (0, 0)),
        ],
        out_specs=pl.BlockSpec((IN_TM, IN_TN), lambda i, j: (i, j)),
        out_shape=jax.ShapeDtypeStruct((n, D_IN), F32),
        scratch_shapes=[pltpu.VMEM((IN_TM, D_MODEL), BF16)],
        compiler_params=pltpu.CompilerParams(
            dimension_semantics=("parallel", "arbitrary"), vmem_limit_bytes=V7X_VMEM_LIMIT),
        name="in_proj",
    )(x2d, gain, w_bf16, qk_gain, bd)


HG_TT = 512


def _hgrn_kernel(*refs, reverse, finish):
    if finish:
        q_ref, f_ref, v_ref, lb_ref, of_ref, g_ref, gain_ref, o_ref, st_ref = refs
    else:
        q_ref, f_ref, v_ref, lb_ref, o_ref, st_ref = refs
    C, SB = HG_CHUNK, HG_SUB
    nchunk = HG_TT // C

    @pl.when(pl.program_id(2) == 0)
    def _():
        st_ref[...] = jnp.zeros_like(st_ref)

    lb = lb_ref[0]
    row = lax.broadcasted_iota(jnp.int32, (C, C), 0)
    col = lax.broadcasted_iota(jnp.int32, (C, C), 1)
    tri = ((col >= row) if reverse else (col <= row)).astype(BF16)
    t_i = lax.broadcasted_iota(jnp.int32, (SB, SB, HG_DK), 0)
    s_i = lax.broadcasted_iota(jnp.int32, (SB, SB, HG_DK), 1)
    pair_mask = (s_i >= t_i) if reverse else (s_i <= t_i)
    ones = jnp.ones((HG_DK, HG_DV), BF16)

    def chunk(ci, carry):
        c = (nchunk - 1 - ci) if reverse else ci
        sl = pl.ds(pl.multiple_of(c * C, C), C)
        q = q_ref[sl, :]
        v = v_ref[sl, :]
        f = lb + (1.0 - lb) * jax.nn.sigmoid(f_ref[sl, :])
        k = 1.0 - f
        qs = q * jax.nn.sigmoid(q)
        hi, lo = _split_bf16(jnp.log(f))
        b = _dot(tri, hi) + _dot(tri, lo)
        vb = v.astype(BF16)
        state = st_ref[...]
        o_inter = _dot_nt((qs * jnp.exp(b)).astype(BF16), state.astype(BF16))

        outs = []
        for blk in range(C // SB):
            lo_r, hi_r = blk * SB, (blk + 1) * SB
            b_i, q_i, k_i, v_i = b[lo_r:hi_r], qs[lo_r:hi_r], k[lo_r:hi_r], v[lo_r:hi_r]
            diff = b_i[:, None, :] - b_i[None, :, :]
            dec = jnp.exp(jnp.where(pair_mask, diff, -jnp.inf))
            prod = q_i[:, None, :] * k_i[None, :, :] * dec
            rs = _dot(prod.reshape(SB * SB, HG_DK).astype(BF16), ones)
            o_i = jnp.sum(rs.reshape(SB, SB, HG_DV) * v_i[None, :, :], axis=1)
            if reverse and blk < C // SB - 1:
                prev, ref = slice(hi_r, C), b[hi_r:hi_r + 1]
            elif (not reverse) and blk > 0:
                prev, ref = slice(0, lo_r), b[lo_r - 1:lo_r]
            else:
                prev = None
            if prev is not None:
                q_s = (q_i * jnp.exp(b_i - ref)).astype(BF16)
                k_s = (k[prev] * jnp.exp(ref - b[prev])).astype(BF16)
                s = _dot_nt(q_s, k_s)
                o_i = o_i + _dot(s.astype(BF16), vb[prev])
            outs.append(o_i)
        o = o_inter + jnp.concatenate(outs, axis=0)

        b_end = b[0:1] if reverse else b[C - 1:C]
        k_d = (k * jnp.exp(b_end - b)).astype(BF16)
        st_ref[...] = state * jnp.exp(b_end) + _dot_tn(vb, k_d)

        if finish:
            tot = of_ref[sl, :] + o
            ms = jnp.mean(tot * tot, axis=-1, keepdims=True)
            g = g_ref[sl, :]
            y = tot * lax.rsqrt(ms + EPS) * gain_ref[...] * (g * jax.nn.sigmoid(g))
            o_ref[sl, :] = y.astype(o_ref.dtype)
        else:
            o_ref[sl, :] = o
        return carry

    lax.fori_loop(0, nchunk, chunk, 0)


def _hgrn_pass(z, lb, batch, seq, *, reverse, o_fwd=None, gain=None):
    n = z.shape[0]
    nt = seq // HG_TT
    finish = o_fwd is not None
    d = 1 if reverse else 0

    def tmap(t):
        return (nt - 1 - t) if reverse else t

    def zspec(col0):
        return pl.BlockSpec((HG_TT, 128), lambda b, h, t: (b * nt + tmap(t), col0 + d * HG_HEADS + h))

    ospec = pl.BlockSpec((HG_TT, 128), lambda b, h, t: (b * nt + tmap(t), h))
    in_specs = [zspec(_COL_HG_Q), zspec(_COL_HG_F), zspec(_COL_HG_I),
                pl.BlockSpec((1, 1, HG_DK), lambda b, h, t: (d * HG_HEADS + h, 0, 0))]
    args = [z, z, z, lb]
    if finish:
        in_specs += [ospec,
                     pl.BlockSpec((HG_TT, 128), lambda b, h, t: (b * nt + tmap(t), _COL_HG_G + h)),
                     pl.BlockSpec((1, HG_DV), lambda b, h, t: (0, 0))]
        args += [o_fwd, z, gain]
    return pl.pallas_call(
        functools.partial(_hgrn_kernel, reverse=reverse, finish=finish),
        grid=(batch, HG_HEADS, nt),
        in_specs=in_specs,
        out_specs=ospec,
        out_shape=jax.ShapeDtypeStruct((n, HG_WIDTH), BF16 if finish else F32),
        scratch_shapes=[pltpu.VMEM((HG_DV, HG_DK), F32)],
        compiler_params=pltpu.CompilerParams(
            dimension_semantics=("parallel", "parallel", "arbitrary"),
            vmem_limit_bytes=V7X_VMEM_LIMIT),
        name="hgrn_bwd" if reverse else "hgrn_fwd",
    )(*args)


NA_BAND = 8
NA_BT = NA_BAND * GRID_W


def _na_bias_table(rpb):
    c = jnp.arange(GRID_W)
    c0 = jnp.clip(c - NA_WIN_C // 2, 0, GRID_W - NA_WIN_C)
    kc = jnp.arange(GRID_W)
    valid = (kc[None, :] >= c0[:, None]) & (kc[None, :] < c0[:, None] + NA_WIN_C)
    off = jnp.clip(kc[None, :] - c[:, None] + NA_WIN_C - 1, 0, 2 * NA_WIN_C - 2)
    t = jnp.where(valid[None, None], rpb.astype(F32)[:, :, off], NEG_BIG)
    t = t.reshape(NA_HEADS // 2, 2, 2 * NA_WIN_R - 1, GRID_W, GRID_W)
    t = t.transpose(0, 2, 4, 1, 3)
    return t.reshape(NA_HEADS // 2, (2 * NA_WIN_R - 1) * GRID_W, 2 * GRID_W)


def _na_kernel(q_ref, k0_ref, k1_ref, k2_ref, v0_ref, v1_ref, v2_ref, bias_ref, o_ref,
               kcat_ref, vcat_ref, *, rows):
    g = pl.program_id(1)
    for i, (kr, vr) in enumerate(((k0_ref, v0_ref), (k1_ref, v1_ref), (k2_ref, v2_ref))):
        kcat_ref[i * NA_BT:(i + 1) * NA_BT, :] = kr[...].astype(BF16)
        vcat_ref[i * NA_BT:(i + 1) * NA_BT, :] = vr[...].astype(BF16)
    lane = lax.broadcasted_iota(jnp.int32, (1, 2 * NA_HEAD_DIM), 1)
    left = lane < NA_HEAD_DIM
    nkeys = NA_WIN_R * GRID_W
    scale = NA_HEAD_DIM ** -0.5

    def body(j, carry):
        r = g * NA_BAND + j
        r0 = jnp.clip(r - NA_WIN_R // 2, 0, rows - NA_WIN_R)
        koff = pl.multiple_of((r0 - g * NA_BAND + NA_BAND) * GRID_W, GRID_W)
        boff = pl.multiple_of((r0 - r + NA_WIN_R - 1) * GRID_W, GRID_W)
        qsl = pl.ds(pl.multiple_of(j * GRID_W, GRID_W), GRID_W)
        qb = (q_ref[qsl, :] * scale).astype(BF16)
        for hp in range(NA_HEADS // 2):
            cs = slice(hp * 128, (hp + 1) * 128)
            q2 = qb[:, cs]
            zero = jnp.zeros_like(q2)
            rhs_t = jnp.concatenate([jnp.where(left, q2, zero), jnp.where(left, zero, q2)], axis=0)
            s_t = _dot_nt(kcat_ref[pl.ds(koff, nkeys), cs], rhs_t)
            s_t = s_t + bias_ref[hp, pl.ds(boff, nkeys), :]
            m = jnp.max(s_t, axis=0, keepdims=True)
            p = jnp.exp(s_t - m)
            l = jnp.sum(p, axis=0, keepdims=True)
            p = (p * (1.0 / l)).astype(BF16)
            o2 = _dot_tn(p, vcat_ref[pl.ds(koff, nkeys), cs])
            o = jnp.where(left, o2[:GRID_W], o2[GRID_W:])
            o_ref[qsl, cs] = o.astype(o_ref.dtype)
        return carry

    lax.fori_loop(0, NA_BAND, body, 0)


def _na(z, bias_tbl, batch, seq):
    n = z.shape[0]
    rows = seq // GRID_W
    nb = rows // NA_BAND

    def kv(colblk, shift):
        return pl.BlockSpec(
            (NA_BT, NA_WIDTH),
            lambda b, g: (b * nb + jnp.clip(g + shift, 0, nb - 1), colblk))

    return pl.pallas_call(
        functools.partial(_na_kernel, rows=rows),
        grid=(batch, nb),
        in_specs=[pl.BlockSpec((NA_BT, NA_WIDTH), lambda b, g: (b * nb + g, 0)),
                  kv(1, -1), kv(1, 0), kv(1, 1), kv(2, -1), kv(2, 0), kv(2, 1),
                  pl.BlockSpec(bias_tbl.shape, lambda b, g: (0, 0, 0))],
        out_specs=pl.BlockSpec((NA_BT, NA_WIDTH), lambda b, g: (b * nb + g, 0)),
        out_shape=jax.ShapeDtypeStruct((n, NA_WIDTH), BF16),
        scratch_shapes=[pltpu.VMEM((3 * NA_BT, NA_WIDTH), BF16),
                        pltpu.VMEM((3 * NA_BT, NA_WIDTH), BF16)],
        compiler_params=pltpu.CompilerParams(
            dimension_semantics=("parallel", "arbitrary"), vmem_limit_bytes=V7X_VMEM_LIMIT),
        name="natten",
    )(z, z, z, z, z, z, z, bias_tbl)


MG_TM = 512


def _merge_kernel(x_ref, oa_ref, ob_ref, ga_ref, gb_ref, wa_ref, wb_ref, wo_ref, nf_ref,
                  x2_ref, xn_ref):
    a = _dot(oa_ref[...], wa_ref[...])
    b = _dot(ob_ref[...], wb_ref[...])
    mix = jax.nn.sigmoid(ga_ref[...]) * a + jax.nn.sigmoid(gb_ref[...]) * b
    x2 = x_ref[...] + _dot(mix.astype(BF16), wo_ref[...])
    x2_ref[...] = x2
    ms = jnp.mean(x2 * x2, axis=-1, keepdims=True)
    xn_ref[...] = (x2 * lax.rsqrt(ms + EPS) * nf_ref[...]).astype(BF16)


def _merge(x2d, o_a, o_b, z, wa, wb, wo, norm_ffn):
    n = x2d.shape[0]
    col_ga = (_COL_HG_G * 128 + HG_WIDTH) // D_MODEL
    full = lambda shape: pl.BlockSpec(shape, lambda i: (0, 0))
    tok = lambda w, c=0: pl.BlockSpec((MG_TM, w), lambda i: (i, c))
    return pl.pallas_call(
        _merge_kernel,
        grid=(n // MG_TM,),
        in_specs=[tok(D_MODEL), tok(HG_WIDTH), tok(NA_WIDTH), tok(D_MODEL, col_ga), tok(D_MODEL, col_ga + 1),
                  full((HG_WIDTH, D_MODEL)), full((NA_WIDTH, D_MODEL)), full((D_MODEL, D_MODEL)),
                  full((1, D_MODEL))],
        out_specs=[tok(D_MODEL), tok(D_MODEL)],
        out_shape=[jax.ShapeDtypeStruct((n, D_MODEL), F32), jax.ShapeDtypeStruct((n, D_MODEL), BF16)],
        compiler_params=pltpu.CompilerParams(
            dimension_semantics=("parallel",), vmem_limit_bytes=V7X_VMEM_LIMIT),
        name="merge",
    )(x2d, o_a, o_b, z, z, wa, wb, wo, norm_ffn)


PEER_TB = 512
PEER_EB = 1024
PEER_LC = 256
PEER_KC = 256


def _peer_route(xn_ref, wq_ref, keys_ref, qt_ref, s_ref, cur_ref, rank_ref, top_ref, cand_ref, theta_ref,
                cnt_ref, coef_ref, r1_ref, e1_ref):
    K = PEER_TOPK
    half_rows = wq_ref.shape[0] // 2
    for part in range(2):
        rs = slice(part * half_rows, (part + 1) * half_rows)
        qt_ref[rs, :] = _dot_nt(wq_ref[rs, :], xn_ref[...]).astype(BF16)

    def score(hp, carry):
        rows = pl.ds(pl.multiple_of(hp * PEER_D_HALF, PEER_D_HALF), PEER_D_HALF)
        s_ref[hp] = _dot(keys_ref[hp], qt_ref[rows, :])
        return carry

    lax.fori_loop(0, 2 * PEER_HEADS, score, 0)

    def extract(h, carry, *, half):
        cur_ref[...] = s_ref[2 * h + half]
        if half:
            rank_ref[...] = jnp.full(rank_ref.shape, float(K), F32)

        def step(it, carry2):
            cur = cur_ref[...]
            m = jnp.max(cur, axis=0, keepdims=True)
            top_ref[half, it, pl.ds(h, 1), :] = m
            hit = cur == m
            cur_ref[...] = jnp.where(hit, -jnp.inf, cur)
            if half:
                rank_ref[...] = jnp.where(hit, it.astype(F32), rank_ref[...])
            return carry2

        lax.fori_loop(0, K, step, 0)
        if half:
            r1_ref[h] = rank_ref[...].astype(BF16)
        return carry

    lax.fori_loop(0, PEER_HEADS, functools.partial(extract, half=0), 0)
    lax.fori_loop(0, PEER_HEADS, functools.partial(extract, half=1), 0)

    ncand = len(_CAND_PAIRS)
    for c, (p, q) in enumerate(_CAND_PAIRS):
        cand_ref[c] = top_ref[0, p] + top_ref[1, q]
    small = cand_ref.shape[1:]

    def tau_step(it, carry):
        tau, cnt = carry
        m = cand_ref[0]
        for c in range(1, ncand):
            m = jnp.maximum(m, cand_ref[c])
        n_eq = jnp.zeros(small, F32)
        for c in range(ncand):
            v = cand_ref[c]
            hit = v == m
            n_eq = n_eq + jnp.where(hit, 1.0, 0.0)
            cand_ref[c] = jnp.where(hit, -jnp.inf, v)
        return jnp.where(cnt < float(K), m, tau), cnt + n_eq

    tau, _ = lax.fori_loop(0, K, tau_step, (jnp.full(small, -jnp.inf, F32), jnp.zeros(small, F32)))
    best = top_ref[0, 0] + top_ref[1, 0]
    zsum = jnp.zeros(small, F32)
    for p, q in _CAND_PAIRS:
        v = top_ref[0, p] + top_ref[1, q]
        zsum = zsum + jnp.where(v >= tau, jnp.exp(v - best), 0.0)
    inv_z = 1.0 / zsum

    for q in range(K):
        th = jnp.full(small, jnp.inf, F32)
        b_q = top_ref[1, q]
        for p in range(K):
            a_p = top_ref[0, p]
            th = jnp.minimum(th, jnp.where(a_p + b_q >= tau, a_p, jnp.inf))
        theta_ref[q] = th

    for h in range(PEER_HEADS):
        s0 = s_ref[2 * h]
        cnt = jnp.zeros(s0.shape, F32)
        for q in range(K):
            cnt = cnt + jnp.where(s0 >= theta_ref[q, h:h + 1, :], 1.0, 0.0)
        cnt_ref[h] = cnt
        coef_ref[h] = jnp.exp(s0 - top_ref[0, 0, h:h + 1, :])
        e1_ref[h] = (jnp.exp(s_ref[2 * h + 1] - top_ref[1, 0, h:h + 1, :]) * inv_z[h:h + 1]).astype(BF16)


def _peer_kernel(xn_ref, x2_ref, wq_ref, keys_ref, u_ref, vt_ref, o_ref,
                 qt_ref, s_ref, cur_ref, rank_ref, top_ref, cand_ref, theta_ref,
                 cnt_ref, coef_ref, r1_ref, e1_ref, g_ref, acc_ref):
    e = pl.program_id(1)
    ne = pl.num_programs(1)

    @pl.when(e == 0)
    def _():
        _peer_route(xn_ref, wq_ref, keys_ref, qt_ref, s_ref, cur_ref, rank_ref, top_ref, cand_ref, theta_ref,
                    cnt_ref, coef_ref, r1_ref, e1_ref)
        acc_ref[...] = jnp.zeros_like(acc_ref)

    n_i = PEER_EB // PEER_N_KEYS
    i_rows = pl.ds(pl.multiple_of(e * n_i, n_i), n_i)
    h_t = _dot_nt(u_ref[...], xn_ref[...])
    zero = jnp.zeros((PEER_N_KEYS, PEER_LC), BF16)
    for lc in range(PEER_TB // PEER_LC):
        ls = slice(lc * PEER_LC, (lc + 1) * PEER_LC)
        cnt_rows = [cnt_ref[h, i_rows, ls].astype(BF16) for h in range(PEER_HEADS)]
        coef_rows = [coef_ref[h, i_rows, ls].astype(BF16) for h in range(PEER_HEADS)]
        for ii in range(n_i):
            gate = zero
            for h in range(PEER_HEADS):
                sel = r1_ref[h, :, ls] < cnt_rows[h][ii:ii + 1]
                gate = gate + jnp.where(sel, e1_ref[h, :, ls], zero) * coef_rows[h][ii:ii + 1]
            g_ref[ii * PEER_N_KEYS:(ii + 1) * PEER_N_KEYS, ls] = gate
    y_t = None
    for c in range(PEER_EB // PEER_KC):
        rs = slice(c * PEER_KC, (c + 1) * PEER_KC)
        hh = h_t[rs]
        act = 0.5 * hh * (1.0 + lax.erf(hh * (2.0 ** -0.5)))
        part = _dot(vt_ref[:, rs], act.astype(BF16) * g_ref[rs, :])
        y_t = part if y_t is None else y_t + part
    acc_ref[...] += y_t

    @pl.when(e == ne - 1)
    def _():
        o_ref[...] = x2_ref[...] + acc_ref[...].T


def _peer(xn, x2, wq_t, keys, u_bf16, vt_bf16):
    n = xn.shape[0]
    tb, eb = PEER_TB, PEER_EB
    return pl.pallas_call(
        _peer_kernel,
        grid=(n // tb, PEER_N_EXPERTS // eb),
        in_specs=[
            pl.BlockSpec((tb, D_MODEL), lambda i, e: (i, 0)),
            pl.BlockSpec((tb, D_MODEL), lambda i, e: (i, 0)),
            pl.BlockSpec(wq_t.shape, lambda i, e: (0, 0)),
            pl.BlockSpec(keys.shape, lambda i, e: (0, 0, 0)),
            pl.BlockSpec((eb, D_MODEL), lambda i, e: (e, 0)),
            pl.BlockSpec((D_MODEL, eb), lambda i, e: (0, e)),
        ],
        out_specs=pl.BlockSpec((tb, D_MODEL), lambda i, e: (i, 0)),
        out_shape=jax.ShapeDtypeStruct((n, D_MODEL), F32),
        scratch_shapes=[
            pltpu.VMEM((wq_t.shape[0], tb), BF16),
            pltpu.VMEM((2 * PEER_HEADS, PEER_N_KEYS, tb), F32),
            pltpu.VMEM((PEER_N_KEYS, tb), F32),
            pltpu.VMEM((PEER_N_KEYS, tb), F32),
            pltpu.VMEM((2, PEER_TOPK, PEER_HEADS, tb), F32),
            pltpu.VMEM((len(_CAND_PAIRS), PEER_HEADS, tb), F32),
            pltpu.VMEM((PEER_TOPK, PEER_HEADS, tb), F32),
            pltpu.VMEM((PEER_HEADS, PEER_N_KEYS, tb), F32),
            pltpu.VMEM((PEER_HEADS, PEER_N_KEYS, tb), F32),
            pltpu.VMEM((PEER_HEADS, PEER_N_KEYS, tb), BF16),
            pltpu.VMEM((PEER_HEADS, PEER_N_KEYS, tb), BF16),
            pltpu.VMEM((eb, tb), BF16),
            pltpu.VMEM((D_MODEL, tb), F32),
        ],
        compiler_params=pltpu.CompilerParams(
            dimension_semantics=("parallel", "arbitrary"), vmem_limit_bytes=V7X_VMEM_LIMIT),
        name="peer",
    )(xn, x2, wq_t, keys, u_bf16, vt_bf16)


def _trunk(x, p):
    batch, seq, _ = x.shape
    x2d = x.reshape(batch * seq, D_MODEL)
    z = _in_proj(x2d, p["norm_mix"], p["w_in"], p["qk_gain"], p["bd"])
    o_f = _hgrn_pass(z, p["lb"], batch, seq, reverse=False)
    o_a = _hgrn_pass(z, p["lb"], batch, seq, reverse=True, o_fwd=o_f, gain=p["hg_out_norm"])
    o_b = _na(z, p["na_bias"], batch, seq)
    x2, xn = _merge(x2d, o_a, o_b, z, p["w_proj_a"], p["w_proj_b"], p["w_out"], p["norm_ffn"])
    y = _peer(xn, x2, p["wq_t"], p["keys"], p["expert_u"], p["expert_vt"])
    return y.reshape(batch, seq, D_MODEL)


def kernel(x_prompt, x_sample, norm_mix, w_in, lb_logits, hg_out_norm, q_norm, k_norm, rel_pos_bias,
           w_proj_a, w_proj_b, w_out, norm_ffn, w_query, sub_keys, expert_u, expert_v):
    l = 0
    lb_all = jnp.cumsum(jax.nn.softmax(lb_logits.astype(F32), axis=0), axis=0)
    head_id = np.arange(2 * NA_WIDTH) // NA_HEAD_DIM
    p = {
        "norm_mix": norm_mix[l].reshape(1, D_MODEL),
        "w_in": w_in[l].astype(BF16),
        "qk_gain": jnp.concatenate([jnp.tile(q_norm[l], NA_HEADS), jnp.tile(k_norm[l], NA_HEADS)]).reshape(1, -1),
        "bd": jnp.asarray(head_id[:, None] == head_id[None, :], BF16),
        "lb": lb_all[l].reshape(2 * HG_HEADS, 1, HG_DK),
        "hg_out_norm": hg_out_norm[l].reshape(1, HG_DV),
        "na_bias": _na_bias_table(rel_pos_bias[l]),
        "w_proj_a": w_proj_a[l].astype(BF16),
        "w_proj_b": w_proj_b[l].astype(BF16),
        "w_out": w_out[l].astype(BF16),
        "norm_ffn": norm_ffn[l].reshape(1, D_MODEL),
        "wq_t": w_query[l].T.astype(BF16),
        "keys": sub_keys[l].reshape(2 * PEER_HEADS, PEER_N_KEYS, PEER_D_HALF).astype(BF16),
        "expert_u": expert_u[l].astype(BF16),
        "expert_vt": expert_v[l].T.astype(BF16),
    }
    return (_trunk(x_prompt, p), _trunk(x_sample, p))
```

```python
import functools

import numpy as np
import jax
import jax.numpy as jnp
from jax import lax
from jax.experimental import pallas as pl
from jax.experimental.pallas import tpu as pltpu

F32 = jnp.float32
BF16 = jnp.bfloat16

D_MODEL = 1024
GRID_W = 64
EPS = 1e-6
HG_HEADS = 4
HG_DK = 128
HG_DV = 128
HG_WIDTH = HG_HEADS * HG_DV
HG_CHUNK = 64
HG_SUB = 16
NA_HEADS = 8
NA_HEAD_DIM = 64
NA_WIDTH = NA_HEADS * NA_HEAD_DIM
NA_WIN_R = 8
NA_WIN_C = 16
PEER_HEADS = 8
PEER_N_KEYS = 128
PEER_N_EXPERTS = PEER_N_KEYS * PEER_N_KEYS
PEER_D_HALF = 128
PEER_TOPK = 16
D_IN = 3 * NA_WIDTH + 3 * 2 * HG_HEADS * HG_DK + HG_WIDTH + 2 * D_MODEL

_COL_HG_Q = (3 * NA_WIDTH) // 128
_COL_HG_F = _COL_HG_Q + 8
_COL_HG_I = _COL_HG_F + 8
_COL_HG_G = _COL_HG_I + 8

V7X_VMEM_LIMIT = 56 * 1024 * 1024
NEG_BIG = -1e30

_CAND_PAIRS = tuple((p, q) for p in range(PEER_TOPK) for q in range(PEER_TOPK)
                    if (p + 1) * (q + 1) <= PEER_TOPK)


def _dot(a, b):
    return jnp.dot(a, b, preferred_element_type=F32)


def _dot_nt(a, b):
    return lax.dot_general(a, b, (((1,), (1,)), ((), ())), preferred_element_type=F32)


def _dot_tn(a, b):
    return lax.dot_general(a, b, (((0,), (0,)), ((), ())), preferred_element_type=F32)


def _split_bf16(x):
    hi = x.astype(BF16)
    lo = (x - hi.astype(F32)).astype(BF16)
    return hi, lo


IN_TM = 512
IN_TN = 1024


def _in_proj_kernel(x_ref, g_ref, w_ref, qkg_ref, bd_ref, z_ref, xn_ref):
    j = pl.program_id(1)

    @pl.when(j == 0)
    def _():
        x = x_ref[...]
        ms = jnp.mean(x * x, axis=-1, keepdims=True)
        xn_ref[...] = (x * lax.rsqrt(ms + EPS) * g_ref[...]).astype(BF16)

    z = _dot(xn_ref[...], w_ref[...])

    @pl.when(j == 0)
    def _():
        hi, lo = _split_bf16(z * z)
        ss = _dot(hi, bd_ref[...]) + _dot(lo, bd_ref[...])
        z_ref[...] = z * lax.rsqrt(ss * (1.0 / NA_HEAD_DIM) + EPS) * qkg_ref[...]

    @pl.when(j != 0)
    def _():
        z_ref[...] = z


def _in_proj(x2d, gain, w_bf16, qk_gain, bd):
    n = x2d.shape[0]
    grid = (n // IN_TM, D_IN // IN_TN)
    return pl.pallas_call(
        _in_proj_kernel,
        grid=grid,
        in_specs=[
            pl.BlockSpec((IN_TM, D_MODEL), lambda i, j: (i, 0)),
            pl.BlockSpec((1, D_MODEL), lambda i, j: (0, 0)),
            pl.BlockSpec((D_MODEL, IN_TN), lambda i, j: (0, j)),
            pl.BlockSpec((1, IN_TN), lambda i, j: (0, 0)),
            pl.BlockSpec((IN_TN, IN_TN), lambda i, j: (0, 0)),
        ],
        out_specs=pl.BlockSpec((IN_TM, IN_TN), lambda i, j: (i, j)),
        out_shape=jax.ShapeDtypeStruct((n, D_IN), F32),
        scratch_shapes=[pltpu.VMEM((IN_TM, D_MODEL), BF16)],
        compiler_params=pltpu.CompilerParams(
            dimension_semantics=("parallel", "arbitrary"), vmem_limit_bytes=V7X_VMEM_LIMIT),
        name="in_proj",
    )(x2d, gain, w_bf16, qk_gain, bd)


HG_TT = 512


def _hgrn_chunk(q, f_logit, v, lb, states, tri, pair_mask, ones, *, reverse):
    C, SB = HG_CHUNK, HG_SUB
    nblk = C // SB
    f = lb + (1.0 - lb) * jax.nn.sigmoid(f_logit)
    qs = q * jax.nn.sigmoid(q)
    hi, lo = _split_bf16(jnp.log(f))
    b = _dot(tri, hi) + _dot(tri, lo)
    c = b - jnp.log(1.0 - f)
    vb = v.astype(BF16)
    q_all = (qs * jnp.exp(b)).astype(BF16)
    b_end = b[0:1] if reverse else b[C - 1:C]
    k_d = jnp.exp(b_end - c).astype(BF16)
    carry_decay = jnp.exp(b_end)

    prevs, q_s, k_s = [], [], []
    for blk in range(nblk):
        lo_r, hi_r = blk * SB, (blk + 1) * SB
        if reverse and blk < nblk - 1:
            prev, ref = slice(hi_r, C), b[hi_r:hi_r + 1]
        elif (not reverse) and blk > 0:
            prev, ref = slice(0, lo_r), b[lo_r - 1:lo_r]
        else:
            prev = None
        prevs.append(prev)
        q_s.append(None if prev is None else (qs[lo_r:hi_r] * jnp.exp(b[lo_r:hi_r] - ref)).astype(BF16))
        k_s.append(None if prev is None else jnp.exp(ref - c[prev]).astype(BF16))

    outs, new_states = [], []
    for h, state in enumerate(states):
        cs = slice(h * HG_DK, (h + 1) * HG_DK)
        parts = []
        for blk in range(nblk):
            rows = slice(blk * SB, (blk + 1) * SB)
            prod = (qs[rows, cs][None, :, :] * jnp.exp(b[rows, cs][None, :, :] - c[rows, cs][:, None, :]))
            prod = prod.astype(BF16)
            prod = jnp.where(pair_mask, prod, jnp.zeros_like(prod))
            rs = _dot(prod.reshape(SB * SB, HG_DK), ones)
            o_i = jnp.sum(rs.reshape(SB, SB, HG_DV) * v[rows, cs][:, None, :], axis=0)
            if prevs[blk] is not None:
                s = _dot_nt(q_s[blk][:, cs], k_s[blk][:, cs])
                o_i = o_i + _dot(s.astype(BF16), vb[prevs[blk], cs])
            parts.append(o_i)
        o_inter = _dot_nt(q_all[:, cs], state.astype(BF16))
        outs.append(o_inter + jnp.concatenate(parts, axis=0))
        new_states.append(state * carry_decay[:, cs] + _dot_tn(vb[:, cs], k_d[:, cs]))
    return outs, new_states


def _hgrn_kernel(*refs, reverse, finish):
    if finish:
        q_ref, f_ref, v_ref, lb_ref, of_ref, g_ref, gain_ref, o_ref, st_ref = refs
    else:
        q_ref, f_ref, v_ref, lb_ref, o_ref, st_ref = refs
    C, SB = HG_CHUNK, HG_SUB
    nchunk = HG_TT // C

    @pl.when(pl.program_id(1) == 0)
    def _():
        st_ref[...] = jnp.zeros_like(st_ref)

    row = lax.broadcasted_iota(jnp.int32, (C, C), 0)
    col = lax.broadcasted_iota(jnp.int32, (C, C), 1)
    tri = ((col >= row) if reverse else (col <= row)).astype(BF16)
    s_i = lax.broadcasted_iota(jnp.int32, (SB, SB, HG_DK), 0).astype(F32).astype(BF16)
    t_i = lax.broadcasted_iota(jnp.int32, (SB, SB, HG_DK), 1).astype(F32).astype(BF16)
    pair_mask = (s_i >= t_i) if reverse else (s_i <= t_i)
    ones = jnp.ones((HG_DK, HG_DV), BF16)

    def chunk(ci, carry):
        c = (nchunk - 1 - ci) if reverse else ci
        sl = pl.ds(pl.multiple_of(c * C, C), C)
        outs, states = _hgrn_chunk(q_ref[sl, :], f_ref[sl, :], v_ref[sl, :], lb_ref[0],
                                   [st_ref[h] for h in range(HG_HEADS)], tri, pair_mask, ones, reverse=reverse)
        for h in range(HG_HEADS):
            cs = slice(h * HG_DK, (h + 1) * HG_DK)
            o, st_ref[h] = outs[h], states[h]
            if finish:
                tot = of_ref[sl, cs] + o
                ms = jnp.mean(tot * tot, axis=-1, keepdims=True)
                g = g_ref[sl, cs]
                y = tot * lax.rsqrt(ms + EPS) * gain_ref[...] * (g * jax.nn.sigmoid(g))
                o_ref[sl, cs] = y.astype(o_ref.dtype)
            else:
                o_ref[sl, cs] = o
        return carry

    lax.fori_loop(0, nchunk, chunk, 0)


def _hgrn_pass(z, lb, batch, seq, *, reverse, o_fwd=None, gain=None):
    n = z.shape[0]
    nt = seq // HG_TT
    finish = o_fwd is not None
    d = 1 if reverse else 0

    def tmap(t):
        return (nt - 1 - t) if reverse else t

    def zspec(col0, per_dir=True):
        blk = col0 // HG_HEADS + (d if per_dir else 0)
        return pl.BlockSpec((HG_TT, HG_WIDTH), lambda b, t: (b * nt + tmap(t), blk))

    ospec = pl.BlockSpec((HG_TT, HG_WIDTH), lambda b, t: (b * nt + tmap(t), 0))
    in_specs = [zspec(_COL_HG_Q), zspec(_COL_HG_F), zspec(_COL_HG_I),
                pl.BlockSpec((1, 1, HG_WIDTH), lambda b, t: (d, 0, 0))]
    args = [z, z, z, lb]
    if finish:
        in_specs += [ospec, zspec(_COL_HG_G, per_dir=False), pl.BlockSpec((1, HG_DV), lambda b, t: (0, 0))]
        args += [o_fwd, z, gain]
    return pl.pallas_call(
        functools.partial(_hgrn_kernel, reverse=reverse, finish=finish),
        grid=(batch, nt),
        in_specs=in_specs,
        out_specs=ospec,
        out_shape=jax.ShapeDtypeStruct((n, HG_WIDTH), BF16 if finish else F32),
        scratch_shapes=[pltpu.VMEM((HG_HEADS, HG_DV, HG_DK), F32)],
        compiler_params=pltpu.CompilerParams(
            dimension_semantics=("parallel", "arbitrary"),
            vmem_limit_bytes=V7X_VMEM_LIMIT),
        name="hgrn_bwd" if reverse else "hgrn_fwd",
    )(*args)


NA_BAND = 8
NA_BT = NA_BAND * GRID_W


def _na_bias_table(rpb):
    c = jnp.arange(GRID_W)
    c0 = jnp.clip(c - NA_WIN_C // 2, 0, GRID_W - NA_WIN_C)
    kc = jnp.arange(GRID_W)
    valid = (kc[None, :] >= c0[:, None]) & (kc[None, :] < c0[:, None] + NA_WIN_C)
    off = jnp.clip(kc[None, :] - c[:, None] + NA_WIN_C - 1, 0, 2 * NA_WIN_C - 2)
    t = jnp.where(valid[None, None], rpb.astype(F32)[:, :, off], NEG_BIG)
    t = t.reshape(NA_HEADS // 2, 2, 2 * NA_WIN_R - 1, GRID_W, GRID_W)
    t = t.transpose(0, 2, 4, 1, 3)
    return t.reshape(NA_HEADS // 2, (2 * NA_WIN_R - 1) * GRID_W, 2 * GRID_W)


def _na_kernel(q_ref, k0_ref, k1_ref, k2_ref, v0_ref, v1_ref, v2_ref, bias_ref, o_ref,
               kcat_ref, vcat_ref, *, rows):
    g = pl.program_id(1)
    for i, (kr, vr) in enumerate(((k0_ref, v0_ref), (k1_ref, v1_ref), (k2_ref, v2_ref))):
        kcat_ref[i * NA_BT:(i + 1) * NA_BT, :] = kr[...].astype(BF16)
        vcat_ref[i * NA_BT:(i + 1) * NA_BT, :] = vr[...].astype(BF16)
    lane = lax.broadcasted_iota(jnp.int32, (1, 2 * NA_HEAD_DIM), 1)
    left = lane < NA_HEAD_DIM
    nkeys = NA_WIN_R * GRID_W
    scale = NA_HEAD_DIM ** -0.5

    def body(j, carry):
        r = g * NA_BAND + j
        r0 = jnp.clip(r - NA_WIN_R // 2, 0, rows - NA_WIN_R)
        koff = pl.multiple_of((r0 - g * NA_BAND + NA_BAND) * GRID_W, GRID_W)
        boff = pl.multiple_of((r0 - r + NA_WIN_R - 1) * GRID_W, GRID_W)
        qsl = pl.ds(pl.multiple_of(j * GRID_W, GRID_W), GRID_W)
        qb = (q_ref[qsl, :] * scale).astype(BF16)
        for hp in range(NA_HEADS // 2):
            cs = slice(hp * 128, (hp + 1) * 128)
            q2 = qb[:, cs]
            zero = jnp.zeros_like(q2)
            rhs_t = jnp.concatenate([jnp.where(left, q2, zero), jnp.where(left, zero, q2)], axis=0)
            s_t = _dot_nt(kcat_ref[pl.ds(koff, nkeys), cs], rhs_t)
            s_t = s_t + bias_ref[hp, pl.ds(boff, nkeys), :]
            m = jnp.max(s_t, axis=0, keepdims=True)
            p = jnp.exp(s_t - m)
            l = jnp.sum(p, axis=0, keepdims=True)
            p = (p * (1.0 / l)).astype(BF16)
            o2 = _dot_tn(p, vcat_ref[pl.ds(koff, nkeys), cs])
            o = jnp.where(left, o2[:GRID_W], o2[GRID_W:])
            o_ref[qsl, cs] = o.astype(o_ref.dtype)
        return carry

    lax.fori_loop(0, NA_BAND, body, 0)


def _na(z, bias_tbl, batch, seq):
    n = z.shape[0]
    rows = seq // GRID_W
    nb = rows // NA_BAND

    def kv(colblk, shift):
        return pl.BlockSpec(
            (NA_BT, NA_WIDTH),
            lambda b, g: (b * nb + jnp.clip(g + shift, 0, nb - 1), colblk))

    return pl.pallas_call(
        functools.partial(_na_kernel, rows=rows),
        grid=(batch, nb),
        in_specs=[pl.BlockSpec((NA_BT, NA_WIDTH), lambda b, g: (b * nb + g, 0)),
                  kv(1, -1), kv(1, 0), kv(1, 1), kv(2, -1), kv(2, 0), kv(2, 1),
                  pl.BlockSpec(bias_tbl.shape, lambda b, g: (0, 0, 0))],
        out_specs=pl.BlockSpec((NA_BT, NA_WIDTH), lambda b, g: (b * nb + g, 0)),
        out_shape=jax.ShapeDtypeStruct((n, NA_WIDTH), BF16),
        scratch_shapes=[pltpu.VMEM((3 * NA_BT, NA_WIDTH), BF16),
                        pltpu.VMEM((3 * NA_BT, NA_WIDTH), BF16)],
        compiler_params=pltpu.CompilerParams(
            dimension_semantics=("parallel", "arbitrary"), vmem_limit_bytes=V7X_VMEM_LIMIT),
        name="natten",
    )(z, z, z, z, z, z, z, bias_tbl)


MG_TM = 512


def _merge_kernel(x_ref, oa_ref, ob_ref, ga_ref, gb_ref, wa_ref, wb_ref, wo_ref, nf_ref,
                  x2_ref, xn_ref):
    a = _dot(oa_ref[...], wa_ref[...])
    b = _dot(ob_ref[...], wb_ref[...])
    mix = jax.nn.sigmoid(ga_ref[...]) * a + jax.nn.sigmoid(gb_ref[...]) * b
    x2 = x_ref[...] + _dot(mix.astype(BF16), wo_ref[...])
    x2_ref[...] = x2
    ms = jnp.mean(x2 * x2, axis=-1, keepdims=True)
    xn_ref[...] = (x2 * lax.rsqrt(ms + EPS) * nf_ref[...]).astype(BF16)


def _merge(x2d, o_a, o_b, z, wa, wb, wo, norm_ffn):
    n = x2d.shape[0]
    col_ga = (_COL_HG_G * 128 + HG_WIDTH) // D_MODEL
    full = lambda shape: pl.BlockSpec(shape, lambda i: (0, 0))
    tok = lambda w, c=0: pl.BlockSpec((MG_TM, w), lambda i: (i, c))
    return pl.pallas_call(
        _merge_kernel,
        grid=(n // MG_TM,),
        in_specs=[tok(D_MODEL), tok(HG_WIDTH), tok(NA_WIDTH), tok(D_MODEL, col_ga), tok(D_MODEL, col_ga + 1),
                  full((HG_WIDTH, D_MODEL)), full((NA_WIDTH, D_MODEL)), full((D_MODEL, D_MODEL)),
                  full((1, D_MODEL))],
        out_specs=[tok(D_MODEL), tok(D_MODEL)],
        out_shape=[jax.ShapeDtypeStruct((n, D_MODEL), F32), jax.ShapeDtypeStruct((n, D_MODEL), BF16)],
        compiler_params=pltpu.CompilerParams(
            dimension_semantics=("parallel",), vmem_limit_bytes=V7X_VMEM_LIMIT),
        name="merge",
    )(x2d, o_a, o_b, z, z, wa, wb, wo, norm_ffn)


PEER_TB = 512
PEER_EB = 1024
PEER_LC = 256
PEER_KC = 256


def _peer_route(xn_ref, wq_ref, keys_ref, qt_ref, s_ref, cur_ref, rank_ref, top_ref, cand_ref, theta_ref,
                cnt_ref, coef_ref, r1_ref, e1_ref):
    K = PEER_TOPK
    half_rows = wq_ref.shape[0] // 2
    for part in range(2):
        rs = slice(part * half_rows, (part + 1) * half_rows)
        qt_ref[rs, :] = _dot_nt(wq_ref[rs, :], xn_ref[...]).astype(BF16)

    def score(hp, carry):
        rows = pl.ds(pl.multiple_of(hp * PEER_D_HALF, PEER_D_HALF), PEER_D_HALF)
        s_ref[hp] = _dot(keys_ref[hp], qt_ref[rows, :])
        return carry

    lax.fori_loop(0, 2 * PEER_HEADS, score, 0)

    def extract(h, carry, *, half):
        cur_ref[...] = s_ref[2 * h + half]
        if half:
            rank_ref[...] = jnp.full(rank_ref.shape, float(K), F32)

        def step(it, carry2):
            cur = cur_ref[...]
            m = jnp.max(cur, axis=0, keepdims=True)
            top_ref[half, it, pl.ds(h, 1), :] = m
            hit = cur == m
            cur_ref[...] = jnp.where(hit, -jnp.inf, cur)
            if half:
                rank_ref[...] = jnp.where(hit, jnp.asarray(it, F32), rank_ref[...])
            return carry2

        lax.fori_loop(0, K, step, 0)
        if half:
            r1_ref[h] = rank_ref[...].astype(BF16)
        return carry

    lax.fori_loop(0, PEER_HEADS, functools.partial(extract, half=0), 0)
    lax.fori_loop(0, PEER_HEADS, functools.partial(extract, half=1), 0)

    ncand = len(_CAND_PAIRS)
    for c, (p, q) in enumerate(_CAND_PAIRS):
        cand_ref[c] = top_ref[0, p] + top_ref[1, q]
    small = cand_ref.shape[1:]

    def tau_step(it, carry):
        tau, cnt = carry
        m = cand_ref[0]
        for c in range(1, ncand):
            m = jnp.maximum(m, cand_ref[c])
        n_eq = jnp.zeros(small, F32)
        for c in range(ncand):
            v = cand_ref[c]
            hit = v == m
            n_eq = n_eq + jnp.where(hit, 1.0, 0.0)
            cand_ref[c] = jnp.where(hit, -jnp.inf, v)
        return jnp.where(cnt < float(K), m, tau), cnt + n_eq

    tau, _ = lax.fori_loop(0, K, tau_step, (jnp.full(small, -jnp.inf, F32), jnp.zeros(small, F32)))
    best = top_ref[0, 0] + top_ref[1, 0]
    zsum = jnp.zeros(small, F32)
    for p, q in _CAND_PAIRS:
        v = top_ref[0, p] + top_ref[1, q]
        zsum = zsum + jnp.where(v >= tau, jnp.exp(v - best), 0.0)
    inv_z = 1.0 / zsum

    for q in range(K):
        th = jnp.full(small, jnp.inf, F32)
        b_q = top_ref[1, q]
        for p in range(K):
            a_p = top_ref[0, p]
            th = jnp.minimum(th, jnp.where(a_p + b_q >= tau, a_p, jnp.inf))
        theta_ref[q] = th

    for h in range(PEER_HEADS):
        s0 = s_ref[2 * h]
        cnt = jnp.zeros(s0.shape, F32)
        for q in range(K):
            cnt = cnt + jnp.where(s0 >= theta_ref[q, h:h + 1, :], 1.0, 0.0)
        cnt_ref[h] = cnt
        coef_ref[h] = jnp.exp(s0 - top_ref[0, 0, h:h + 1, :])
        e1_ref[h] = (jnp.exp(s_ref[2 * h + 1] - top_ref[1, 0, h:h + 1, :]) * inv_z[h:h + 1]).astype(BF16)


def _peer_kernel(xn_ref, x2_ref, wq_ref, keys_ref, u_ref, vt_ref, o_ref,
                 qt_ref, s_ref, cur_ref, rank_ref, top_ref, cand_ref, theta_ref,
                 cnt_ref, coef_ref, r1_ref, e1_ref, g_ref, acc_ref):
    e = pl.program_id(1)
    ne = pl.num_programs(1)

    @pl.when(e == 0)
    def _():
        _peer_route(xn_ref, wq_ref, keys_ref, qt_ref, s_ref, cur_ref, rank_ref, top_ref, cand_ref, theta_ref,
                    cnt_ref, coef_ref, r1_ref, e1_ref)
        acc_ref[...] = jnp.zeros_like(acc_ref)

    n_i = PEER_EB // PEER_N_KEYS
    i_rows = pl.ds(pl.multiple_of(e * n_i, n_i), n_i)
    h_t = _dot_nt(u_ref[...], xn_ref[...])
    zero = jnp.zeros((PEER_N_KEYS, PEER_LC), BF16)
    for lc in range(PEER_TB // PEER_LC):
        ls = slice(lc * PEER_LC, (lc + 1) * PEER_LC)
        cnt_rows = [cnt_ref[h, i_rows, ls].astype(BF16) for h in range(PEER_HEADS)]
        coef_rows = [coef_ref[h, i_rows, ls].astype(BF16) for h in range(PEER_HEADS)]
        for ii in range(n_i):
            gate = zero
            for h in range(PEER_HEADS):
                sel = r1_ref[h, :, ls] < cnt_rows[h][ii:ii + 1]
                gate = gate + jnp.where(sel, e1_ref[h, :, ls], zero) * coef_rows[h][ii:ii + 1]
            g_ref[ii * PEER_N_KEYS:(ii + 1) * PEER_N_KEYS, ls] = gate
    y_t = None
    for c in range(PEER_EB // PEER_KC):
        rs = slice(c * PEER_KC, (c + 1) * PEER_KC)
        hh = h_t[rs]
        act = 0.5 * hh * (1.0 + lax.erf(hh * (2.0 ** -0.5)))
        part = _dot(vt_ref[:, rs], act.astype(BF16) * g_ref[rs, :])
        y_t = part if y_t is None else y_t + part
    acc_ref[...] += y_t

    @pl.when(e == ne - 1)
    def _():
        o_ref[...] = x2_ref[...] + acc_ref[...].T


def _peer(xn, x2, wq_t, keys, u_bf16, vt_bf16):
    n = xn.shape[0]
    tb, eb = PEER_TB, PEER_EB
    return pl.pallas_call(
        _peer_kernel,
        grid=(n // tb, PEER_N_EXPERTS // eb),
        in_specs=[
            pl.BlockSpec((tb, D_MODEL), lambda i, e: (i, 0)),
            pl.BlockSpec((tb, D_MODEL), lambda i, e: (i, 0)),
            pl.BlockSpec(wq_t.shape, lambda i, e: (0, 0)),
            pl.BlockSpec(keys.shape, lambda i, e: (0, 0, 0)),
            pl.BlockSpec((eb, D_MODEL), lambda i, e: (e, 0)),
            pl.BlockSpec((D_MODEL, eb), lambda i, e: (0, e)),
        ],
        out_specs=pl.BlockSpec((tb, D_MODEL), lambda i, e: (i, 0)),
        out_shape=jax.ShapeDtypeStruct((n, D_MODEL), F32),
        scratch_shapes=[
            pltpu.VMEM((wq_t.shape[0], tb), BF16),
            pltpu.VMEM((2 * PEER_HEADS, PEER_N_KEYS, tb), F32),
            pltpu.VMEM((PEER_N_KEYS, tb), F32),
            pltpu.VMEM((PEER_N_KEYS, tb), F32),
            pltpu.VMEM((2, PEER_TOPK, PEER_HEADS, tb), F32),
            pltpu.VMEM((len(_CAND_PAIRS), PEER_HEADS, tb), F32),
            pltpu.VMEM((PEER_TOPK, PEER_HEADS, tb), F32),
            pltpu.VMEM((PEER_HEADS, PEER_N_KEYS, tb), F32),
            pltpu.VMEM((PEER_HEADS, PEER_N_KEYS, tb), F32),
            pltpu.VMEM((PEER_HEADS, PEER_N_KEYS, tb), BF16),
            pltpu.VMEM((PEER_HEADS, PEER_N_KEYS, tb), BF16),
            pltpu.VMEM((eb, tb), BF16),
            pltpu.VMEM((D_MODEL, tb), F32),
        ],
        compiler_params=pltpu.CompilerParams(
            dimension_semantics=("parallel", "arbitrary"), vmem_limit_bytes=V7X_VMEM_LIMIT),
        name="peer",
    )(xn, x2, wq_t, keys, u_bf16, vt_bf16)


def _trunk(x, p):
    batch, seq, _ = x.shape
    x2d = x.reshape(batch * seq, D_MODEL)
    z = _in_proj(x2d, p["norm_mix"], p["w_in"], p["qk_gain"], p["bd"])
    o_f = _hgrn_pass(z, p["lb"], batch, seq, reverse=False)
    o_a = _hgrn_pass(z, p["lb"], batch, seq, reverse=True, o_fwd=o_f, gain=p["hg_out_norm"])
    o_b = _na(z, p["na_bias"], batch, seq)
    x2, xn = _merge(x2d, o_a, o_b, z, p["w_proj_a"], p["w_proj_b"], p["w_out"], p["norm_ffn"])
    y = _peer(xn, x2, p["wq_t"], p["keys"], p["expert_u"], p["expert_vt"])
    return y.reshape(batch, seq, D_MODEL)


def kernel(x_prompt, x_sample, norm_mix, w_in, lb_logits, hg_out_norm, q_norm, k_norm, rel_pos_bias,
           w_proj_a, w_proj_b, w_out, norm_ffn, w_query, sub_keys, expert_u, expert_v):
    l = 0
    lb_all = jnp.cumsum(jax.nn.softmax(lb_logits.astype(F32), axis=0), axis=0)
    head_id = np.arange(2 * NA_WIDTH) // NA_HEAD_DIM
    p = {
        "norm_mix": norm_mix[l].reshape(1, D_MODEL),
        "w_in": w_in[l].astype(BF16),
        "qk_gain": jnp.concatenate([jnp.tile(q_norm[l], NA_HEADS), jnp.tile(k_norm[l], NA_HEADS)]).reshape(1, -1),
        "bd": jnp.asarray(head_id[:, None] == head_id[None, :], BF16),
        "lb": lb_all[l].reshape(2, 1, HG_WIDTH),
        "hg_out_norm": hg_out_norm[l].reshape(1, HG_DV),
        "na_bias": _na_bias_table(rel_pos_bias[l]),
        "w_proj_a": w_proj_a[l].astype(BF16),
        "w_proj_b": w_proj_b[l].astype(BF16),
        "w_out": w_out[l].astype(BF16),
        "norm_ffn": norm_ffn[l].reshape(1, D_MODEL),
        "wq_t": w_query[l].T.astype(BF16),
        "keys": sub_keys[l].reshape(2 * PEER_HEADS, PEER_N_KEYS, PEER_D_HALF).astype(BF16),
        "expert_u": expert_u[l].astype(BF16),
        "expert_vt": expert_v[l].T.astype(BF16),
    }
    return (_trunk(x_prompt, p), _trunk(x_sample, p))
```

```python
import functools

import numpy as np
import jax
import jax.numpy as jnp
from jax import lax
from jax.experimental import pallas as pl
from jax.experimental.pallas import tpu as pltpu

F32 = jnp.float32
BF16 = jnp.bfloat16

D_MODEL = 1024
GRID_W = 64
EPS = 1e-6
HG_HEADS = 4
HG_DK = 128
HG_DV = 128
HG_WIDTH = HG_HEADS * HG_DV
HG_CHUNK = 64
HG_SUB = 16
NA_HEADS = 8
NA_HEAD_DIM = 64
NA_WIDTH = NA_HEADS * NA_HEAD_DIM
NA_WIN_R = 8
NA_WIN_C = 16
PEER_HEADS = 8
PEER_N_KEYS = 128
PEER_N_EXPERTS = PEER_N_KEYS * PEER_N_KEYS
PEER_D_HALF = 128
PEER_TOPK = 16
D_IN = 3 * NA_WIDTH + 3 * 2 * HG_HEADS * HG_DK + HG_WIDTH + 2 * D_MODEL

_COL_HG_Q = (3 * NA_WIDTH) // 128
_COL_HG_F = _COL_HG_Q + 8
_COL_HG_I = _COL_HG_F + 8
_COL_HG_G = _COL_HG_I + 8

V7X_VMEM_LIMIT = 56 * 1024 * 1024
NEG_BIG = -1e30

_CAND_PAIRS = tuple((p, q) for p in range(PEER_TOPK) for q in range(PEER_TOPK)
                    if (p + 1) * (q + 1) <= PEER_TOPK)


def _dot(a, b):
    return jnp.dot(a, b, preferred_element_type=F32)


def _dot_nt(a, b):
    return lax.dot_general(a, b, (((1,), (1,)), ((), ())), preferred_element_type=F32)


def _dot_tn(a, b):
    return lax.dot_general(a, b, (((0,), (0,)), ((), ())), preferred_element_type=F32)


def _split_bf16(x):
    hi = x.astype(BF16)
    lo = (x - hi.astype(F32)).astype(BF16)
    return hi, lo


IN_TM = 256
IN_TN = 1024


def _in_proj_kernel(x_ref, g_ref, w_ref, qkg_ref, red_ref, exp_ref, z_ref):
    x = x_ref[...]
    ms = jnp.mean(x * x, axis=-1, keepdims=True)
    xn = (x * lax.rsqrt(ms + EPS) * g_ref[...]).astype(BF16)
    for j in range(D_IN // IN_TN):
        cs = slice(j * IN_TN, (j + 1) * IN_TN)
        z = _dot(xn, w_ref[:, cs])
        if j == 0:
            hi, lo = _split_bf16(z * z)
            hi, lo = _split_bf16(_dot(hi, red_ref[...]) + _dot(lo, red_ref[...]))
            ss = _dot(hi, exp_ref[...]) + _dot(lo, exp_ref[...])
            z = z * lax.rsqrt(ss * (1.0 / NA_HEAD_DIM) + EPS) * qkg_ref[...]
        z_ref[:, cs] = z


def _in_proj(x2d, gain, w_bf16, qk_gain, head_red, head_exp):
    n = x2d.shape[0]
    const = lambda shape: pl.BlockSpec(shape, lambda i: (0, 0), pipeline_mode=pl.Buffered(1))
    return pl.pallas_call(
        _in_proj_kernel,
        grid=(n // IN_TM,),
        in_specs=[
            pl.BlockSpec((IN_TM, D_MODEL), lambda i: (i, 0)),
            const((1, D_MODEL)),
            const((D_MODEL, D_IN)),
            const((1, IN_TN)),
            const(head_red.shape),
            const(head_exp.shape),
        ],
        out_specs=pl.BlockSpec((IN_TM, D_IN), lambda i: (i, 0)),
        out_shape=jax.ShapeDtypeStruct((n, D_IN), F32),
        compiler_params=pltpu.CompilerParams(
            dimension_semantics=("parallel",), vmem_limit_bytes=V7X_VMEM_LIMIT),
        name="in_proj",
    )(x2d, gain, w_bf16, qk_gain, head_red, head_exp)


HG_TT = 512


def _hgrn_chunk(q, f_logit, v, lb, states, tri, pair_mask, ones, *, reverse):
    C, SB = HG_CHUNK, HG_SUB
    nblk = C // SB
    f = lb + (1.0 - lb) * jax.nn.sigmoid(f_logit)
    qs = q * jax.nn.sigmoid(q)
    hi, lo = _split_bf16(jnp.log(f))
    b = _dot(tri, hi) + _dot(tri, lo)
    c = b - jnp.log(1.0 - f)
    vb = v.astype(BF16)
    q_all = (qs * jnp.exp(b)).astype(BF16)
    b_end = b[0:1] if reverse else b[C - 1:C]
    k_d = jnp.exp(b_end - c).astype(BF16)
    carry_decay = jnp.exp(b_end)

    prevs, q_s, k_s = [], [], []
    for blk in range(nblk):
        lo_r, hi_r = blk * SB, (blk + 1) * SB
        if reverse and blk < nblk - 1:
            prev, ref = slice(hi_r, C), b[hi_r:hi_r + 1]
        elif (not reverse) and blk > 0:
            prev, ref = slice(0, lo_r), b[lo_r - 1:lo_r]
        else:
            prev = None
        prevs.append(prev)
        q_s.append(None if prev is None else (qs[lo_r:hi_r] * jnp.exp(b[lo_r:hi_r] - ref)).astype(BF16))
        k_s.append(None if prev is None else jnp.exp(ref - c[prev]).astype(BF16))

    outs, new_states = [], []
    for h, state in enumerate(states):
        cs = slice(h * HG_DK, (h + 1) * HG_DK)
        parts = []
        for blk in range(nblk):
            rows = slice(blk * SB, (blk + 1) * SB)
            prod = (qs[rows, cs][None, :, :] * jnp.exp(b[rows, cs][None, :, :] - c[rows, cs][:, None, :]))
            prod = prod.astype(BF16)
            prod = jnp.where(pair_mask, prod, jnp.zeros_like(prod))
            rs = _dot(prod.reshape(SB * SB, HG_DK), ones)
            o_i = jnp.sum(rs.reshape(SB, SB, HG_DV) * v[rows, cs][:, None, :], axis=0)
            if prevs[blk] is not None:
                s = _dot_nt(q_s[blk][:, cs], k_s[blk][:, cs])
                o_i = o_i + _dot(s.astype(BF16), vb[prevs[blk], cs])
            parts.append(o_i)
        o_inter = _dot_nt(q_all[:, cs], state.astype(BF16))
        outs.append(o_inter + jnp.concatenate(parts, axis=0))
        new_states.append(state * carry_decay[:, cs] + _dot_tn(vb[:, cs], k_d[:, cs]))
    return outs, new_states


def _hgrn_kernel(*refs, reverse, finish):
    if finish:
        q_ref, f_ref, v_ref, lb_ref, of_ref, g_ref, gain_ref, o_ref, st_ref = refs
    else:
        q_ref, f_ref, v_ref, lb_ref, o_ref, st_ref = refs
    C, SB = HG_CHUNK, HG_SUB
    nchunk = HG_TT // C

    @pl.when(pl.program_id(1) == 0)
    def _():
        st_ref[...] = jnp.zeros_like(st_ref)

    row = lax.broadcasted_iota(jnp.int32, (C, C), 0)
    col = lax.broadcasted_iota(jnp.int32, (C, C), 1)
    tri = ((col >= row) if reverse else (col <= row)).astype(BF16)
    s_i = lax.broadcasted_iota(jnp.int32, (SB, SB, HG_DK), 0).astype(F32).astype(BF16)
    t_i = lax.broadcasted_iota(jnp.int32, (SB, SB, HG_DK), 1).astype(F32).astype(BF16)
    pair_mask = (s_i >= t_i) if reverse else (s_i <= t_i)
    ones = jnp.ones((HG_DK, HG_DV), BF16)

    def chunk(ci, carry):
        c = (nchunk - 1 - ci) if reverse else ci
        sl = pl.ds(pl.multiple_of(c * C, C), C)
        outs, states = _hgrn_chunk(q_ref[sl, :], f_ref[sl, :], v_ref[sl, :], lb_ref[0],
                                   [st_ref[h] for h in range(HG_HEADS)], tri, pair_mask, ones, reverse=reverse)
        for h in range(HG_HEADS):
            cs = slice(h * HG_DK, (h + 1) * HG_DK)
            o, st_ref[h] = outs[h], states[h]
            if finish:
                tot = of_ref[sl, cs] + o
                ms = jnp.mean(tot * tot, axis=-1, keepdims=True)
                g = g_ref[sl, cs]
                y = tot * lax.rsqrt(ms + EPS) * gain_ref[...] * (g * jax.nn.sigmoid(g))
                o_ref[sl, cs] = y.astype(o_ref.dtype)
            else:
                o_ref[sl, cs] = o
        return carry

    lax.fori_loop(0, nchunk, chunk, 0)


def _hgrn_pass(z, lb, batch, seq, *, reverse, o_fwd=None, gain=None):
    n = z.shape[0]
    nt = seq // HG_TT
    finish = o_fwd is not None
    d = 1 if reverse else 0

    def tmap(t):
        return (nt - 1 - t) if reverse else t

    def zspec(col0, per_dir=True):
        blk = col0 // HG_HEADS + (d if per_dir else 0)
        return pl.BlockSpec((HG_TT, HG_WIDTH), lambda b, t: (b * nt + tmap(t), blk))

    ospec = pl.BlockSpec((HG_TT, HG_WIDTH), lambda b, t: (b * nt + tmap(t), 0))
    in_specs = [zspec(_COL_HG_Q), zspec(_COL_HG_F), zspec(_COL_HG_I),
                pl.BlockSpec((1, 1, HG_WIDTH), lambda b, t: (d, 0, 0))]
    args = [z, z, z, lb]
    if finish:
        in_specs += [ospec, zspec(_COL_HG_G, per_dir=False), pl.BlockSpec((1, HG_DV), lambda b, t: (0, 0))]
        args += [o_fwd, z, gain]
    return pl.pallas_call(
        functools.partial(_hgrn_kernel, reverse=reverse, finish=finish),
        grid=(batch, nt),
        in_specs=in_specs,
        out_specs=ospec,
        out_shape=jax.ShapeDtypeStruct((n, HG_WIDTH), BF16 if finish else F32),
        scratch_shapes=[pltpu.VMEM((HG_HEADS, HG_DV, HG_DK), F32)],
        compiler_params=pltpu.CompilerParams(
            dimension_semantics=("parallel", "arbitrary"),
            vmem_limit_bytes=V7X_VMEM_LIMIT),
        name="hgrn_bwd" if reverse else "hgrn_fwd",
    )(*args)


NA_BAND = 8
NA_BT = NA_BAND * GRID_W


def _na_bias_table(rpb):
    c = jnp.arange(GRID_W)
    c0 = jnp.clip(c - NA_WIN_C // 2, 0, GRID_W - NA_WIN_C)
    kc = jnp.arange(GRID_W)
    valid = (kc[None, :] >= c0[:, None]) & (kc[None, :] < c0[:, None] + NA_WIN_C)
    off = jnp.clip(kc[None, :] - c[:, None] + NA_WIN_C - 1, 0, 2 * NA_WIN_C - 2)
    t = jnp.where(valid[None, None], rpb.astype(F32)[:, :, off], NEG_BIG)
    t = t.reshape(NA_HEADS // 2, 2, 2 * NA_WIN_R - 1, GRID_W, GRID_W)
    t = t.transpose(0, 2, 4, 1, 3)
    return t.reshape(NA_HEADS // 2, (2 * NA_WIN_R - 1) * GRID_W, 2 * GRID_W)


def _na_kernel(q_ref, k0_ref, k1_ref, k2_ref, v0_ref, v1_ref, v2_ref, bias_ref, o_ref,
               kcat_ref, vcat_ref, *, rows):
    g = pl.program_id(1)
    for i, (kr, vr) in enumerate(((k0_ref, v0_ref), (k1_ref, v1_ref), (k2_ref, v2_ref))):
        kcat_ref[i * NA_BT:(i + 1) * NA_BT, :] = kr[...].astype(BF16)
        vcat_ref[i * NA_BT:(i + 1) * NA_BT, :] = vr[...].astype(BF16)
    lane = lax.broadcasted_iota(jnp.int32, (1, 2 * NA_HEAD_DIM), 1)
    left = lane < NA_HEAD_DIM
    nkeys = NA_WIN_R * GRID_W
    scale = NA_HEAD_DIM ** -0.5

    def body(j, carry):
        r = g * NA_BAND + j
        r0 = jnp.clip(r - NA_WIN_R // 2, 0, rows - NA_WIN_R)
        koff = pl.multiple_of((r0 - g * NA_BAND + NA_BAND) * GRID_W, GRID_W)
        boff = pl.multiple_of((r0 - r + NA_WIN_R - 1) * GRID_W, GRID_W)
        qsl = pl.ds(pl.multiple_of(j * GRID_W, GRID_W), GRID_W)
        qb = (q_ref[qsl, :] * scale).astype(BF16)
        for hp in range(NA_HEADS // 2):
            cs = slice(hp * 128, (hp + 1) * 128)
            q2 = qb[:, cs]
            zero = jnp.zeros_like(q2)
            rhs_t = jnp.concatenate([jnp.where(left, q2, zero), jnp.where(left, zero, q2)], axis=0)
            s_t = _dot_nt(kcat_ref[pl.ds(koff, nkeys), cs], rhs_t)
            s_t = s_t + bias_ref[hp, pl.ds(boff, nkeys), :]
            m = jnp.max(s_t, axis=0, keepdims=True)
            p = jnp.exp(s_t - m)
            l = jnp.sum(p, axis=0, keepdims=True)
            p = (p * (1.0 / l)).astype(BF16)
            o2 = _dot_tn(p, vcat_ref[pl.ds(koff, nkeys), cs])
            o = jnp.where(left, o2[:GRID_W], o2[GRID_W:])
            o_ref[qsl, cs] = o.astype(o_ref.dtype)
        return carry

    lax.fori_loop(0, NA_BAND, body, 0)


def _na(z, bias_tbl, batch, seq):
    n = z.shape[0]
    rows = seq // GRID_W
    nb = rows // NA_BAND

    def kv(colblk, shift):
        return pl.BlockSpec(
            (NA_BT, NA_WIDTH),
            lambda b, g: (b * nb + jnp.clip(g + shift, 0, nb - 1), colblk))

    return pl.pallas_call(
        functools.partial(_na_kernel, rows=rows),
        grid=(batch, nb),
        in_specs=[pl.BlockSpec((NA_BT, NA_WIDTH), lambda b, g: (b * nb + g, 0)),
                  kv(1, -1), kv(1, 0), kv(1, 1), kv(2, -1), kv(2, 0), kv(2, 1),
                  pl.BlockSpec(bias_tbl.shape, lambda b, g: (0, 0, 0))],
        out_specs=pl.BlockSpec((NA_BT, NA_WIDTH), lambda b, g: (b * nb + g, 0)),
        out_shape=jax.ShapeDtypeStruct((n, NA_WIDTH), BF16),
        scratch_shapes=[pltpu.VMEM((3 * NA_BT, NA_WIDTH), BF16),
                        pltpu.VMEM((3 * NA_BT, NA_WIDTH), BF16)],
        compiler_params=pltpu.CompilerParams(
            dimension_semantics=("parallel", "arbitrary"), vmem_limit_bytes=V7X_VMEM_LIMIT),
        name="natten",
    )(z, z, z, z, z, z, z, bias_tbl)


MG_TM = 512


def _merge_kernel(x_ref, oa_ref, ob_ref, ga_ref, gb_ref, wa_ref, wb_ref, wo_ref, nf_ref,
                  x2_ref, xt_ref):
    a = _dot(oa_ref[...], wa_ref[...])
    b = _dot(ob_ref[...], wb_ref[...])
    mix = jax.nn.sigmoid(ga_ref[...]) * a + jax.nn.sigmoid(gb_ref[...]) * b
    x2 = x_ref[...] + _dot(mix.astype(BF16), wo_ref[...])
    x2_ref[...] = x2
    ms = jnp.mean(x2 * x2, axis=-1, keepdims=True)
    xt_ref[...] = (x2 * lax.rsqrt(ms + EPS) * nf_ref[...]).T.astype(BF16)


def _merge(x2d, o_a, o_b, z, wa, wb, wo, norm_ffn):
    n = x2d.shape[0]
    col_ga = (_COL_HG_G * 128 + HG_WIDTH) // D_MODEL
    full = lambda shape: pl.BlockSpec(shape, lambda i: (0, 0))
    tok = lambda w, c=0: pl.BlockSpec((MG_TM, w), lambda i: (i, c))
    return pl.pallas_call(
        _merge_kernel,
        grid=(n // MG_TM,),
        in_specs=[tok(D_MODEL), tok(HG_WIDTH), tok(NA_WIDTH), tok(D_MODEL, col_ga), tok(D_MODEL, col_ga + 1),
                  full((HG_WIDTH, D_MODEL)), full((NA_WIDTH, D_MODEL)), full((D_MODEL, D_MODEL)),
                  full((1, D_MODEL))],
        out_specs=[tok(D_MODEL), pl.BlockSpec((D_MODEL, MG_TM), lambda i: (0, i))],
        out_shape=[jax.ShapeDtypeStruct((n, D_MODEL), F32), jax.ShapeDtypeStruct((D_MODEL, n), BF16)],
        compiler_params=pltpu.CompilerParams(
            dimension_semantics=("parallel",), vmem_limit_bytes=V7X_VMEM_LIMIT),
        name="merge",
    )(x2d, o_a, o_b, z, z, wa, wb, wo, norm_ffn)


PEER_TB = 512
PEER_EB = 1024
PEER_LC = 256
PEER_KC = 256


def _peer_route(xt_ref, wq_ref, keys_ref, qt_ref, s_ref, cur_ref, rank_ref, top_ref, cand_ref, theta_ref,
                cnt_ref, coef_ref, r1_ref, e1_ref):
    K = PEER_TOPK
    half_rows = wq_ref.shape[0] // 2
    for part in range(2):
        rs = slice(part * half_rows, (part + 1) * half_rows)
        qt_ref[rs, :] = _dot(wq_ref[rs, :], xt_ref[...]).astype(BF16)

    def score(hp, carry):
        rows = pl.ds(pl.multiple_of(hp * PEER_D_HALF, PEER_D_HALF), PEER_D_HALF)
        s_ref[hp] = _dot(keys_ref[hp], qt_ref[rows, :])
        return carry

    lax.fori_loop(0, 2 * PEER_HEADS, score, 0)

    def extract(h, carry, *, half):
        cur_ref[...] = s_ref[2 * h + half]
        if half:
            rank_ref[...] = jnp.full(rank_ref.shape, float(K), F32)

        def step(it, carry2):
            cur = cur_ref[...]
            m = jnp.max(cur, axis=0, keepdims=True)
            top_ref[half, it, pl.ds(h, 1), :] = m
            hit = cur == m
            cur_ref[...] = jnp.where(hit, -jnp.inf, cur)
            if half:
                rank_ref[...] = jnp.where(hit, jnp.asarray(it, F32), rank_ref[...])
            return carry2

        lax.fori_loop(0, K, step, 0)
        if half:
            r1_ref[h] = rank_ref[...].astype(BF16)
        return carry

    lax.fori_loop(0, PEER_HEADS, functools.partial(extract, half=0), 0)
    lax.fori_loop(0, PEER_HEADS, functools.partial(extract, half=1), 0)

    ncand = len(_CAND_PAIRS)
    for c, (p, q) in enumerate(_CAND_PAIRS):
        cand_ref[c] = top_ref[0, p] + top_ref[1, q]
    small = cand_ref.shape[1:]

    def tau_step(it, carry):
        tau, cnt = carry
        m = cand_ref[0]
        for c in range(1, ncand):
            m = jnp.maximum(m, cand_ref[c])
        n_eq = jnp.zeros(small, F32)
        for c in range(ncand):
            v = cand_ref[c]
            hit = v == m
            n_eq = n_eq + jnp.where(hit, 1.0, 0.0)
            cand_ref[c] = jnp.where(hit, -jnp.inf, v)
        return jnp.where(cnt < float(K), m, tau), cnt + n_eq

    tau, _ = lax.fori_loop(0, K, tau_step, (jnp.full(small, -jnp.inf, F32), jnp.zeros(small, F32)))
    best = top_ref[0, 0] + top_ref[1, 0]
    zsum = jnp.zeros(small, F32)
    for p, q in _CAND_PAIRS:
        v = top_ref[0, p] + top_ref[1, q]
        zsum = zsum + jnp.where(v >= tau, jnp.exp(v - best), 0.0)
    inv_z = 1.0 / zsum

    for q in range(K):
        th = jnp.full(small, jnp.inf, F32)
        b_q = top_ref[1, q]
        for p in range(K):
            a_p = top_ref[0, p]
            th = jnp.minimum(th, jnp.where(a_p + b_q >= tau, a_p, jnp.inf))
        theta_ref[q] = th

    for h in range(PEER_HEADS):
        s0 = s_ref[2 * h]
        cnt = jnp.zeros(s0.shape, F32)
        for q in range(K):
            cnt = cnt + jnp.where(s0 >= theta_ref[q, h:h + 1, :], 1.0, 0.0)
        cnt_ref[h] = cnt
        coef_ref[h] = jnp.exp(s0 - top_ref[0, 0, h:h + 1, :])
        e1_ref[h] = (jnp.exp(s_ref[2 * h + 1] - top_ref[1, 0, h:h + 1, :]) * inv_z[h:h + 1]).astype(BF16)


def _peer_kernel(xt_ref, x2_ref, wq_ref, keys_ref, u_ref, vt_ref, o_ref,
                 qt_ref, s_ref, cur_ref, rank_ref, top_ref, cand_ref, theta_ref,
                 cnt_ref, coef_ref, r1_ref, e1_ref, g_ref, acc_ref):
    e = pl.program_id(1)
    ne = pl.num_programs(1)

    @pl.when(e == 0)
    def _():
        _peer_route(xt_ref, wq_ref, keys_ref, qt_ref, s_ref, cur_ref, rank_ref, top_ref, cand_ref, theta_ref,
                    cnt_ref, coef_ref, r1_ref, e1_ref)
        acc_ref[...] = jnp.zeros_like(acc_ref)

    n_i = PEER_EB // PEER_N_KEYS
    i_rows = pl.ds(pl.multiple_of(e * n_i, n_i), n_i)
    h_t = _dot(u_ref[...], xt_ref[...])
    zero = jnp.zeros((PEER_N_KEYS, PEER_LC), BF16)
    for lc in range(PEER_TB // PEER_LC):
        ls = slice(lc * PEER_LC, (lc + 1) * PEER_LC)
        cnt_rows = [cnt_ref[h, i_rows, ls].astype(BF16) for h in range(PEER_HEADS)]
        coef_rows = [coef_ref[h, i_rows, ls].astype(BF16) for h in range(PEER_HEADS)]
        for ii in range(n_i):
            gate = zero
            for h in range(PEER_HEADS):
                sel = r1_ref[h, :, ls] < cnt_rows[h][ii:ii + 1]
                gate = gate + jnp.where(sel, e1_ref[h, :, ls], zero) * coef_rows[h][ii:ii + 1]
            g_ref[ii * PEER_N_KEYS:(ii + 1) * PEER_N_KEYS, ls] = gate
    y_t = None
    for c in range(PEER_EB // PEER_KC):
        rs = slice(c * PEER_KC, (c + 1) * PEER_KC)
        hh = h_t[rs]
        act = 0.5 * hh * (1.0 + lax.erf(hh * (2.0 ** -0.5)))
        part = _dot(vt_ref[:, rs], act.astype(BF16) * g_ref[rs, :])
        y_t = part if y_t is None else y_t + part
    acc_ref[...] += y_t

    @pl.when(e == ne - 1)
    def _():
        o_ref[...] = x2_ref[...] + acc_ref[...].T


def _peer(xt, x2, wq_t, keys, u_bf16, vt_bf16):
    n = x2.shape[0]
    tb, eb = PEER_TB, PEER_EB
    ne = PEER_N_EXPERTS // eb
    return pl.pallas_call(
        _peer_kernel,
        grid=(n // tb, PEER_N_EXPERTS // eb),
        in_specs=[
            pl.BlockSpec((D_MODEL, tb), lambda i, e: (0, i)),
            pl.BlockSpec((tb, D_MODEL), lambda i, e: (i, 0)),
            pl.BlockSpec(wq_t.shape, lambda i, e: (0, 0)),
            pl.BlockSpec(keys.shape, lambda i, e: (0, 0, 0)),
            pl.BlockSpec((eb, D_MODEL), lambda i, e: (e, 0)),
            pl.BlockSpec((D_MODEL, eb), lambda i, e: (0, e)),
        ],
        out_specs=pl.BlockSpec((tb, D_MODEL), lambda i, e: (i, 0)),
        out_shape=jax.ShapeDtypeStruct((n, D_MODEL), F32),
        scratch_shapes=[
            pltpu.VMEM((wq_t.shape[0], tb), BF16),
            pltpu.VMEM((2 * PEER_HEADS, PEER_N_KEYS, tb), F32),
            pltpu.VMEM((PEER_N_KEYS, tb), F32),
            pltpu.VMEM((PEER_N_KEYS, tb), F32),
            pltpu.VMEM((2, PEER_TOPK, PEER_HEADS, tb), F32),
            pltpu.VMEM((len(_CAND_PAIRS), PEER_HEADS, tb), F32),
            pltpu.VMEM((PEER_TOPK, PEER_HEADS, tb), F32),
            pltpu.VMEM((PEER_HEADS, PEER_N_KEYS, tb), F32),
            pltpu.VMEM((PEER_HEADS, PEER_N_KEYS, tb), F32),
            pltpu.VMEM((PEER_HEADS, PEER_N_KEYS, tb), BF16),
            pltpu.VMEM((PEER_HEADS, PEER_N_KEYS, tb), BF16),
            pltpu.VMEM((eb, tb), BF16),
            pltpu.VMEM((D_MODEL, tb), F32),
        ],
        compiler_params=pltpu.CompilerParams(
            dimension_semantics=("parallel", "arbitrary"), vmem_limit_bytes=V7X_VMEM_LIMIT),
        name="peer",
    )(xt, x2, wq_t, keys, u_bf16, vt_bf16)


def _trunk(x, p):
    batch, seq, _ = x.shape
    x2d = x.reshape(batch * seq, D_MODEL)
    z = _in_proj(x2d, p["norm_mix"], p["w_in"], p["qk_gain"], p["head_red"], p["head_exp"])
    o_f = _hgrn_pass(z, p["lb"], batch, seq, reverse=False)
    o_a = _hgrn_pass(z, p["lb"], batch, seq, reverse=True, o_fwd=o_f, gain=p["hg_out_norm"])
    o_b = _na(z, p["na_bias"], batch, seq)
    x2, xt = _merge(x2d, o_a, o_b, z, p["w_proj_a"], p["w_proj_b"], p["w_out"], p["norm_ffn"])
    y = _peer(xt, x2, p["wq_t"], p["keys"], p["expert_u"], p["expert_vt"])
    return y.reshape(batch, seq, D_MODEL)


def kernel(x_prompt, x_sample, norm_mix, w_in, lb_logits, hg_out_norm, q_norm, k_norm, rel_pos_bias,
           w_proj_a, w_proj_b, w_out, norm_ffn, w_query, sub_keys, expert_u, expert_v):
    l = 0
    lb_all = jnp.cumsum(jax.nn.softmax(lb_logits.astype(F32), axis=0), axis=0)
    head_id = np.arange(2 * NA_WIDTH) // NA_HEAD_DIM
    p = {
        "norm_mix": norm_mix[l].reshape(1, D_MODEL),
        "w_in": w_in[l].astype(BF16),
        "qk_gain": jnp.concatenate([jnp.tile(q_norm[l], NA_HEADS), jnp.tile(k_norm[l], NA_HEADS)]).reshape(1, -1),
        "head_red": jnp.asarray(head_id[:, None] == np.arange(128)[None, :], BF16),
        "head_exp": jnp.asarray(np.arange(128)[:, None] == head_id[None, :], BF16),
        "lb": lb_all[l].reshape(2, 1, HG_WIDTH),
        "hg_out_norm": hg_out_norm[l].reshape(1, HG_DV),
        "na_bias": _na_bias_table(rel_pos_bias[l]),
        "w_proj_a": w_proj_a[l].astype(BF16),
        "w_proj_b": w_proj_b[l].astype(BF16),
        "w_out": w_out[l].astype(BF16),
        "norm_ffn": norm_ffn[l].reshape(1, D_MODEL),
        "wq_t": w_query[l].T.astype(BF16),
        "keys": sub_keys[l].reshape(2 * PEER_HEADS, PEER_N_KEYS, PEER_D_HALF).astype(BF16),
        "expert_u": expert_u[l].astype(BF16),
        "expert_vt": expert_v[l].T.astype(BF16),
    }
    return (_trunk(x_prompt, p), _trunk(x_sample, p))
```

```python
import functools

import numpy as np
import jax
import jax.numpy as jnp
from jax import lax
from jax.experimental import pallas as pl
from jax.experimental.pallas import tpu as pltpu

F32 = jnp.float32
BF16 = jnp.bfloat16

D_MODEL = 1024
GRID_W = 64
EPS = 1e-6
HG_HEADS = 4
HG_DK = 128
HG_DV = 128
HG_WIDTH = HG_HEADS * HG_DV
HG_CHUNK = 64
HG_SUB = 16
NA_HEADS = 8
NA_HEAD_DIM = 64
NA_WIDTH = NA_HEADS * NA_HEAD_DIM
NA_WIN_R = 8
NA_WIN_C = 16
PEER_HEADS = 8
PEER_N_KEYS = 128
PEER_N_EXPERTS = PEER_N_KEYS * PEER_N_KEYS
PEER_D_HALF = 128
PEER_TOPK = 16
D_IN = 3 * NA_WIDTH + 3 * 2 * HG_HEADS * HG_DK + HG_WIDTH + 2 * D_MODEL

_COL_HG_Q = (3 * NA_WIDTH) // 128
_COL_HG_F = _COL_HG_Q + 8
_COL_HG_I = _COL_HG_F + 8
_COL_HG_G = _COL_HG_I + 8

V7X_VMEM_LIMIT = 56 * 1024 * 1024
NEG_BIG = -1e30

_CAND_PAIRS = tuple((p, q) for p in range(PEER_TOPK) for q in range(PEER_TOPK)
                    if (p + 1) * (q + 1) <= PEER_TOPK)


def _sorting_network(n):
    def merge(lo, hi, r):
        step = r * 2
        if step < hi - lo:
            yield from merge(lo, hi, step)
            yield from merge(lo + r, hi, step)
            yield from ((i, i + r) for i in range(lo + r, hi - r, step))
        else:
            yield (lo, lo + r)

    def sort(lo, hi):
        if hi - lo >= 1:
            mid = lo + (hi - lo) // 2
            yield from sort(lo, mid)
            yield from sort(mid + 1, hi)
            yield from merge(lo, hi, 1)

    return tuple(sort(0, n - 1))


_SORT16 = _sorting_network(PEER_TOPK)
_BITONIC16 = tuple((i, i + d) for d in (8, 4, 2, 1) for i in range(PEER_TOPK) if not i & d)


def _dot(a, b):
    return jnp.dot(a, b, preferred_element_type=F32)


def _dot_nt(a, b):
    return lax.dot_general(a, b, (((1,), (1,)), ((), ())), preferred_element_type=F32)


def _dot_tn(a, b):
    return lax.dot_general(a, b, (((0,), (0,)), ((), ())), preferred_element_type=F32)


def _split_bf16(x):
    hi = x.astype(BF16)
    lo = (x - hi.astype(F32)).astype(BF16)
    return hi, lo


IN_TM = 256
IN_TN = 1024


def _in_proj_kernel(x_ref, g_ref, w_ref, qkg_ref, red_ref, exp_ref, z_ref, qkv_ref):
    x = x_ref[...]
    ms = jnp.mean(x * x, axis=-1, keepdims=True)
    xn = (x * lax.rsqrt(ms + EPS) * g_ref[...]).astype(BF16)
    for j in range(D_IN // IN_TN):
        cs = slice(j * IN_TN, (j + 1) * IN_TN)
        z = _dot(xn, w_ref[:, cs])
        if j == 0:
            hi, lo = _split_bf16(z * z)
            hi, lo = _split_bf16(_dot(hi, red_ref[...]) + _dot(lo, red_ref[...]))
            ss = _dot(hi, exp_ref[...]) + _dot(lo, exp_ref[...])
            z = z * lax.rsqrt(ss * (1.0 / NA_HEAD_DIM) + EPS) * qkg_ref[...]
        z_ref[:, cs] = z
        lo_c, hi_c = j * IN_TN, min((j + 1) * IN_TN, 3 * NA_WIDTH)
        if hi_c > lo_c:
            qkv_ref[:, lo_c:hi_c] = z[:, :hi_c - lo_c].astype(BF16)


def _in_proj(x2d, gain, w_bf16, qk_gain, head_red, head_exp):
    n = x2d.shape[0]
    const = lambda shape: pl.BlockSpec(shape, lambda i: (0, 0), pipeline_mode=pl.Buffered(1))
    return pl.pallas_call(
        _in_proj_kernel,
        grid=(n // IN_TM,),
        in_specs=[
            pl.BlockSpec((IN_TM, D_MODEL), lambda i: (i, 0)),
            const((1, D_MODEL)),
            const((D_MODEL, D_IN)),
            const((1, IN_TN)),
            const(head_red.shape),
            const(head_exp.shape),
        ],
        out_specs=[pl.BlockSpec((IN_TM, D_IN), lambda i: (i, 0)),
                   pl.BlockSpec((IN_TM, 3 * NA_WIDTH), lambda i: (i, 0))],
        out_shape=[jax.ShapeDtypeStruct((n, D_IN), F32), jax.ShapeDtypeStruct((n, 3 * NA_WIDTH), BF16)],
        compiler_params=pltpu.CompilerParams(
            dimension_semantics=("parallel",), vmem_limit_bytes=V7X_VMEM_LIMIT),
        name="in_proj",
    )(x2d, gain, w_bf16, qk_gain, head_red, head_exp)


HG_TT = 512


def _hgrn_chunk(q, f_logit, v, lb, states, tri, pair_mask, ones, *, reverse):
    C, SB = HG_CHUNK, HG_SUB
    nblk = C // SB
    f = lb + (1.0 - lb) * jax.nn.sigmoid(f_logit)
    qs = q * jax.nn.sigmoid(q)
    hi, lo = _split_bf16(jnp.log(f))
    b = _dot(tri, hi) + _dot(tri, lo)
    c = b - jnp.log(1.0 - f)
    vb = v.astype(BF16)
    q_all = (qs * jnp.exp(b)).astype(BF16)
    b_end = b[0:1] if reverse else b[C - 1:C]
    k_d = jnp.exp(b_end - c).astype(BF16)
    carry_decay = jnp.exp(b_end)

    prevs, q_s, k_s = [], [], []
    for blk in range(nblk):
        lo_r, hi_r = blk * SB, (blk + 1) * SB
        if reverse and blk < nblk - 1:
            prev, ref = slice(hi_r, C), b[hi_r:hi_r + 1]
        elif (not reverse) and blk > 0:
            prev, ref = slice(0, lo_r), b[lo_r - 1:lo_r]
        else:
            prev = None
        prevs.append(prev)
        q_s.append(None if prev is None else (qs[lo_r:hi_r] * jnp.exp(b[lo_r:hi_r] - ref)).astype(BF16))
        k_s.append(None if prev is None else jnp.exp(ref - c[prev]).astype(BF16))

    outs, new_states = [], []
    for h, state in enumerate(states):
        cs = slice(h * HG_DK, (h + 1) * HG_DK)
        parts = []
        for blk in range(nblk):
            rows = slice(blk * SB, (blk + 1) * SB)
            prod = (qs[rows, cs][None, :, :] * jnp.exp(b[rows, cs][None, :, :] - c[rows, cs][:, None, :]))
            prod = prod.astype(BF16)
            prod = jnp.where(pair_mask, prod, jnp.zeros_like(prod))
            rs = _dot(prod.reshape(SB * SB, HG_DK), ones)
            o_i = jnp.sum(rs.reshape(SB, SB, HG_DV) * v[rows, cs][:, None, :], axis=0)
            if prevs[blk] is not None:
                s = _dot_nt(q_s[blk][:, cs], k_s[blk][:, cs])
                o_i = o_i + _dot(s.astype(BF16), vb[prevs[blk], cs])
            parts.append(o_i)
        o_inter = _dot_nt(q_all[:, cs], state.astype(BF16))
        outs.append(o_inter + jnp.concatenate(parts, axis=0))
        new_states.append(state * carry_decay[:, cs] + _dot_tn(vb[:, cs], k_d[:, cs]))
    return outs, new_states


def _hgrn_kernel(*refs, reverse, finish):
    if finish:
        q_ref, f_ref, v_ref, lb_ref, of_ref, g_ref, gain_ref, o_ref, st_ref = refs
    else:
        q_ref, f_ref, v_ref, lb_ref, o_ref, st_ref = refs
    C, SB = HG_CHUNK, HG_SUB
    nchunk = HG_TT // C

    @pl.when(pl.program_id(1) == 0)
    def _():
        st_ref[...] = jnp.zeros_like(st_ref)

    row = lax.broadcasted_iota(jnp.int32, (C, C), 0)
    col = lax.broadcasted_iota(jnp.int32, (C, C), 1)
    tri = ((col >= row) if reverse else (col <= row)).astype(BF16)
    s_i = lax.broadcasted_iota(jnp.int32, (SB, SB, HG_DK), 0).astype(F32).astype(BF16)
    t_i = lax.broadcasted_iota(jnp.int32, (SB, SB, HG_DK), 1).astype(F32).astype(BF16)
    pair_mask = (s_i >= t_i) if reverse else (s_i <= t_i)
    ones = jnp.ones((HG_DK, HG_DV), BF16)

    def chunk(ci, carry):
        c = (nchunk - 1 - ci) if reverse else ci
        sl = pl.ds(pl.multiple_of(c * C, C), C)
        outs, states = _hgrn_chunk(q_ref[sl, :], f_ref[sl, :], v_ref[sl, :], lb_ref[0],
                                   [st_ref[h] for h in range(HG_HEADS)], tri, pair_mask, ones, reverse=reverse)
        for h in range(HG_HEADS):
            cs = slice(h * HG_DK, (h + 1) * HG_DK)
            o, st_ref[h] = outs[h], states[h]
            if finish:
                tot = of_ref[sl, cs] + o
                ms = jnp.mean(tot * tot, axis=-1, keepdims=True)
                g = g_ref[sl, cs]
                y = tot * lax.rsqrt(ms + EPS) * gain_ref[...] * (g * jax.nn.sigmoid(g))
                o_ref[sl, cs] = y.astype(o_ref.dtype)
            else:
                o_ref[sl, cs] = o
        return carry

    lax.fori_loop(0, nchunk, chunk, 0)


def _hgrn_pass(z, lb, batch, seq, *, reverse, o_fwd=None, gain=None):
    n = z.shape[0]
    nt = seq // HG_TT
    finish = o_fwd is not None
    d = 1 if reverse else 0

    def tmap(t):
        return (nt - 1 - t) if reverse else t

    def zspec(col0, per_dir=True):
        blk = col0 // HG_HEADS + (d if per_dir else 0)
        return pl.BlockSpec((HG_TT, HG_WIDTH), lambda b, t: (b * nt + tmap(t), blk))

    ospec = pl.BlockSpec((HG_TT, HG_WIDTH), lambda b, t: (b * nt + tmap(t), 0))
    in_specs = [zspec(_COL_HG_Q), zspec(_COL_HG_F), zspec(_COL_HG_I),
                pl.BlockSpec((1, 1, HG_WIDTH), lambda b, t: (d, 0, 0))]
    args = [z, z, z, lb]
    if finish:
        in_specs += [ospec, zspec(_COL_HG_G, per_dir=False), pl.BlockSpec((1, HG_DV), lambda b, t: (0, 0))]
        args += [o_fwd, z, gain]
    return pl.pallas_call(
        functools.partial(_hgrn_kernel, reverse=reverse, finish=finish),
        grid=(batch, nt),
        in_specs=in_specs,
        out_specs=ospec,
        out_shape=jax.ShapeDtypeStruct((n, HG_WIDTH), BF16 if finish else F32),
        scratch_shapes=[pltpu.VMEM((HG_HEADS, HG_DV, HG_DK), F32)],
        compiler_params=pltpu.CompilerParams(
            dimension_semantics=("parallel", "arbitrary"),
            vmem_limit_bytes=V7X_VMEM_LIMIT),
        name="hgrn_bwd" if reverse else "hgrn_fwd",
    )(*args)


NA_BAND = 8
NA_BT = NA_BAND * GRID_W


def _na_bias_table(rpb):
    c = jnp.arange(GRID_W)
    c0 = jnp.clip(c - NA_WIN_C // 2, 0, GRID_W - NA_WIN_C)
    kc = jnp.arange(GRID_W)
    valid = (kc[None, :] >= c0[:, None]) & (kc[None, :] < c0[:, None] + NA_WIN_C)
    off = jnp.clip(kc[None, :] - c[:, None] + NA_WIN_C - 1, 0, 2 * NA_WIN_C - 2)
    t = jnp.where(valid[None, None], rpb.astype(F32)[:, :, off], NEG_BIG)
    t = t.reshape(NA_HEADS // 2, 2, 2 * NA_WIN_R - 1, GRID_W, GRID_W)
    t = t.transpose(0, 2, 4, 1, 3)
    return t.reshape(NA_HEADS // 2, (2 * NA_WIN_R - 1) * GRID_W, 2 * GRID_W)


def _na_kernel(q_ref, k0_ref, k1_ref, k2_ref, v0_ref, v1_ref, v2_ref, bias_ref, o_ref,
               kcat_ref, vcat_ref, *, rows):
    g = pl.program_id(1)
    for i, (kr, vr) in enumerate(((k0_ref, v0_ref), (k1_ref, v1_ref), (k2_ref, v2_ref))):
        kcat_ref[i * NA_BT:(i + 1) * NA_BT, :] = kr[...]
        vcat_ref[i * NA_BT:(i + 1) * NA_BT, :] = vr[...]
    lane = lax.broadcasted_iota(jnp.int32, (1, 2 * NA_HEAD_DIM), 1)
    left = lane < NA_HEAD_DIM
    nkeys = NA_WIN_R * GRID_W
    scale = NA_HEAD_DIM ** -0.5
    pairs = range(NA_HEADS // 2)

    def body(j, carry):
        r = g * NA_BAND + j
        r0 = jnp.clip(r - NA_WIN_R // 2, 0, rows - NA_WIN_R)
        koff = pl.multiple_of((r0 - g * NA_BAND + NA_BAND) * GRID_W, GRID_W)
        boff = pl.multiple_of((r0 - r + NA_WIN_R - 1) * GRID_W, GRID_W)
        qsl = pl.ds(pl.multiple_of(j * GRID_W, GRID_W), GRID_W)
        qb = q_ref[qsl, :] * scale
        cols = [slice(hp * 128, (hp + 1) * 128) for hp in pairs]
        s_t = []
        for cs in cols:
            q2 = qb[:, cs]
            zero = jnp.zeros_like(q2)
            rhs_t = jnp.concatenate([jnp.where(left, q2, zero), jnp.where(left, zero, q2)], axis=0)
            s_t.append(_dot_nt(kcat_ref[pl.ds(koff, nkeys), cs], rhs_t))
        s_t = [s + bias_ref[hp, pl.ds(boff, nkeys), :] for hp, s in zip(pairs, s_t)]
        p = [jnp.exp(s - jnp.max(s, axis=0, keepdims=True)) for s in s_t]
        p = [(x * (1.0 / jnp.sum(x, axis=0, keepdims=True))).astype(BF16) for x in p]
        o2 = [_dot_tn(x, vcat_ref[pl.ds(koff, nkeys), cs]) for x, cs in zip(p, cols)]
        for cs, o in zip(cols, o2):
            o_ref[qsl, cs] = jnp.where(left, o[:GRID_W], o[GRID_W:]).astype(o_ref.dtype)
        return carry

    lax.fori_loop(0, NA_BAND, body, 0)


def _na(z, bias_tbl, batch, seq):
    n = z.shape[0]
    rows = seq // GRID_W
    nb = rows // NA_BAND

    def kv(colblk, shift):
        return pl.BlockSpec(
            (NA_BT, NA_WIDTH),
            lambda b, g: (b * nb + jnp.clip(g + shift, 0, nb - 1), colblk))

    return pl.pallas_call(
        functools.partial(_na_kernel, rows=rows),
        grid=(batch, nb),
        in_specs=[pl.BlockSpec((NA_BT, NA_WIDTH), lambda b, g: (b * nb + g, 0)),
                  kv(1, -1), kv(1, 0), kv(1, 1), kv(2, -1), kv(2, 0), kv(2, 1),
                  pl.BlockSpec(bias_tbl.shape, lambda b, g: (0, 0, 0))],
        out_specs=pl.BlockSpec((NA_BT, NA_WIDTH), lambda b, g: (b * nb + g, 0)),
        out_shape=jax.ShapeDtypeStruct((n, NA_WIDTH), BF16),
        scratch_shapes=[pltpu.VMEM((3 * NA_BT, NA_WIDTH), BF16),
                        pltpu.VMEM((3 * NA_BT, NA_WIDTH), BF16)],
        compiler_params=pltpu.CompilerParams(
            dimension_semantics=("parallel", "arbitrary"), vmem_limit_bytes=V7X_VMEM_LIMIT),
        name="natten",
    )(z, z, z, z, z, z, z, bias_tbl)


MG_TM = 512


def _merge_kernel(x_ref, oa_ref, ob_ref, ga_ref, gb_ref, wa_ref, wb_ref, wo_ref, nf_ref,
                  x2_ref, xt_ref):
    a = _dot(oa_ref[...], wa_ref[...])
    b = _dot(ob_ref[...], wb_ref[...])
    mix = jax.nn.sigmoid(ga_ref[...]) * a + jax.nn.sigmoid(gb_ref[...]) * b
    x2 = x_ref[...] + _dot(mix.astype(BF16), wo_ref[...])
    x2_ref[...] = x2
    ms = jnp.mean(x2 * x2, axis=-1, keepdims=True)
    xt_ref[...] = (x2 * lax.rsqrt(ms + EPS) * nf_ref[...]).T.astype(BF16)


def _merge(x2d, o_a, o_b, z, wa, wb, wo, norm_ffn):
    n = x2d.shape[0]
    col_ga = (_COL_HG_G * 128 + HG_WIDTH) // D_MODEL
    full = lambda shape: pl.BlockSpec(shape, lambda i: (0, 0))
    tok = lambda w, c=0: pl.BlockSpec((MG_TM, w), lambda i: (i, c))
    return pl.pallas_call(
        _merge_kernel,
        grid=(n // MG_TM,),
        in_specs=[tok(D_MODEL), tok(HG_WIDTH), tok(NA_WIDTH), tok(D_MODEL, col_ga), tok(D_MODEL, col_ga + 1),
                  full((HG_WIDTH, D_MODEL)), full((NA_WIDTH, D_MODEL)), full((D_MODEL, D_MODEL)),
                  full((1, D_MODEL))],
        out_specs=[tok(D_MODEL), pl.BlockSpec((D_MODEL, MG_TM), lambda i: (0, i))],
        out_shape=[jax.ShapeDtypeStruct((n, D_MODEL), F32), jax.ShapeDtypeStruct((D_MODEL, n), BF16)],
        compiler_params=pltpu.CompilerParams(
            dimension_semantics=("parallel",), vmem_limit_bytes=V7X_VMEM_LIMIT),
        name="merge",
    )(x2d, o_a, o_b, z, z, wa, wb, wo, norm_ffn)


PEER_TB = 512
PEER_EB = 1024
PEER_LC = 256
PEER_KC = 256


def _peer_route(xt_ref, wq_ref, keys_ref, qt_ref, s_ref, toph_ref, top_ref, cand_ref, theta_ref,
                cnt_ref, coef_ref, r1_ref, e1_ref):
    K = PEER_TOPK
    half_rows = wq_ref.shape[0] // 2
    for part in range(2):
        rs = slice(part * half_rows, (part + 1) * half_rows)
        qt_ref[rs, :] = _dot(wq_ref[rs, :], xt_ref[...]).astype(BF16)

    def score(hp, carry):
        rows = pl.ds(pl.multiple_of(hp * PEER_D_HALF, PEER_D_HALF), PEER_D_HALF)
        s_ref[hp] = _dot(keys_ref[hp], qt_ref[rows, :])
        return carry

    lax.fori_loop(0, 2 * PEER_HEADS, score, 0)

    n_grp = PEER_N_KEYS // 8
    assert n_grp == K

    def exchange(rows, i, j):
        rows[i], rows[j] = jnp.maximum(rows[i], rows[j]), jnp.minimum(rows[i], rows[j])

    def extract(h, carry, *, half):
        for lc in range(xt_ref.shape[1] // 128):
            ls = slice(lc * 128, (lc + 1) * 128)
            keys = [s_ref[2 * h + half, 8 * r:8 * r + 8, ls] for r in range(n_grp)]
            rows = list(keys)
            for i, j in _SORT16:
                exchange(rows, i, j)
            for shift in (4, 2, 1):
                rows = [jnp.maximum(rows[i], pltpu.roll(rows[K - 1 - i], shift, axis=0)) for i in range(K)]
                for i, j in _BITONIC16:
                    exchange(rows, i, j)
            toph_ref[half, h, :, ls] = jnp.concatenate([rows[q][0:1] for q in range(K)], axis=0)
            if half:
                for r in range(0, n_grp, 2):
                    rank = [sum(jnp.where(k < rows[q], 1.0, 0.0) for q in range(K)) for k in keys[r:r + 2]]
                    r1_ref[h, 8 * r:8 * r + 16, ls] = jnp.concatenate(rank, axis=0).astype(BF16)
        return carry

    lax.fori_loop(0, PEER_HEADS, functools.partial(extract, half=0), 0)
    lax.fori_loop(0, PEER_HEADS, functools.partial(extract, half=1), 0)
    for half in range(2):
        for q in range(K):
            top_ref[half, q] = jnp.concatenate(
                [toph_ref[half, h, q:q + 1, :] for h in range(PEER_HEADS)], axis=0)

    ncand = len(_CAND_PAIRS)
    for c, (p, q) in enumerate(_CAND_PAIRS):
        cand_ref[c] = top_ref[0, p] + top_ref[1, q]
    small = cand_ref.shape[1:]

    def tau_step(it, carry):
        tau, cnt = carry
        m = cand_ref[0]
        for c in range(1, ncand):
            m = jnp.maximum(m, cand_ref[c])
        n_eq = jnp.zeros(small, F32)
        for c in range(ncand):
            v = cand_ref[c]
            hit = v == m
            n_eq = n_eq + jnp.where(hit, 1.0, 0.0)
            cand_ref[c] = jnp.where(hit, -jnp.inf, v)
        return jnp.where(cnt < float(K), m, tau), cnt + n_eq

    tau, _ = lax.fori_loop(0, K, tau_step, (jnp.full(small, -jnp.inf, F32), jnp.zeros(small, F32)))
    best = top_ref[0, 0] + top_ref[1, 0]
    zsum = jnp.zeros(small, F32)
    for p, q in _CAND_PAIRS:
        v = top_ref[0, p] + top_ref[1, q]
        zsum = zsum + jnp.where(v >= tau, jnp.exp(v - best), 0.0)
    inv_z = 1.0 / zsum

    for q in range(K):
        th = jnp.full(small, jnp.inf, F32)
        b_q = top_ref[1, q]
        for p in range(K):
            a_p = top_ref[0, p]
            th = jnp.minimum(th, jnp.where(a_p + b_q >= tau, a_p, jnp.inf))
        theta_ref[q] = th

    for h in range(PEER_HEADS):
        s0 = s_ref[2 * h]
        cnt = jnp.zeros(s0.shape, F32)
        for q in range(K):
            cnt = cnt + jnp.where(s0 >= theta_ref[q, h:h + 1, :], 1.0, 0.0)
        cnt_ref[h] = cnt
        coef_ref[h] = jnp.exp(s0 - top_ref[0, 0, h:h + 1, :])
        e1_ref[h] = (jnp.exp(s_ref[2 * h + 1] - top_ref[1, 0, h:h + 1, :]) * inv_z[h:h + 1]).astype(BF16)


def _peer_kernel(xt_ref, x2_ref, wq_ref, keys_ref, u_ref, vt_ref, o_ref,
                 qt_ref, s_ref, toph_ref, top_ref, cand_ref, theta_ref,
                 cnt_ref, coef_ref, r1_ref, e1_ref, g_ref, acc_ref):
    e = pl.program_id(1)
    ne = pl.num_programs(1)

    @pl.when(e == 0)
    def _():
        _peer_route(xt_ref, wq_ref, keys_ref, qt_ref, s_ref, toph_ref, top_ref, cand_ref, theta_ref,
                    cnt_ref, coef_ref, r1_ref, e1_ref)
        acc_ref[...] = jnp.zeros_like(acc_ref)

    n_i = PEER_EB // PEER_N_KEYS
    i_rows = pl.ds(pl.multiple_of(e * n_i, n_i), n_i)
    h_t = _dot(u_ref[...], xt_ref[...])
    zero = jnp.zeros((PEER_N_KEYS, PEER_LC), BF16)
    for lc in range(PEER_TB // PEER_LC):
        ls = slice(lc * PEER_LC, (lc + 1) * PEER_LC)
        cnt_rows = [cnt_ref[h, i_rows, ls].astype(BF16) for h in range(PEER_HEADS)]
        coef_rows = [coef_ref[h, i_rows, ls].astype(BF16) for h in range(PEER_HEADS)]
        for ii in range(n_i):
            gate = zero
            for h in range(PEER_HEADS):
                sel = r1_ref[h, :, ls] < cnt_rows[h][ii:ii + 1]
                gate = gate + jnp.where(sel, e1_ref[h, :, ls], zero) * coef_rows[h][ii:ii + 1]
            g_ref[ii * PEER_N_KEYS:(ii + 1) * PEER_N_KEYS, ls] = gate
    y_t = None
    for c in range(PEER_EB // PEER_KC):
        rs = slice(c * PEER_KC, (c + 1) * PEER_KC)
        hh = h_t[rs]
        act = 0.5 * hh * (1.0 + lax.erf(hh * (2.0 ** -0.5)))
        part = _dot(vt_ref[:, rs], act.astype(BF16) * g_ref[rs, :])
        y_t = part if y_t is None else y_t + part
    acc_ref[...] += y_t

    @pl.when(e == ne - 1)
    def _():
        o_ref[...] = x2_ref[...] + acc_ref[...].T


def _peer(xt, x2, wq_t, keys, u_bf16, vt_bf16):
    n = x2.shape[0]
    tb, eb = PEER_TB, PEER_EB
    ne = PEER_N_EXPERTS // eb
    return pl.pallas_call(
        _peer_kernel,
        grid=(n // tb, PEER_N_EXPERTS // eb),
        in_specs=[
            pl.BlockSpec((D_MODEL, tb), lambda i, e: (0, i)),
            pl.BlockSpec((tb, D_MODEL), lambda i, e: (i, 0)),
            pl.BlockSpec(wq_t.shape, lambda i, e: (0, 0)),
            pl.BlockSpec(keys.shape, lambda i, e: (0, 0, 0)),
            pl.BlockSpec((eb, D_MODEL), lambda i, e: (e, 0)),
            pl.BlockSpec((D_MODEL, eb), lambda i, e: (0, e)),
        ],
        out_specs=pl.BlockSpec((tb, D_MODEL), lambda i, e: (i, 0)),
        out_shape=jax.ShapeDtypeStruct((n, D_MODEL), F32),
        scratch_shapes=[
            pltpu.VMEM((wq_t.shape[0], tb), BF16),
            pltpu.VMEM((2 * PEER_HEADS, PEER_N_KEYS, tb), F32),
            pltpu.VMEM((2, PEER_HEADS, PEER_TOPK, tb), F32),
            pltpu.VMEM((2, PEER_TOPK, PEER_HEADS, tb), F32),
            pltpu.VMEM((len(_CAND_PAIRS), PEER_HEADS, tb), F32),
            pltpu.VMEM((PEER_TOPK, PEER_HEADS, tb), F32),
            pltpu.VMEM((PEER_HEADS, PEER_N_KEYS, tb), F32),
            pltpu.VMEM((PEER_HEADS, PEER_N_KEYS, tb), F32),
            pltpu.VMEM((PEER_HEADS, PEER_N_KEYS, tb), BF16),
            pltpu.VMEM((PEER_HEADS, PEER_N_KEYS, tb), BF16),
            pltpu.VMEM((eb, tb), BF16),
            pltpu.VMEM((D_MODEL, tb), F32),
        ],
        compiler_params=pltpu.CompilerParams(
            dimension_semantics=("parallel", "arbitrary"), vmem_limit_bytes=V7X_VMEM_LIMIT),
        name="peer",
    )(xt, x2, wq_t, keys, u_bf16, vt_bf16)


def _trunk(x, p):
    batch, seq, _ = x.shape
    x2d = x.reshape(batch * seq, D_MODEL)
    z, qkv = _in_proj(x2d, p["norm_mix"], p["w_in"], p["qk_gain"], p["head_red"], p["head_exp"])
    o_f = _hgrn_pass(z, p["lb"], batch, seq, reverse=False)
    o_a = _hgrn_pass(z, p["lb"], batch, seq, reverse=True, o_fwd=o_f, gain=p["hg_out_norm"])
    o_b = _na(qkv, p["na_bias"], batch, seq)
    x2, xt = _merge(x2d, o_a, o_b, z, p["w_proj_a"], p["w_proj_b"], p["w_out"], p["norm_ffn"])
    y = _peer(xt, x2, p["wq_t"], p["keys"], p["expert_u"], p["expert_vt"])
    return y.reshape(batch, seq, D_MODEL)


def kernel(x_prompt, x_sample, norm_mix, w_in, lb_logits, hg_out_norm, q_norm, k_norm, rel_pos_bias,
           w_proj_a, w_proj_b, w_out, norm_ffn, w_query, sub_keys, expert_u, expert_v):
    l = 0
    lb_all = jnp.cumsum(jax.nn.softmax(lb_logits.astype(F32), axis=0), axis=0)
    head_id = np.arange(2 * NA_WIDTH) // NA_HEAD_DIM
    p = {
        "norm_mix": norm_mix[l].reshape(1, D_MODEL),
        "w_in": w_in[l].astype(BF16),
        "qk_gain": jnp.concatenate([jnp.tile(q_norm[l], NA_HEADS), jnp.tile(k_norm[l], NA_HEADS)]).reshape(1, -1),
        "head_red": jnp.asarray(head_id[:, None] == np.arange(128)[None, :], BF16),
        "head_exp": jnp.asarray(np.arange(128)[:, None] == head_id[None, :], BF16),
        "lb": lb_all[l].reshape(2, 1, HG_WIDTH),
        "hg_out_norm": hg_out_norm[l].reshape(1, HG_DV),
        "na_bias": _na_bias_table(rel_pos_bias[l]),
        "w_proj_a": w_proj_a[l].astype(BF16),
        "w_proj_b": w_proj_b[l].astype(BF16),
        "w_out": w_out[l].astype(BF16),
        "norm_ffn": norm_ffn[l].reshape(1, D_MODEL),
        "wq_t": w_query[l].T.astype(BF16),
        "keys": sub_keys[l].reshape(2 * PEER_HEADS, PEER_N_KEYS, PEER_D_HALF).astype(BF16),
        "expert_u": expert_u[l].astype(BF16),
        "expert_vt": expert_v[l].T.astype(BF16),
    }
    return (_trunk(x_prompt, p), _trunk(x_sample, p))
```

```python
import functools

import numpy as np
import jax
import jax.numpy as jnp
from jax import lax
from jax.experimental import pallas as pl
from jax.experimental.pallas import tpu as pltpu

F32 = jnp.float32
BF16 = jnp.bfloat16

D_MODEL = 1024
GRID_W = 64
EPS = 1e-6
HG_HEADS = 4
HG_DK = 128
HG_DV = 128
HG_WIDTH = HG_HEADS * HG_DV
HG_CHUNK = 64
HG_SUB = 16
NA_HEADS = 8
NA_HEAD_DIM = 64
NA_WIDTH = NA_HEADS * NA_HEAD_DIM
NA_WIN_R = 8
NA_WIN_C = 16
PEER_HEADS = 8
PEER_N_KEYS = 128
PEER_N_EXPERTS = PEER_N_KEYS * PEER_N_KEYS
PEER_D_HALF = 128
PEER_TOPK = 16
D_IN = 3 * NA_WIDTH + 3 * 2 * HG_HEADS * HG_DK + HG_WIDTH + 2 * D_MODEL

_COL_HG_Q = (3 * NA_WIDTH) // 128
_COL_HG_F = _COL_HG_Q + 8
_COL_HG_I = _COL_HG_F + 8
_COL_HG_G = _COL_HG_I + 8

V7X_VMEM_LIMIT = 56 * 1024 * 1024
NEG_BIG = -1e30

_CAND_PAIRS = tuple((p, q) for p in range(PEER_TOPK) for q in range(PEER_TOPK)
                    if (p + 1) * (q + 1) <= PEER_TOPK)


def _sorting_network(n):
    def merge(lo, hi, r):
        step = r * 2
        if step < hi - lo:
            yield from merge(lo, hi, step)
            yield from merge(lo + r, hi, step)
            yield from ((i, i + r) for i in range(lo + r, hi - r, step))
        else:
            yield (lo, lo + r)

    def sort(lo, hi):
        if hi - lo >= 1:
            mid = lo + (hi - lo) // 2
            yield from sort(lo, mid)
            yield from sort(mid + 1, hi)
            yield from merge(lo, hi, 1)

    return tuple(sort(0, n - 1))


_SORT16 = _sorting_network(PEER_TOPK)
_BITONIC16 = tuple((i, i + d) for d in (8, 4, 2, 1) for i in range(PEER_TOPK) if not i & d)


def _dot(a, b):
    return jnp.dot(a, b, preferred_element_type=F32)


def _dot_nt(a, b):
    return lax.dot_general(a, b, (((1,), (1,)), ((), ())), preferred_element_type=F32)


def _dot_tn(a, b):
    return lax.dot_general(a, b, (((0,), (0,)), ((), ())), preferred_element_type=F32)


def _split_bf16(x):
    hi = x.astype(BF16)
    lo = (x - hi.astype(F32)).astype(BF16)
    return hi, lo


IN_TM = 256
IN_TN = 1024


def _in_proj_kernel(x_ref, g_ref, w_ref, qkg_ref, red_ref, exp_ref, z_ref, qkv_ref):
    x = x_ref[...]
    ms = jnp.mean(x * x, axis=-1, keepdims=True)
    xn = (x * lax.rsqrt(ms + EPS) * g_ref[...]).astype(BF16)
    for j in range(D_IN // IN_TN):
        cs = slice(j * IN_TN, (j + 1) * IN_TN)
        z = _dot(xn, w_ref[:, cs])
        if j == 0:
            hi, lo = _split_bf16(z * z)
            hi, lo = _split_bf16(_dot(hi, red_ref[...]) + _dot(lo, red_ref[...]))
            ss = _dot(hi, exp_ref[...]) + _dot(lo, exp_ref[...])
            z = z * lax.rsqrt(ss * (1.0 / NA_HEAD_DIM) + EPS) * qkg_ref[...]
        z_ref[:, cs] = z
        lo_c, hi_c = j * IN_TN, min((j + 1) * IN_TN, 3 * NA_WIDTH)
        if hi_c > lo_c:
            qkv_ref[:, lo_c:hi_c] = z[:, :hi_c - lo_c].astype(BF16)


def _in_proj(x2d, gain, w_bf16, qk_gain, head_red, head_exp):
    n = x2d.shape[0]
    const = lambda shape: pl.BlockSpec(shape, lambda i: (0, 0), pipeline_mode=pl.Buffered(1))
    return pl.pallas_call(
        _in_proj_kernel,
        grid=(n // IN_TM,),
        in_specs=[
            pl.BlockSpec((IN_TM, D_MODEL), lambda i: (i, 0)),
            const((1, D_MODEL)),
            const((D_MODEL, D_IN)),
            const((1, IN_TN)),
            const(head_red.shape),
            const(head_exp.shape),
        ],
        out_specs=[pl.BlockSpec((IN_TM, D_IN), lambda i: (i, 0)),
                   pl.BlockSpec((IN_TM, 3 * NA_WIDTH), lambda i: (i, 0))],
        out_shape=[jax.ShapeDtypeStruct((n, D_IN), F32), jax.ShapeDtypeStruct((n, 3 * NA_WIDTH), BF16)],
        compiler_params=pltpu.CompilerParams(
            dimension_semantics=("parallel",), vmem_limit_bytes=V7X_VMEM_LIMIT),
        name="in_proj",
    )(x2d, gain, w_bf16, qk_gain, head_red, head_exp)


HG_TT = 512


def _hgrn_chunk(q, f_logit, v, lb, states, tri, pair_mask, level_masks, ones, *, reverse):
    C, SB = HG_CHUNK, HG_SUB
    nblk = C // SB
    f = lb + (1.0 - lb) * jax.nn.sigmoid(f_logit)
    qs = q * jax.nn.sigmoid(q)
    hi, lo = _split_bf16(jnp.log(f))
    b = _dot(tri, hi) + _dot(tri, lo)
    c = b - jnp.log(1.0 - f)
    vb = v.astype(BF16)
    q_all = (qs * jnp.exp(b)).astype(BF16)
    b_end = b[0:1] if reverse else b[C - 1:C]
    k_d = jnp.exp(b_end - c).astype(BF16)
    carry_decay = jnp.exp(b_end)

    def boundary_refs(size):
        q_ref, k_ref = [], []
        for g0 in range(0, C, 2 * size):
            first, second = slice(g0, g0 + size), slice(g0 + size, g0 + 2 * size)
            if reverse:
                bound = jnp.broadcast_to(b[g0 + size:g0 + size + 1], (size, b.shape[1]))
                q_ref += [bound, b[second]]
                k_ref += [c[first], bound]
            else:
                bound = jnp.broadcast_to(b[g0 + size - 1:g0 + size], (size, b.shape[1]))
                q_ref += [b[first], bound]
                k_ref += [bound, c[second]]
        return jnp.concatenate(q_ref, axis=0), jnp.concatenate(k_ref, axis=0)

    level_q, level_k = [], []
    for size in (C // 2, C // 4):
        q_bound, k_bound = boundary_refs(size)
        level_q.append((qs * jnp.exp(b - q_bound)).astype(BF16))
        level_k.append(jnp.exp(k_bound - c).astype(BF16))

    outs, new_states = [], []
    for h, state in enumerate(states):
        cs = slice(h * HG_DK, (h + 1) * HG_DK)
        prods = []
        for blk in range(nblk):
            rows = slice(blk * SB, (blk + 1) * SB)
            prod = (qs[rows, cs][None, :, :] * jnp.exp(b[rows, cs][None, :, :] - c[rows, cs][:, None, :]))
            prod = prod.astype(BF16)
            prods.append(jnp.where(pair_mask, prod, jnp.zeros_like(prod)).reshape(SB * SB, HG_DK))
        rs = _dot(jnp.concatenate(prods, axis=0), ones)
        rs = rs.reshape(nblk, SB, SB, HG_DV)
        o_diag = [jnp.sum(rs[blk] * v[blk * SB:(blk + 1) * SB, cs][:, None, :], axis=0) for blk in range(nblk)]
        s_off = sum(jnp.where(m, _dot_nt(lq[:, cs], lk[:, cs]), 0.0)
                    for m, lq, lk in zip(level_masks, level_q, level_k))
        o_off = _dot(s_off.astype(BF16), vb[:, cs])
        o_inter = _dot_nt(q_all[:, cs], state.astype(BF16))
        outs.append(o_inter + o_off + jnp.concatenate(o_diag, axis=0))
        new_states.append(state * carry_decay[:, cs] + _dot_tn(vb[:, cs], k_d[:, cs]))
    return outs, new_states


def _hgrn_kernel(*refs, reverse, finish):
    if finish:
        q_ref, f_ref, v_ref, lb_ref, of_ref, g_ref, gain_ref, o_ref, st_ref = refs
    else:
        q_ref, f_ref, v_ref, lb_ref, o_ref, st_ref = refs
    C, SB = HG_CHUNK, HG_SUB
    nchunk = HG_TT // C

    @pl.when(pl.program_id(1) == 0)
    def _():
        st_ref[...] = jnp.zeros_like(st_ref)

    row = lax.broadcasted_iota(jnp.int32, (C, C), 0)
    col = lax.broadcasted_iota(jnp.int32, (C, C), 1)
    tri = ((col >= row) if reverse else (col <= row)).astype(BF16)
    s_i = lax.broadcasted_iota(jnp.int32, (SB, SB, HG_DK), 0).astype(F32).astype(BF16)
    t_i = lax.broadcasted_iota(jnp.int32, (SB, SB, HG_DK), 1).astype(F32).astype(BF16)
    pair_mask = (s_i >= t_i) if reverse else (s_i <= t_i)
    ones = jnp.ones((HG_DK, HG_DV), BF16)
    level_masks = []
    for size in (C // 2, C // 4):
        tg, sg = row // size, col // size
        later, earlier = (0, 1) if reverse else (1, 0)
        level_masks.append((tg // 2 == sg // 2) & (tg % 2 == later) & (sg % 2 == earlier))

    def chunk(ci, carry):
        c = (nchunk - 1 - ci) if reverse else ci
        sl = pl.ds(pl.multiple_of(c * C, C), C)
        outs, states = _hgrn_chunk(q_ref[sl, :], f_ref[sl, :], v_ref[sl, :], lb_ref[0],
                                   [st_ref[h] for h in range(HG_HEADS)], tri, pair_mask, level_masks, ones,
                                   reverse=reverse)
        for h in range(HG_HEADS):
            cs = slice(h * HG_DK, (h + 1) * HG_DK)
            o, st_ref[h] = outs[h], states[h]
            if finish:
                tot = of_ref[sl, cs] + o
                ms = jnp.mean(tot * tot, axis=-1, keepdims=True)
                g = g_ref[sl, cs]
                y = tot * lax.rsqrt(ms + EPS) * gain_ref[...] * (g * jax.nn.sigmoid(g))
                o_ref[sl, cs] = y.astype(o_ref.dtype)
            else:
                o_ref[sl, cs] = o
        return carry

    lax.fori_loop(0, nchunk, chunk, 0)


def _hgrn_pass(z, lb, batch, seq, *, reverse, o_fwd=None, gain=None):
    n = z.shape[0]
    nt = seq // HG_TT
    finish = o_fwd is not None
    d = 1 if reverse else 0

    def tmap(t):
        return (nt - 1 - t) if reverse else t

    def zspec(col0, per_dir=True):
        blk = col0 // HG_HEADS + (d if per_dir else 0)
        return pl.BlockSpec((HG_TT, HG_WIDTH), lambda b, t: (b * nt + tmap(t), blk))

    ospec = pl.BlockSpec((HG_TT, HG_WIDTH), lambda b, t: (b * nt + tmap(t), 0))
    in_specs = [zspec(_COL_HG_Q), zspec(_COL_HG_F), zspec(_COL_HG_I),
                pl.BlockSpec((1, 1, HG_WIDTH), lambda b, t: (d, 0, 0))]
    args = [z, z, z, lb]
    if finish:
        in_specs += [ospec, zspec(_COL_HG_G, per_dir=False), pl.BlockSpec((1, HG_DV), lambda b, t: (0, 0))]
        args += [o_fwd, z, gain]
    return pl.pallas_call(
        functools.partial(_hgrn_kernel, reverse=reverse, finish=finish),
        grid=(batch, nt),
        in_specs=in_specs,
        out_specs=ospec,
        out_shape=jax.ShapeDtypeStruct((n, HG_WIDTH), BF16 if finish else F32),
        scratch_shapes=[pltpu.VMEM((HG_HEADS, HG_DV, HG_DK), F32)],
        compiler_params=pltpu.CompilerParams(
            dimension_semantics=("parallel", "arbitrary"),
            vmem_limit_bytes=V7X_VMEM_LIMIT),
        name="hgrn_bwd" if reverse else "hgrn_fwd",
    )(*args)


NA_BAND = 8
NA_BT = NA_BAND * GRID_W


def _na_bias_table(rpb):
    c = jnp.arange(GRID_W)
    c0 = jnp.clip(c - NA_WIN_C // 2, 0, GRID_W - NA_WIN_C)
    kc = jnp.arange(GRID_W)
    valid = (kc[None, :] >= c0[:, None]) & (kc[None, :] < c0[:, None] + NA_WIN_C)
    off = jnp.clip(kc[None, :] - c[:, None] + NA_WIN_C - 1, 0, 2 * NA_WIN_C - 2)
    t = jnp.where(valid[None, None], rpb.astype(F32)[:, :, off], NEG_BIG)
    t = t.reshape(NA_HEADS // 2, 2, 2 * NA_WIN_R - 1, GRID_W, GRID_W)
    t = t.transpose(0, 2, 4, 1, 3)
    return t.reshape(NA_HEADS // 2, (2 * NA_WIN_R - 1) * GRID_W, 2 * GRID_W)


def _na_kernel(q_ref, k0_ref, k1_ref, k2_ref, v0_ref, v1_ref, v2_ref, bias_ref, o_ref,
               kcat_ref, vcat_ref, *, rows):
    g = pl.program_id(1)
    for i, (kr, vr) in enumerate(((k0_ref, v0_ref), (k1_ref, v1_ref), (k2_ref, v2_ref))):
        kcat_ref[i * NA_BT:(i + 1) * NA_BT, :] = kr[...]
        vcat_ref[i * NA_BT:(i + 1) * NA_BT, :] = vr[...]
    lane = lax.broadcasted_iota(jnp.int32, (1, 2 * NA_HEAD_DIM), 1)
    left = lane < NA_HEAD_DIM
    nkeys = NA_WIN_R * GRID_W
    scale = NA_HEAD_DIM ** -0.5
    pairs = range(NA_HEADS // 2)

    def body(j, carry):
        r = g * NA_BAND + j
        r0 = jnp.clip(r - NA_WIN_R // 2, 0, rows - NA_WIN_R)
        koff = pl.multiple_of((r0 - g * NA_BAND + NA_BAND) * GRID_W, GRID_W)
        boff = pl.multiple_of((r0 - r + NA_WIN_R - 1) * GRID_W, GRID_W)
        qsl = pl.ds(pl.multiple_of(j * GRID_W, GRID_W), GRID_W)
        qb = q_ref[qsl, :] * scale
        cols = [slice(hp * 128, (hp + 1) * 128) for hp in pairs]
        s_t = []
        for cs in cols:
            q2 = qb[:, cs]
            zero = jnp.zeros_like(q2)
            rhs_t = jnp.concatenate([jnp.where(left, q2, zero), jnp.where(left, zero, q2)], axis=0)
            s_t.append(_dot_nt(kcat_ref[pl.ds(koff, nkeys), cs], rhs_t))
        s_t = [s + bias_ref[hp, pl.ds(boff, nkeys), :] for hp, s in zip(pairs, s_t)]
        p = [jnp.exp(s - jnp.max(s, axis=0, keepdims=True)) for s in s_t]
        p = [(x * (1.0 / jnp.sum(x, axis=0, keepdims=True))).astype(BF16) for x in p]
        o2 = [_dot_tn(x, vcat_ref[pl.ds(koff, nkeys), cs]) for x, cs in zip(p, cols)]
        for cs, o in zip(cols, o2):
            o_ref[qsl, cs] = jnp.where(left, o[:GRID_W], o[GRID_W:]).astype(o_ref.dtype)
        return carry

    lax.fori_loop(0, NA_BAND, body, 0)


def _na(z, bias_tbl, batch, seq):
    n = z.shape[0]
    rows = seq // GRID_W
    nb = rows // NA_BAND

    def kv(colblk, shift):
        return pl.BlockSpec(
            (NA_BT, NA_WIDTH),
            lambda b, g: (b * nb + jnp.clip(g + shift, 0, nb - 1), colblk))

    return pl.pallas_call(
        functools.partial(_na_kernel, rows=rows),
        grid=(batch, nb),
        in_specs=[pl.BlockSpec((NA_BT, NA_WIDTH), lambda b, g: (b * nb + g, 0)),
                  kv(1, -1), kv(1, 0), kv(1, 1), kv(2, -1), kv(2, 0), kv(2, 1),
                  pl.BlockSpec(bias_tbl.shape, lambda b, g: (0, 0, 0))],
        out_specs=pl.BlockSpec((NA_BT, NA_WIDTH), lambda b, g: (b * nb + g, 0)),
        out_shape=jax.ShapeDtypeStruct((n, NA_WIDTH), BF16),
        scratch_shapes=[pltpu.VMEM((3 * NA_BT, NA_WIDTH), BF16),
                        pltpu.VMEM((3 * NA_BT, NA_WIDTH), BF16)],
        compiler_params=pltpu.CompilerParams(
            dimension_semantics=("parallel", "arbitrary"), vmem_limit_bytes=V7X_VMEM_LIMIT),
        name="natten",
    )(z, z, z, z, z, z, z, bias_tbl)


MG_TM = 512


def _merge_kernel(x_ref, oa_ref, ob_ref, ga_ref, gb_ref, wa_ref, wb_ref, wo_ref, nf_ref,
                  x2_ref, xt_ref):
    a = _dot(oa_ref[...], wa_ref[...])
    b = _dot(ob_ref[...], wb_ref[...])
    mix = jax.nn.sigmoid(ga_ref[...]) * a + jax.nn.sigmoid(gb_ref[...]) * b
    x2 = x_ref[...] + _dot(mix.astype(BF16), wo_ref[...])
    x2_ref[...] = x2
    ms = jnp.mean(x2 * x2, axis=-1, keepdims=True)
    xt_ref[...] = (x2 * lax.rsqrt(ms + EPS) * nf_ref[...]).T.astype(BF16)


def _merge(x2d, o_a, o_b, z, wa, wb, wo, norm_ffn):
    n = x2d.shape[0]
    col_ga = (_COL_HG_G * 128 + HG_WIDTH) // D_MODEL
    full = lambda shape: pl.BlockSpec(shape, lambda i: (0, 0))
    tok = lambda w, c=0: pl.BlockSpec((MG_TM, w), lambda i: (i, c))
    return pl.pallas_call(
        _merge_kernel,
        grid=(n // MG_TM,),
        in_specs=[tok(D_MODEL), tok(HG_WIDTH), tok(NA_WIDTH), tok(D_MODEL, col_ga), tok(D_MODEL, col_ga + 1),
                  full((HG_WIDTH, D_MODEL)), full((NA_WIDTH, D_MODEL)), full((D_MODEL, D_MODEL)),
                  full((1, D_MODEL))],
        out_specs=[tok(D_MODEL), pl.BlockSpec((D_MODEL, MG_TM), lambda i: (0, i))],
        out_shape=[jax.ShapeDtypeStruct((n, D_MODEL), F32), jax.ShapeDtypeStruct((D_MODEL, n), BF16)],
        compiler_params=pltpu.CompilerParams(
            dimension_semantics=("parallel",), vmem_limit_bytes=V7X_VMEM_LIMIT),
        name="merge",
    )(x2d, o_a, o_b, z, z, wa, wb, wo, norm_ffn)


PEER_TB = 512
PEER_EB = 1024
PEER_LC = 256
PEER_KC = 256


def _peer_route(xt_ref, wq_ref, keys_ref, qt_ref, s_ref, toph_ref, top_ref, cand_ref, theta_ref,
                cnt_ref, coef_ref, r1_ref, e1_ref):
    K = PEER_TOPK
    half_rows = wq_ref.shape[0] // 2
    for part in range(2):
        rs = slice(part * half_rows, (part + 1) * half_rows)
        qt_ref[rs, :] = _dot(wq_ref[rs, :], xt_ref[...]).astype(BF16)

    for hp in range(2 * PEER_HEADS):
        s_ref[hp] = _dot(keys_ref[hp], qt_ref[hp * PEER_D_HALF:(hp + 1) * PEER_D_HALF, :])

    n_grp = PEER_N_KEYS // 8
    assert n_grp == K

    def exchange(rows, i, j):
        rows[i], rows[j] = jnp.maximum(rows[i], rows[j]), jnp.minimum(rows[i], rows[j])

    def extract(h, carry, *, half):
        for lc in range(xt_ref.shape[1] // 128):
            ls = slice(lc * 128, (lc + 1) * 128)
            keys = [s_ref[2 * h + half, 8 * r:8 * r + 8, ls] for r in range(n_grp)]
            rows = list(keys)
            for i, j in _SORT16:
                exchange(rows, i, j)
            for shift in (4, 2, 1):
                rows = [jnp.maximum(rows[i], pltpu.roll(rows[K - 1 - i], shift, axis=0)) for i in range(K)]
                for i, j in _BITONIC16:
                    exchange(rows, i, j)
            toph_ref[half, h, :, ls] = jnp.concatenate([rows[q][0:1] for q in range(K)], axis=0)
            if half:
                for r in range(0, n_grp, 2):
                    rank = [sum(jnp.where(k < rows[q], 1.0, 0.0) for q in range(K)) for k in keys[r:r + 2]]
                    r1_ref[h, 8 * r:8 * r + 16, ls] = jnp.concatenate(rank, axis=0).astype(BF16)
        return carry

    lax.fori_loop(0, PEER_HEADS, functools.partial(extract, half=0), 0)
    lax.fori_loop(0, PEER_HEADS, functools.partial(extract, half=1), 0)
    for half in range(2):
        for q in range(K):
            top_ref[half, q] = jnp.concatenate(
                [toph_ref[half, h, q:q + 1, :] for h in range(PEER_HEADS)], axis=0)

    ncand = len(_CAND_PAIRS)
    for c, (p, q) in enumerate(_CAND_PAIRS):
        cand_ref[c] = top_ref[0, p] + top_ref[1, q]
    small = cand_ref.shape[1:]

    def tau_step(it, carry):
        tau, cnt = carry
        m = cand_ref[0]
        for c in range(1, ncand):
            m = jnp.maximum(m, cand_ref[c])
        n_eq = jnp.zeros(small, F32)
        for c in range(ncand):
            v = cand_ref[c]
            hit = v == m
            n_eq = n_eq + jnp.where(hit, 1.0, 0.0)
            cand_ref[c] = jnp.where(hit, -jnp.inf, v)
        return jnp.where(cnt < float(K), m, tau), cnt + n_eq

    tau, _ = lax.fori_loop(0, K, tau_step, (jnp.full(small, -jnp.inf, F32), jnp.zeros(small, F32)))
    best = top_ref[0, 0] + top_ref[1, 0]
    zsum = jnp.zeros(small, F32)
    for p, q in _CAND_PAIRS:
        v = top_ref[0, p] + top_ref[1, q]
        zsum = zsum + jnp.where(v >= tau, jnp.exp(v - best), 0.0)
    inv_z = 1.0 / zsum

    for q in range(K):
        th = jnp.full(small, jnp.inf, F32)
        b_q = top_ref[1, q]
        for p in range(K):
            a_p = top_ref[0, p]
            th = jnp.minimum(th, jnp.where(a_p + b_q >= tau, a_p, jnp.inf))
        theta_ref[q] = th

    for h in range(PEER_HEADS):
        s0 = s_ref[2 * h]
        cnt = jnp.zeros(s0.shape, F32)
        for q in range(K):
            cnt = cnt + jnp.where(s0 >= theta_ref[q, h:h + 1, :], 1.0, 0.0)
        cnt_ref[h] = cnt
        coef_ref[h] = jnp.exp(s0 - top_ref[0, 0, h:h + 1, :])
        e1_ref[h] = (jnp.exp(s_ref[2 * h + 1] - top_ref[1, 0, h:h + 1, :]) * inv_z[h:h + 1]).astype(BF16)


def _peer_kernel(xt_ref, x2_ref, wq_ref, keys_ref, u_ref, vt_ref, o_ref,
                 qt_ref, s_ref, toph_ref, top_ref, cand_ref, theta_ref,
                 cnt_ref, coef_ref, r1_ref, e1_ref, g_ref, acc_ref):
    e = pl.program_id(1)
    ne = pl.num_programs(1)

    @pl.when(e == 0)
    def _():
        _peer_route(xt_ref, wq_ref, keys_ref, qt_ref, s_ref, toph_ref, top_ref, cand_ref, theta_ref,
                    cnt_ref, coef_ref, r1_ref, e1_ref)
        acc_ref[...] = jnp.zeros_like(acc_ref)

    n_i = PEER_EB // PEER_N_KEYS
    i_rows = pl.ds(pl.multiple_of(e * n_i, n_i), n_i)
    h_t = _dot(u_ref[...], xt_ref[...])
    zero = jnp.zeros((PEER_N_KEYS, PEER_LC), BF16)
    for lc in range(PEER_TB // PEER_LC):
        ls = slice(lc * PEER_LC, (lc + 1) * PEER_LC)
        cnt_rows = [cnt_ref[h, i_rows, ls].astype(BF16) for h in range(PEER_HEADS)]
        coef_rows = [coef_ref[h, i_rows, ls].astype(BF16) for h in range(PEER_HEADS)]
        for ii in range(n_i):
            gate = zero
            for h in range(PEER_HEADS):
                sel = r1_ref[h, :, ls] < cnt_rows[h][ii:ii + 1]
                gate = gate + jnp.where(sel, e1_ref[h, :, ls], zero) * coef_rows[h][ii:ii + 1]
            g_ref[ii * PEER_N_KEYS:(ii + 1) * PEER_N_KEYS, ls] = gate
    y_t = None
    for c in range(PEER_EB // PEER_KC):
        rs = slice(c * PEER_KC, (c + 1) * PEER_KC)
        hh = h_t[rs]
        act = 0.5 * hh * (1.0 + lax.erf(hh * (2.0 ** -0.5)))
        part = _dot(vt_ref[:, rs], act.astype(BF16) * g_ref[rs, :])
        y_t = part if y_t is None else y_t + part
    acc_ref[...] += y_t

    @pl.when(e == ne - 1)
    def _():
        o_ref[...] = x2_ref[...] + acc_ref[...].T


def _peer(xt, x2, wq_t, keys, u_bf16, vt_bf16):
    n = x2.shape[0]
    tb, eb = PEER_TB, PEER_EB
    ne = PEER_N_EXPERTS // eb
    return pl.pallas_call(
        _peer_kernel,
        grid=(n // tb, PEER_N_EXPERTS // eb),
        in_specs=[
            pl.BlockSpec((D_MODEL, tb), lambda i, e: (0, i)),
            pl.BlockSpec((tb, D_MODEL), lambda i, e: (i, 0)),
            pl.BlockSpec(wq_t.shape, lambda i, e: (0, 0)),
            pl.BlockSpec(keys.shape, lambda i, e: (0, 0, 0)),
            pl.BlockSpec((eb, D_MODEL), lambda i, e: (e, 0)),
            pl.BlockSpec((D_MODEL, eb), lambda i, e: (0, e)),
        ],
        out_specs=pl.BlockSpec((tb, D_MODEL), lambda i, e: (i, 0)),
        out_shape=jax.ShapeDtypeStruct((n, D_MODEL), F32),
        scratch_shapes=[
            pltpu.VMEM((wq_t.shape[0], tb), BF16),
            pltpu.VMEM((2 * PEER_HEADS, PEER_N_KEYS, tb), F32),
            pltpu.VMEM((2, PEER_HEADS, PEER_TOPK, tb), F32),
            pltpu.VMEM((2, PEER_TOPK, PEER_HEADS, tb), F32),
            pltpu.VMEM((len(_CAND_PAIRS), PEER_HEADS, tb), F32),
            pltpu.VMEM((PEER_TOPK, PEER_HEADS, tb), F32),
            pltpu.VMEM((PEER_HEADS, PEER_N_KEYS, tb), F32),
            pltpu.VMEM((PEER_HEADS, PEER_N_KEYS, tb), F32),
            pltpu.VMEM((PEER_HEADS, PEER_N_KEYS, tb), BF16),
            pltpu.VMEM((PEER_HEADS, PEER_N_KEYS, tb), BF16),
            pltpu.VMEM((eb, tb), BF16),
            pltpu.VMEM((D_MODEL, tb), F32),
        ],
        compiler_params=pltpu.CompilerParams(
            dimension_semantics=("parallel", "arbitrary"), vmem_limit_bytes=V7X_VMEM_LIMIT),
        name="peer",
    )(xt, x2, wq_t, keys, u_bf16, vt_bf16)


def _trunk(x, p):
    batch, seq, _ = x.shape
    x2d = x.reshape(batch * seq, D_MODEL)
    z, qkv = _in_proj(x2d, p["norm_mix"], p["w_in"], p["qk_gain"], p["head_red"], p["head_exp"])
    o_f = _hgrn_pass(z, p["lb"], batch, seq, reverse=False)
    o_a = _hgrn_pass(z, p["lb"], batch, seq, reverse=True, o_fwd=o_f, gain=p["hg_out_norm"])
    o_b = _na(qkv, p["na_bias"], batch, seq)
    x2, xt = _merge(x2d, o_a, o_b, z, p["w_proj_a"], p["w_proj_b"], p["w_out"], p["norm_ffn"])
    y = _peer(xt, x2, p["wq_t"], p["keys"], p["expert_u"], p["expert_vt"])
    return y.reshape(batch, seq, D_MODEL)


def kernel(x_prompt, x_sample, norm_mix, w_in, lb_logits, hg_out_norm, q_norm, k_norm, rel_pos_bias,
           w_proj_a, w_proj_b, w_out, norm_ffn, w_query, sub_keys, expert_u, expert_v):
    l = 0
    lb_all = jnp.cumsum(jax.nn.softmax(lb_logits.astype(F32), axis=0), axis=0)
    head_id = np.arange(2 * NA_WIDTH) // NA_HEAD_DIM
    p = {
        "norm_mix": norm_mix[l].reshape(1, D_MODEL),
        "w_in": w_in[l].astype(BF16),
        "qk_gain": jnp.concatenate([jnp.tile(q_norm[l], NA_HEADS), jnp.tile(k_norm[l], NA_HEADS)]).reshape(1, -1),
        "head_red": jnp.asarray(head_id[:, None] == np.arange(128)[None, :], BF16),
        "head_exp": jnp.asarray(np.arange(128)[:, None] == head_id[None, :], BF16),
        "lb": lb_all[l].reshape(2, 1, HG_WIDTH),
        "hg_out_norm": hg_out_norm[l].reshape(1, HG_DV),
        "na_bias": _na_bias_table(rel_pos_bias[l]),
        "w_proj_a": w_proj_a[l].astype(BF16),
        "w_proj_b": w_proj_b[l].astype(BF16),
        "w_out": w_out[l].astype(BF16),
        "norm_ffn": norm_ffn[l].reshape(1, D_MODEL),
        "wq_t": w_query[l].T.astype(BF16),
        "keys": sub_keys[l].reshape(2 * PEER_HEADS, PEER_N_KEYS, PEER_D_HALF).astype(BF16),
        "expert_u": expert_u[l].astype(BF16),
        "expert_vt": expert_v[l].T.astype(BF16),
    }
    return (_trunk(x_prompt, p), _trunk(x_sample, p))
```

```python
import functools

import numpy as np
import jax
import jax.numpy as jnp
from jax import lax
from jax.experimental import pallas as pl
from jax.experimental.pallas import tpu as pltpu

F32 = jnp.float32
BF16 = jnp.bfloat16

D_MODEL = 1024
GRID_W = 64
EPS = 1e-6
HG_HEADS = 4
HG_DK = 128
HG_DV = 128
HG_WIDTH = HG_HEADS * HG_DV
HG_CHUNK = 64
HG_SUB = 16
HG_LEVELS = (32, 16)
assert HG_LEVELS[0] * 2 == HG_CHUNK and HG_LEVELS[-1] == HG_SUB
NA_HEADS = 8
NA_HEAD_DIM = 64
NA_WIDTH = NA_HEADS * NA_HEAD_DIM
NA_WIN_R = 8
NA_WIN_C = 16
PEER_HEADS = 8
PEER_N_KEYS = 128
PEER_N_EXPERTS = PEER_N_KEYS * PEER_N_KEYS
PEER_D_HALF = 128
PEER_TOPK = 16
D_IN = 3 * NA_WIDTH + 3 * 2 * HG_HEADS * HG_DK + HG_WIDTH + 2 * D_MODEL

_COL_HG_Q = (3 * NA_WIDTH) // 128
_COL_HG_F = _COL_HG_Q + 8
_COL_HG_I = _COL_HG_F + 8
_COL_HG_G = _COL_HG_I + 8

V7X_VMEM_LIMIT = 56 * 1024 * 1024
NEG_BIG = -1e30

_CAND_PAIRS = tuple((p, q) for p in range(PEER_TOPK) for q in range(PEER_TOPK)
                    if (p + 1) * (q + 1) <= PEER_TOPK)


def _sorting_network(n):
    def merge(lo, hi, r):
        step = r * 2
        if step < hi - lo:
            yield from merge(lo, hi, step)
            yield from merge(lo + r, hi, step)
            yield from ((i, i + r) for i in range(lo + r, hi - r, step))
        else:
            yield (lo, lo + r)

    def sort(lo, hi):
        if hi - lo >= 1:
            mid = lo + (hi - lo) // 2
            yield from sort(lo, mid)
            yield from sort(mid + 1, hi)
            yield from merge(lo, hi, 1)

    return tuple(sort(0, n - 1))


_SORT16 = _sorting_network(PEER_TOPK)
_BITONIC16 = tuple((i, i + d) for d in (8, 4, 2, 1) for i in range(PEER_TOPK) if not i & d)


def _dot(a, b):
    return jnp.dot(a, b, preferred_element_type=F32)


def _dot_nt(a, b):
    return lax.dot_general(a, b, (((1,), (1,)), ((), ())), preferred_element_type=F32)


def _dot_tn(a, b):
    return lax.dot_general(a, b, (((0,), (0,)), ((), ())), preferred_element_type=F32)


def _split_bf16(x):
    hi = x.astype(BF16)
    lo = (x - hi.astype(F32)).astype(BF16)
    return hi, lo


IN_TM = 256
IN_TN = 1024


def _in_proj_kernel(x_ref, g_ref, w_ref, qkg_ref, red_ref, exp_ref, z_ref, qkv_ref):
    x = x_ref[...]
    ms = jnp.mean(x * x, axis=-1, keepdims=True)
    xn = (x * lax.rsqrt(ms + EPS) * g_ref[...]).astype(BF16)
    for j in range(D_IN // IN_TN):
        cs = slice(j * IN_TN, (j + 1) * IN_TN)
        z = _dot(xn, w_ref[:, cs])
        if j == 0:
            hi, lo = _split_bf16(z * z)
            hi, lo = _split_bf16(_dot(hi, red_ref[...]) + _dot(lo, red_ref[...]))
            ss = _dot(hi, exp_ref[...]) + _dot(lo, exp_ref[...])
            z = z * lax.rsqrt(ss * (1.0 / NA_HEAD_DIM) + EPS) * qkg_ref[...]
        z_ref[:, cs] = z
        lo_c, hi_c = j * IN_TN, min((j + 1) * IN_TN, 3 * NA_WIDTH)
        if hi_c > lo_c:
            qkv_ref[:, lo_c:hi_c] = z[:, :hi_c - lo_c].astype(BF16)


def _in_proj(x2d, gain, w_bf16, qk_gain, head_red, head_exp):
    n = x2d.shape[0]
    const = lambda shape: pl.BlockSpec(shape, lambda i: (0, 0), pipeline_mode=pl.Buffered(1))
    return pl.pallas_call(
        _in_proj_kernel,
        grid=(n // IN_TM,),
        in_specs=[
            pl.BlockSpec((IN_TM, D_MODEL), lambda i: (i, 0)),
            const((1, D_MODEL)),
            const((D_MODEL, D_IN)),
            const((1, IN_TN)),
            const(head_red.shape),
            const(head_exp.shape),
        ],
        out_specs=[pl.BlockSpec((IN_TM, D_IN), lambda i: (i, 0)),
                   pl.BlockSpec((IN_TM, 3 * NA_WIDTH), lambda i: (i, 0))],
        out_shape=[jax.ShapeDtypeStruct((n, D_IN), F32), jax.ShapeDtypeStruct((n, 3 * NA_WIDTH), BF16)],
        compiler_params=pltpu.CompilerParams(
            dimension_semantics=("parallel",), vmem_limit_bytes=V7X_VMEM_LIMIT),
        name="in_proj",
    )(x2d, gain, w_bf16, qk_gain, head_red, head_exp)


HG_TT = 512


def _hgrn_chunk(q, f_logit, v, lb, states, tri, pair_mask, level_masks, ones, *, reverse):
    C, SB = HG_CHUNK, HG_SUB
    nblk = C // SB
    f = lb + (1.0 - lb) * jax.nn.sigmoid(f_logit)
    qs = q * jax.nn.sigmoid(q)
    hi, lo = _split_bf16(jnp.log(f))
    b = _dot(tri, hi) + _dot(tri, lo)
    c = b - jnp.log(1.0 - f)
    vb = v.astype(BF16)
    q_all = (qs * jnp.exp(b)).astype(BF16)
    b_end = b[0:1] if reverse else b[C - 1:C]
    k_d = jnp.exp(b_end - c).astype(BF16)
    carry_decay = jnp.exp(b_end)

    def boundary_refs(size):
        q_ref, k_ref = [], []
        for g0 in range(0, C, 2 * size):
            first, second = slice(g0, g0 + size), slice(g0 + size, g0 + 2 * size)
            if reverse:
                bound = jnp.broadcast_to(b[g0 + size:g0 + size + 1], (size, b.shape[1]))
                q_ref += [bound, b[second]]
                k_ref += [c[first], bound]
            else:
                bound = jnp.broadcast_to(b[g0 + size - 1:g0 + size], (size, b.shape[1]))
                q_ref += [b[first], bound]
                k_ref += [bound, c[second]]
        return jnp.concatenate(q_ref, axis=0), jnp.concatenate(k_ref, axis=0)

    level_q, level_k = [], []
    for size in HG_LEVELS:
        q_bound, k_bound = boundary_refs(size)
        level_q.append((qs * jnp.exp(b - q_bound)).astype(BF16))
        level_k.append(jnp.exp(k_bound - c).astype(BF16))

    outs, new_states = [], []
    for h, state in enumerate(states):
        cs = slice(h * HG_DK, (h + 1) * HG_DK)
        prods = []
        for blk in range(nblk):
            rows = slice(blk * SB, (blk + 1) * SB)
            prod = (qs[rows, cs][None, :, :] * jnp.exp(b[rows, cs][None, :, :] - c[rows, cs][:, None, :]))
            prods.append(jnp.where(pair_mask, prod, 0.0).reshape(SB * SB, HG_DK))
        rs = _dot(jnp.concatenate(prods, axis=0).astype(BF16), ones)
        rs = rs.reshape(nblk, SB, SB, HG_DV)
        o_diag = [jnp.sum(rs[blk] * v[blk * SB:(blk + 1) * SB, cs][:, None, :], axis=0) for blk in range(nblk)]
        s_off = sum(jnp.where(m, _dot_nt(lq[:, cs], lk[:, cs]), 0.0)
                    for m, lq, lk in zip(level_masks, level_q, level_k))
        o_off = _dot(s_off.astype(BF16), vb[:, cs])
        o_inter = _dot_nt(q_all[:, cs], state.astype(BF16))
        outs.append(o_inter + o_off + jnp.concatenate(o_diag, axis=0))
        new_states.append(state * carry_decay[:, cs] + _dot_tn(vb[:, cs], k_d[:, cs]))
    return outs, new_states


def _hgrn_kernel(*refs, reverse, finish):
    if finish:
        q_ref, f_ref, v_ref, lb_ref, of_ref, g_ref, gain_ref, o_ref, st_ref = refs
    else:
        q_ref, f_ref, v_ref, lb_ref, o_ref, st_ref = refs
    C, SB = HG_CHUNK, HG_SUB
    nchunk = HG_TT // C

    @pl.when(pl.program_id(1) == 0)
    def _():
        st_ref[...] = jnp.zeros_like(st_ref)

    row = lax.broadcasted_iota(jnp.int32, (C, C), 0)
    col = lax.broadcasted_iota(jnp.int32, (C, C), 1)
    tri = ((col >= row) if reverse else (col <= row)).astype(BF16)
    s_i = lax.broadcasted_iota(jnp.int32, (SB, SB, HG_DK), 0)
    t_i = lax.broadcasted_iota(jnp.int32, (SB, SB, HG_DK), 1)
    pair_mask = (s_i >= t_i) if reverse else (s_i <= t_i)
    ones = jnp.ones((HG_DK, HG_DV), BF16)
    level_masks = []
    for size in HG_LEVELS:
        tg, sg = row // size, col // size
        later, earlier = (0, 1) if reverse else (1, 0)
        level_masks.append((tg // 2 == sg // 2) & (tg % 2 == later) & (sg % 2 == earlier))

    def chunk(ci, carry):
        c = (nchunk - 1 - ci) if reverse else ci
        sl = pl.ds(pl.multiple_of(c * C, C), C)
        outs, states = _hgrn_chunk(q_ref[sl, :], f_ref[sl, :], v_ref[sl, :], lb_ref[0],
                                   [st_ref[h] for h in range(HG_HEADS)], tri, pair_mask, level_masks, ones,
                                   reverse=reverse)
        for h in range(HG_HEADS):
            cs = slice(h * HG_DK, (h + 1) * HG_DK)
            o, st_ref[h] = outs[h], states[h]
            if finish:
                tot = of_ref[sl, cs] + o
                ms = jnp.mean(tot * tot, axis=-1, keepdims=True)
                g = g_ref[sl, cs]
                y = tot * lax.rsqrt(ms + EPS) * gain_ref[...] * (g * jax.nn.sigmoid(g))
                o_ref[sl, cs] = y.astype(o_ref.dtype)
            else:
                o_ref[sl, cs] = o
        return carry

    lax.fori_loop(0, nchunk, chunk, 0)


def _hgrn_pass(z, lb, batch, seq, *, reverse, o_fwd=None, gain=None):
    n = z.shape[0]
    nt = seq // HG_TT
    finish = o_fwd is not None
    d = 1 if reverse else 0

    def tmap(t):
        return (nt - 1 - t) if reverse else t

    def zspec(col0, per_dir=True):
        blk = col0 // HG_HEADS + (d if per_dir else 0)
        return pl.BlockSpec((HG_TT, HG_WIDTH), lambda b, t: (b * nt + tmap(t), blk))

    ospec = pl.BlockSpec((HG_TT, HG_WIDTH), lambda b, t: (b * nt + tmap(t), 0))
    in_specs = [zspec(_COL_HG_Q), zspec(_COL_HG_F), zspec(_COL_HG_I),
                pl.BlockSpec((1, 1, HG_WIDTH), lambda b, t: (d, 0, 0))]
    args = [z, z, z, lb]
    if finish:
        in_specs += [ospec, zspec(_COL_HG_G, per_dir=False), pl.BlockSpec((1, HG_DV), lambda b, t: (0, 0))]
        args += [o_fwd, z, gain]
    return pl.pallas_call(
        functools.partial(_hgrn_kernel, reverse=reverse, finish=finish),
        grid=(batch, nt),
        in_specs=in_specs,
        out_specs=ospec,
        out_shape=jax.ShapeDtypeStruct((n, HG_WIDTH), BF16 if finish else F32),
        scratch_shapes=[pltpu.VMEM((HG_HEADS, HG_DV, HG_DK), F32)],
        compiler_params=pltpu.CompilerParams(
            dimension_semantics=("parallel", "arbitrary"),
            vmem_limit_bytes=V7X_VMEM_LIMIT),
        name="hgrn_bwd" if reverse else "hgrn_fwd",
    )(*args)


NA_BAND = 8
NA_BT = NA_BAND * GRID_W


def _na_bias_table(rpb):
    c = jnp.arange(GRID_W)
    c0 = jnp.clip(c - NA_WIN_C // 2, 0, GRID_W - NA_WIN_C)
    kc = jnp.arange(GRID_W)
    valid = (kc[None, :] >= c0[:, None]) & (kc[None, :] < c0[:, None] + NA_WIN_C)
    off = jnp.clip(kc[None, :] - c[:, None] + NA_WIN_C - 1, 0, 2 * NA_WIN_C - 2)
    t = jnp.where(valid[None, None], rpb.astype(F32)[:, :, off], NEG_BIG)
    t = t.reshape(NA_HEADS // 2, 2, 2 * NA_WIN_R - 1, GRID_W, GRID_W)
    t = t.transpose(0, 2, 4, 1, 3)
    return t.reshape(NA_HEADS // 2, (2 * NA_WIN_R - 1) * GRID_W, 2 * GRID_W)


def _na_kernel(q_ref, k0_ref, k1_ref, k2_ref, v0_ref, v1_ref, v2_ref, bias_ref, o_ref,
               kcat_ref, vcat_ref, *, rows):
    g = pl.program_id(1)
    for i, (kr, vr) in enumerate(((k0_ref, v0_ref), (k1_ref, v1_ref), (k2_ref, v2_ref))):
        kcat_ref[i * NA_BT:(i + 1) * NA_BT, :] = kr[...]
        vcat_ref[i * NA_BT:(i + 1) * NA_BT, :] = vr[...]
    lane = lax.broadcasted_iota(jnp.int32, (1, 2 * NA_HEAD_DIM), 1)
    left = lane < NA_HEAD_DIM
    nkeys = NA_WIN_R * GRID_W
    scale = NA_HEAD_DIM ** -0.5
    pairs = range(NA_HEADS // 2)

    def body(j, carry):
        r = g * NA_BAND + j
        r0 = jnp.clip(r - NA_WIN_R // 2, 0, rows - NA_WIN_R)
        koff = pl.multiple_of((r0 - g * NA_BAND + NA_BAND) * GRID_W, GRID_W)
        boff = pl.multiple_of((r0 - r + NA_WIN_R - 1) * GRID_W, GRID_W)
        qsl = pl.ds(pl.multiple_of(j * GRID_W, GRID_W), GRID_W)
        qb = q_ref[qsl, :] * scale
        cols = [slice(hp * 128, (hp + 1) * 128) for hp in pairs]
        s_t = []
        for cs in cols:
            q2 = qb[:, cs]
            zero = jnp.zeros_like(q2)
            rhs_t = jnp.concatenate([jnp.where(left, q2, zero), jnp.where(left, zero, q2)], axis=0)
            s_t.append(_dot_nt(kcat_ref[pl.ds(koff, nkeys), cs], rhs_t))
        s_t = [s + bias_ref[hp, pl.ds(boff, nkeys), :] for hp, s in zip(pairs, s_t)]
        p = [jnp.exp(s - jnp.max(s, axis=0, keepdims=True)) for s in s_t]
        p = [(x * (1.0 / jnp.sum(x, axis=0, keepdims=True))).astype(BF16) for x in p]
        o2 = [_dot_tn(x, vcat_ref[pl.ds(koff, nkeys), cs]) for x, cs in zip(p, cols)]
        for cs, o in zip(cols, o2):
            o_ref[qsl, cs] = jnp.where(left, o[:GRID_W], o[GRID_W:]).astype(o_ref.dtype)
        return carry

    lax.fori_loop(0, NA_BAND, body, 0)


def _na(z, bias_tbl, batch, seq):
    n = z.shape[0]
    rows = seq // GRID_W
    nb = rows // NA_BAND

    def kv(colblk, shift):
        return pl.BlockSpec(
            (NA_BT, NA_WIDTH),
            lambda b, g: (b * nb + jnp.clip(g + shift, 0, nb - 1), colblk))

    return pl.pallas_call(
        functools.partial(_na_kernel, rows=rows),
        grid=(batch, nb),
        in_specs=[pl.BlockSpec((NA_BT, NA_WIDTH), lambda b, g: (b * nb + g, 0)),
                  kv(1, -1), kv(1, 0), kv(1, 1), kv(2, -1), kv(2, 0), kv(2, 1),
                  pl.BlockSpec(bias_tbl.shape, lambda b, g: (0, 0, 0))],
        out_specs=pl.BlockSpec((NA_BT, NA_WIDTH), lambda b, g: (b * nb + g, 0)),
        out_shape=jax.ShapeDtypeStruct((n, NA_WIDTH), BF16),
        scratch_shapes=[pltpu.VMEM((3 * NA_BT, NA_WIDTH), BF16),
                        pltpu.VMEM((3 * NA_BT, NA_WIDTH), BF16)],
        compiler_params=pltpu.CompilerParams(
            dimension_semantics=("parallel", "arbitrary"), vmem_limit_bytes=V7X_VMEM_LIMIT),
        name="natten",
    )(z, z, z, z, z, z, z, bias_tbl)


MG_TM = 512


def _merge_kernel(x_ref, oa_ref, ob_ref, ga_ref, gb_ref, wa_ref, wb_ref, wo_ref, nf_ref,
                  x2_ref, xt_ref):
    a = _dot(oa_ref[...], wa_ref[...])
    b = _dot(ob_ref[...], wb_ref[...])
    mix = jax.nn.sigmoid(ga_ref[...]) * a + jax.nn.sigmoid(gb_ref[...]) * b
    x2 = x_ref[...] + _dot(mix.astype(BF16), wo_ref[...])
    x2_ref[...] = x2
    ms = jnp.mean(x2 * x2, axis=-1, keepdims=True)
    xt_ref[...] = (x2 * lax.rsqrt(ms + EPS) * nf_ref[...]).T.astype(BF16)


def _merge(x2d, o_a, o_b, z, wa, wb, wo, norm_ffn):
    n = x2d.shape[0]
    col_ga = (_COL_HG_G * 128 + HG_WIDTH) // D_MODEL
    full = lambda shape: pl.BlockSpec(shape, lambda i: (0, 0))
    tok = lambda w, c=0: pl.BlockSpec((MG_TM, w), lambda i: (i, c))
    return pl.pallas_call(
        _merge_kernel,
        grid=(n // MG_TM,),
        in_specs=[tok(D_MODEL), tok(HG_WIDTH), tok(NA_WIDTH), tok(D_MODEL, col_ga), tok(D_MODEL, col_ga + 1),
                  full((HG_WIDTH, D_MODEL)), full((NA_WIDTH, D_MODEL)), full((D_MODEL, D_MODEL)),
                  full((1, D_MODEL))],
        out_specs=[tok(D_MODEL), pl.BlockSpec((D_MODEL, MG_TM), lambda i: (0, i))],
        out_shape=[jax.ShapeDtypeStruct((n, D_MODEL), F32), jax.ShapeDtypeStruct((D_MODEL, n), BF16)],
        compiler_params=pltpu.CompilerParams(
            dimension_semantics=("parallel",), vmem_limit_bytes=V7X_VMEM_LIMIT),
        name="merge",
    )(x2d, o_a, o_b, z, z, wa, wb, wo, norm_ffn)


PEER_TB = 512
PEER_EB = 1024
PEER_LC = 256
PEER_KC = 256


def _peer_route(xt_ref, wq_ref, keys_ref, qt_ref, s_ref, toph_ref, top_ref, cand_ref, theta_ref,
                cnt_ref, coef_ref, r1_ref, e1_ref):
    K = PEER_TOPK
    half_rows = wq_ref.shape[0] // 2
    for part in range(2):
        rs = slice(part * half_rows, (part + 1) * half_rows)
        qt_ref[rs, :] = _dot(wq_ref[rs, :], xt_ref[...]).astype(BF16)

    for hp in range(2 * PEER_HEADS):
        s_ref[hp] = _dot(keys_ref[hp], qt_ref[hp * PEER_D_HALF:(hp + 1) * PEER_D_HALF, :])

    n_grp = PEER_N_KEYS // 8
    assert n_grp == K

    def exchange(rows, i, j):
        rows[i], rows[j] = jnp.maximum(rows[i], rows[j]), jnp.minimum(rows[i], rows[j])

    def extract(h, carry, *, half):
        for lc in range(xt_ref.shape[1] // 128):
            ls = slice(lc * 128, (lc + 1) * 128)
            keys = [s_ref[2 * h + half, 8 * r:8 * r + 8, ls] for r in range(n_grp)]
            rows = list(keys)
            for i, j in _SORT16:
                exchange(rows, i, j)
            for shift in (4, 2, 1):
                rows = [jnp.maximum(rows[i], pltpu.roll(rows[K - 1 - i], shift, axis=0)) for i in range(K)]
                for i, j in _BITONIC16:
                    exchange(rows, i, j)
            toph_ref[half, h, :, ls] = jnp.concatenate([rows[q][0:1] for q in range(K)], axis=0)
            if half:
                for r in range(0, n_grp, 2):
                    rank = []
                    for k in keys[r:r + 2]:
                        rk = jnp.zeros_like(k)
                        for q in range(K):
                            rk = jnp.where(k < rows[q], float(q + 1), rk)
                        rank.append(rk)
                    r1_ref[h, 8 * r:8 * r + 16, ls] = jnp.concatenate(rank, axis=0).astype(BF16)
        return carry

    lax.fori_loop(0, PEER_HEADS, functools.partial(extract, half=0), 0)
    lax.fori_loop(0, PEER_HEADS, functools.partial(extract, half=1), 0)
    for half in range(2):
        for q in range(K):
            top_ref[half, q] = jnp.concatenate(
                [toph_ref[half, h, q:q + 1, :] for h in range(PEER_HEADS)], axis=0)

    ncand = len(_CAND_PAIRS)
    for c, (p, q) in enumerate(_CAND_PAIRS):
        cand_ref[c] = top_ref[0, p] + top_ref[1, q]
    small = cand_ref.shape[1:]

    def tau_step(it, carry):
        tau, cnt = carry
        m = cand_ref[0]
        for c in range(1, ncand):
            m = jnp.maximum(m, cand_ref[c])
        n_eq = jnp.zeros(small, F32)
        for c in range(ncand):
            v = cand_ref[c]
            hit = v == m
            n_eq = n_eq + jnp.where(hit, 1.0, 0.0)
            cand_ref[c] = jnp.where(hit, -jnp.inf, v)
        return jnp.where(cnt < float(K), m, tau), cnt + n_eq

    tau, _ = lax.fori_loop(0, K, tau_step, (jnp.full(small, -jnp.inf, F32), jnp.zeros(small, F32)))
    best = top_ref[0, 0] + top_ref[1, 0]
    zsum = jnp.zeros(small, F32)
    for p, q in _CAND_PAIRS:
        v = top_ref[0, p] + top_ref[1, q]
        zsum = zsum + jnp.where(v >= tau, jnp.exp(v - best), 0.0)
    inv_z = 1.0 / zsum

    for q in range(K):
        th = jnp.full(small, jnp.inf, F32)
        b_q = top_ref[1, q]
        for p in range(K):
            a_p = top_ref[0, p]
            th = jnp.minimum(th, jnp.where(a_p + b_q >= tau, a_p, jnp.inf))
        theta_ref[q] = th

    for h in range(PEER_HEADS):
        s0 = s_ref[2 * h]
        cnt = jnp.zeros(s0.shape, F32)
        for q in range(K):
            cnt = jnp.where(s0 >= theta_ref[q, h:h + 1, :], float(q + 1), cnt)
        cnt_ref[h] = cnt
        coef_ref[h] = jnp.exp(s0 - top_ref[0, 0, h:h + 1, :])
        e1_ref[h] = (jnp.exp(s_ref[2 * h + 1] - top_ref[1, 0, h:h + 1, :]) * (0.5 * inv_z[h:h + 1])).astype(BF16)


def _peer_kernel(xt_ref, x2_ref, wq_ref, keys_ref, u_ref, vt_ref, o_ref,
                 qt_ref, s_ref, toph_ref, top_ref, cand_ref, theta_ref,
                 cnt_ref, coef_ref, r1_ref, e1_ref, g_ref, acc_ref):
    e = pl.program_id(1)
    ne = pl.num_programs(1)

    @pl.when(e == 0)
    def _():
        _peer_route(xt_ref, wq_ref, keys_ref, qt_ref, s_ref, toph_ref, top_ref, cand_ref, theta_ref,
                    cnt_ref, coef_ref, r1_ref, e1_ref)
        acc_ref[...] = jnp.zeros_like(acc_ref)

    n_i = PEER_EB // PEER_N_KEYS
    i_rows = pl.ds(pl.multiple_of(e * n_i, n_i), n_i)
    h_t = _dot(u_ref[...], xt_ref[...])
    zero = jnp.zeros((PEER_N_KEYS, PEER_LC), BF16)
    for lc in range(PEER_TB // PEER_LC):
        ls = slice(lc * PEER_LC, (lc + 1) * PEER_LC)
        cnt_rows = [cnt_ref[h, i_rows, ls].astype(BF16) for h in range(PEER_HEADS)]
        coef_rows = [coef_ref[h, i_rows, ls].astype(BF16) for h in range(PEER_HEADS)]
        for ii in range(n_i):
            gate = zero
            for h in range(PEER_HEADS):
                sel = r1_ref[h, :, ls] < cnt_rows[h][ii:ii + 1]
                gate = gate + jnp.where(sel, e1_ref[h, :, ls], zero) * coef_rows[h][ii:ii + 1]
            g_ref[ii * PEER_N_KEYS:(ii + 1) * PEER_N_KEYS, ls] = gate
    y_t = None
    for c in range(PEER_EB // PEER_KC):
        rs = slice(c * PEER_KC, (c + 1) * PEER_KC)
        hh = h_t[rs]
        act = hh * (1.0 + lax.erf(hh * (2.0 ** -0.5)))
        part = _dot(vt_ref[:, rs], act.astype(BF16) * g_ref[rs, :])
        y_t = part if y_t is None else y_t + part
    acc_ref[...] += y_t

    @pl.when(e == ne - 1)
    def _():
        o_ref[...] = x2_ref[...] + acc_ref[...].T


def _peer(xt, x2, wq_t, keys, u_bf16, vt_bf16):
    n = x2.shape[0]
    tb, eb = PEER_TB, PEER_EB
    ne = PEER_N_EXPERTS // eb
    return pl.pallas_call(
        _peer_kernel,
        grid=(n // tb, PEER_N_EXPERTS // eb),
        in_specs=[
            pl.BlockSpec((D_MODEL, tb), lambda i, e: (0, i)),
            pl.BlockSpec((tb, D_MODEL), lambda i, e: (i, 0)),
            pl.BlockSpec(wq_t.shape, lambda i, e: (0, 0)),
            pl.BlockSpec(keys.shape, lambda i, e: (0, 0, 0)),
            pl.BlockSpec((eb, D_MODEL), lambda i, e: (e, 0)),
            pl.BlockSpec((D_MODEL, eb), lambda i, e: (0, e)),
        ],
        out_specs=pl.BlockSpec((tb, D_MODEL), lambda i, e: (i, 0)),
        out_shape=jax.ShapeDtypeStruct((n, D_MODEL), F32),
        scratch_shapes=[
            pltpu.VMEM((wq_t.shape[0], tb), BF16),
            pltpu.VMEM((2 * PEER_HEADS, PEER_N_KEYS, tb), F32),
            pltpu.VMEM((2, PEER_HEADS, PEER_TOPK, tb), F32),
            pltpu.VMEM((2, PEER_TOPK, PEER_HEADS, tb), F32),
            pltpu.VMEM((len(_CAND_PAIRS), PEER_HEADS, tb), F32),
            pltpu.VMEM((PEER_TOPK, PEER_HEADS, tb), F32),
            pltpu.VMEM((PEER_HEADS, PEER_N_KEYS, tb), F32),
            pltpu.VMEM((PEER_HEADS, PEER_N_KEYS, tb), F32),
            pltpu.VMEM((PEER_HEADS, PEER_N_KEYS, tb), BF16),
            pltpu.VMEM((PEER_HEADS, PEER_N_KEYS, tb), BF16),
            pltpu.VMEM((eb, tb), BF16),
            pltpu.VMEM((D_MODEL, tb), F32),
        ],
        compiler_params=pltpu.CompilerParams(
            dimension_semantics=("parallel", "arbitrary"), vmem_limit_bytes=V7X_VMEM_LIMIT),
        name="peer",
    )(xt, x2, wq_t, keys, u_bf16, vt_bf16)


def _trunk(x, p):
    batch, seq, _ = x.shape
    x2d = x.reshape(batch * seq, D_MODEL)
    z, qkv = _in_proj(x2d, p["norm_mix"], p["w_in"], p["qk_gain"], p["head_red"], p["head_exp"])
    o_f = _hgrn_pass(z, p["lb"], batch, seq, reverse=False)
    o_a = _hgrn_pass(z, p["lb"], batch, seq, reverse=True, o_fwd=o_f, gain=p["hg_out_norm"])
    o_b = _na(qkv, p["na_bias"], batch, seq)
    x2, xt = _merge(x2d, o_a, o_b, z, p["w_proj_a"], p["w_proj_b"], p["w_out"], p["norm_ffn"])
    y = _peer(xt, x2, p["wq_t"], p["keys"], p["expert_u"], p["expert_vt"])
    return y.reshape(batch, seq, D_MODEL)


def kernel(x_prompt, x_sample, norm_mix, w_in, lb_logits, hg_out_norm, q_norm, k_norm, rel_pos_bias,
           w_proj_a, w_proj_b, w_out, norm_ffn, w_query, sub_keys, expert_u, expert_v):
    l = 0
    lb_all = jnp.cumsum(jax.nn.softmax(lb_logits.astype(F32), axis=0), axis=0)
    head_id = np.arange(2 * NA_WIDTH) // NA_HEAD_DIM
    p = {
        "norm_mix": norm_mix[l].reshape(1, D_MODEL),
        "w_in": w_in[l].astype(BF16),
        "qk_gain": jnp.concatenate([jnp.tile(q_norm[l], NA_HEADS), jnp.tile(k_norm[l], NA_HEADS)]).reshape(1, -1),
        "head_red": jnp.asarray(head_id[:, None] == np.arange(128)[None, :], BF16),
        "head_exp": jnp.asarray(np.arange(128)[:, None] == head_id[None, :], BF16),
        "lb": lb_all[l].reshape(2, 1, HG_WIDTH),
        "hg_out_norm": hg_out_norm[l].reshape(1, HG_DV),
        "na_bias": _na_bias_table(rel_pos_bias[l]),
        "w_proj_a": w_proj_a[l].astype(BF16),
        "w_proj_b": w_proj_b[l].astype(BF16),
        "w_out": w_out[l].astype(BF16),
        "norm_ffn": norm_ffn[l].reshape(1, D_MODEL),
        "wq_t": w_query[l].T.astype(BF16),
        "keys": sub_keys[l].reshape(2 * PEER_HEADS, PEER_N_KEYS, PEER_D_HALF).astype(BF16),
        "expert_u": expert_u[l].astype(BF16),
        "expert_vt": expert_v[l].T.astype(BF16),
    }
    return (_trunk(x_prompt, p), _trunk(x_sample, p))
```

```python
import functools

import numpy as np
import jax
import jax.numpy as jnp
from jax import lax
from jax.experimental import pallas as pl
from jax.experimental.pallas import tpu as pltpu

F32 = jnp.float32
BF16 = jnp.bfloat16

D_MODEL = 1024
GRID_W = 64
EPS = 1e-6
HG_HEADS = 4
HG_DK = 128
HG_DV = 128
HG_WIDTH = HG_HEADS * HG_DV
HG_CHUNK = 64
HG_SUB = 16
HG_LEVELS = (32, 16)
assert HG_LEVELS[0] * 2 == HG_CHUNK and HG_LEVELS[-1] == HG_SUB
NA_HEADS = 8
NA_HEAD_DIM = 64
NA_WIDTH = NA_HEADS * NA_HEAD_DIM
NA_WIN_R = 8
NA_WIN_C = 16
PEER_HEADS = 8
PEER_N_KEYS = 128
PEER_N_EXPERTS = PEER_N_KEYS * PEER_N_KEYS
PEER_D_HALF = 128
PEER_TOPK = 16
D_IN = 3 * NA_WIDTH + 3 * 2 * HG_HEADS * HG_DK + HG_WIDTH + 2 * D_MODEL

_COL_HG_Q = (3 * NA_WIDTH) // 128
_COL_HG_F = _COL_HG_Q + 8
_COL_HG_I = _COL_HG_F + 8
_COL_HG_G = _COL_HG_I + 8

V7X_VMEM_LIMIT = 56 * 1024 * 1024
NEG_BIG = -1e30

_CAND_PAIRS = tuple((p, q) for p in range(PEER_TOPK) for q in range(PEER_TOPK)
                    if (p + 1) * (q + 1) <= PEER_TOPK)


def _sorting_network(n):
    def merge(lo, hi, r):
        step = r * 2
        if step < hi - lo:
            yield from merge(lo, hi, step)
            yield from merge(lo + r, hi, step)
            yield from ((i, i + r) for i in range(lo + r, hi - r, step))
        else:
            yield (lo, lo + r)

    def sort(lo, hi):
        if hi - lo >= 1:
            mid = lo + (hi - lo) // 2
            yield from sort(lo, mid)
            yield from sort(mid + 1, hi)
            yield from merge(lo, hi, 1)

    return tuple(sort(0, n - 1))


_SORT16 = _sorting_network(PEER_TOPK)
_BITONIC16 = tuple((i, i + d) for d in (8, 4, 2, 1) for i in range(PEER_TOPK) if not i & d)


def _dot(a, b):
    return jnp.dot(a, b, preferred_element_type=F32)


def _dot_nt(a, b):
    return lax.dot_general(a, b, (((1,), (1,)), ((), ())), preferred_element_type=F32)


def _dot_tn(a, b):
    return lax.dot_general(a, b, (((0,), (0,)), ((), ())), preferred_element_type=F32)


def _split_bf16(x):
    hi = x.astype(BF16)
    lo = (x - hi.astype(F32)).astype(BF16)
    return hi, lo


IN_TM = 256
IN_TN = 1024


def _in_proj_kernel(x_ref, g_ref, w_ref, qkg_ref, red_ref, exp_ref, z_ref, qkv_ref):
    x = x_ref[...]
    ms = jnp.mean(x * x, axis=-1, keepdims=True)
    xn = (x * lax.rsqrt(ms + EPS) * g_ref[...]).astype(BF16)
    for j in range(D_IN // IN_TN):
        cs = slice(j * IN_TN, (j + 1) * IN_TN)
        z = _dot(xn, w_ref[:, cs])
        if j == 0:
            hi, lo = _split_bf16(z * z)
            hi, lo = _split_bf16(_dot(hi, red_ref[...]) + _dot(lo, red_ref[...]))
            ss = _dot(hi, exp_ref[...]) + _dot(lo, exp_ref[...])
            z = z * lax.rsqrt(ss * (1.0 / NA_HEAD_DIM) + EPS) * qkg_ref[...]
        z_ref[:, cs] = z
        lo_c, hi_c = j * IN_TN, min((j + 1) * IN_TN, 3 * NA_WIDTH)
        if hi_c > lo_c:
            qkv_ref[:, lo_c:hi_c] = z[:, :hi_c - lo_c].astype(BF16)


def _in_proj(x2d, gain, w_bf16, qk_gain, head_red, head_exp):
    n = x2d.shape[0]
    const = lambda shape: pl.BlockSpec(shape, lambda i: (0, 0), pipeline_mode=pl.Buffered(1))
    return pl.pallas_call(
        _in_proj_kernel,
        grid=(n // IN_TM,),
        in_specs=[
            pl.BlockSpec((IN_TM, D_MODEL), lambda i: (i, 0)),
            const((1, D_MODEL)),
            const((D_MODEL, D_IN)),
            const((1, IN_TN)),
            const(head_red.shape),
            const(head_exp.shape),
        ],
        out_specs=[pl.BlockSpec((IN_TM, D_IN), lambda i: (i, 0)),
                   pl.BlockSpec((IN_TM, 3 * NA_WIDTH), lambda i: (i, 0))],
        out_shape=[jax.ShapeDtypeStruct((n, D_IN), F32), jax.ShapeDtypeStruct((n, 3 * NA_WIDTH), BF16)],
        compiler_params=pltpu.CompilerParams(
            dimension_semantics=("parallel",), vmem_limit_bytes=V7X_VMEM_LIMIT),
        name="in_proj",
    )(x2d, gain, w_bf16, qk_gain, head_red, head_exp)


HG_TT = 512


def _hgrn_chunk(q, f_logit, v, lb, states, tri, pair_mask, level_masks, ones, *, reverse):
    C, SB = HG_CHUNK, HG_SUB
    nblk = C // SB
    f = lb + (1.0 - lb) * jax.nn.sigmoid(f_logit)
    qs = q * jax.nn.sigmoid(q)
    hi, lo = _split_bf16(jnp.log(f))
    b = _dot(tri, hi) + _dot(tri, lo)
    c = b - jnp.log(1.0 - f)
    vb = v.astype(BF16)
    q_all = (qs * jnp.exp(b)).astype(BF16)
    b_end = b[0:1] if reverse else b[C - 1:C]
    k_d = jnp.exp(b_end - c).astype(BF16)
    carry_decay = jnp.exp(b_end)

    def boundary_refs(size):
        q_ref, k_ref = [], []
        for g0 in range(0, C, 2 * size):
            first, second = slice(g0, g0 + size), slice(g0 + size, g0 + 2 * size)
            if reverse:
                bound = jnp.broadcast_to(b[g0 + size:g0 + size + 1], (size, b.shape[1]))
                q_ref += [bound, b[second]]
                k_ref += [c[first], bound]
            else:
                bound = jnp.broadcast_to(b[g0 + size - 1:g0 + size], (size, b.shape[1]))
                q_ref += [b[first], bound]
                k_ref += [bound, c[second]]
        return jnp.concatenate(q_ref, axis=0), jnp.concatenate(k_ref, axis=0)

    level_q, level_k = [], []
    for size in HG_LEVELS:
        q_bound, k_bound = boundary_refs(size)
        level_q.append((qs * jnp.exp(b - q_bound)).astype(BF16))
        level_k.append(jnp.exp(k_bound - c).astype(BF16))

    outs, new_states = [], []
    for h, state in enumerate(states):
        cs = slice(h * HG_DK, (h + 1) * HG_DK)
        prods = []
        for blk in range(nblk):
            rows = slice(blk * SB, (blk + 1) * SB)
            prod = (qs[rows, cs][None, :, :] * jnp.exp(b[rows, cs][None, :, :] - c[rows, cs][:, None, :]))
            prods.append(jnp.where(pair_mask, prod, 0.0).reshape(SB * SB, HG_DK))
        rs = _dot(jnp.concatenate(prods, axis=0).astype(BF16), ones)
        rs = rs.reshape(nblk, SB, SB, HG_DV)
        o_diag = [jnp.sum(rs[blk] * v[blk * SB:(blk + 1) * SB, cs][:, None, :], axis=0) for blk in range(nblk)]
        s_off = sum(jnp.where(m, _dot_nt(lq[:, cs], lk[:, cs]), 0.0)
                    for m, lq, lk in zip(level_masks, level_q, level_k))
        o_off = _dot(s_off.astype(BF16), vb[:, cs])
        o_inter = _dot_nt(q_all[:, cs], state.astype(BF16))
        outs.append(o_inter + o_off + jnp.concatenate(o_diag, axis=0))
        new_states.append(state * carry_decay[:, cs] + _dot_tn(vb[:, cs], k_d[:, cs]))
    return outs, new_states


def _hgrn_kernel(*refs, reverse, finish):
    if finish:
        q_ref, f_ref, v_ref, lb_ref, of_ref, g_ref, gain_ref, o_ref, st_ref = refs
    else:
        q_ref, f_ref, v_ref, lb_ref, o_ref, st_ref = refs
    C, SB = HG_CHUNK, HG_SUB
    nchunk = HG_TT // C

    @pl.when(pl.program_id(1) == 0)
    def _():
        st_ref[...] = jnp.zeros_like(st_ref)

    row = lax.broadcasted_iota(jnp.int32, (C, C), 0)
    col = lax.broadcasted_iota(jnp.int32, (C, C), 1)
    tri = ((col >= row) if reverse else (col <= row)).astype(BF16)
    s_i = lax.broadcasted_iota(jnp.int32, (SB, SB, HG_DK), 0)
    t_i = lax.broadcasted_iota(jnp.int32, (SB, SB, HG_DK), 1)
    pair_mask = (s_i >= t_i) if reverse else (s_i <= t_i)
    ones = jnp.ones((HG_DK, HG_DV), BF16)
    level_masks = []
    for size in HG_LEVELS:
        tg, sg = row // size, col // size
        later, earlier = (0, 1) if reverse else (1, 0)
        level_masks.append((tg // 2 == sg // 2) & (tg % 2 == later) & (sg % 2 == earlier))

    def chunk(ci, carry):
        c = (nchunk - 1 - ci) if reverse else ci
        sl = pl.ds(pl.multiple_of(c * C, C), C)
        outs, states = _hgrn_chunk(q_ref[sl, :], f_ref[sl, :], v_ref[sl, :], lb_ref[0],
                                   [st_ref[h] for h in range(HG_HEADS)], tri, pair_mask, level_masks, ones,
                                   reverse=reverse)
        for h in range(HG_HEADS):
            cs = slice(h * HG_DK, (h + 1) * HG_DK)
            o, st_ref[h] = outs[h], states[h]
            if finish:
                tot = of_ref[sl, cs] + o
                ms = jnp.mean(tot * tot, axis=-1, keepdims=True)
                g = g_ref[sl, cs]
                y = tot * lax.rsqrt(ms + EPS) * gain_ref[...] * (g * jax.nn.sigmoid(g))
                o_ref[sl, cs] = y.astype(o_ref.dtype)
            else:
                o_ref[sl, cs] = o
        return carry

    lax.fori_loop(0, nchunk, chunk, 0)


def _hgrn_pass(z, lb, batch, seq, *, reverse, o_fwd=None, gain=None):
    n = z.shape[0]
    nt = seq // HG_TT
    finish = o_fwd is not None
    d = 1 if reverse else 0

    def tmap(t):
        return (nt - 1 - t) if reverse else t

    def zspec(col0, per_dir=True):
        blk = col0 // HG_HEADS + (d if per_dir else 0)
        return pl.BlockSpec((HG_TT, HG_WIDTH), lambda b, t: (b * nt + tmap(t), blk))

    ospec = pl.BlockSpec((HG_TT, HG_WIDTH), lambda b, t: (b * nt + tmap(t), 0))
    in_specs = [zspec(_COL_HG_Q), zspec(_COL_HG_F), zspec(_COL_HG_I),
                pl.BlockSpec((1, 1, HG_WIDTH), lambda b, t: (d, 0, 0))]
    args = [z, z, z, lb]
    if finish:
        in_specs += [ospec, zspec(_COL_HG_G, per_dir=False), pl.BlockSpec((1, HG_DV), lambda b, t: (0, 0))]
        args += [o_fwd, z, gain]
    return pl.pallas_call(
        functools.partial(_hgrn_kernel, reverse=reverse, finish=finish),
        grid=(batch, nt),
        in_specs=in_specs,
        out_specs=ospec,
        out_shape=jax.ShapeDtypeStruct((n, HG_WIDTH), BF16 if finish else F32),
        scratch_shapes=[pltpu.VMEM((HG_HEADS, HG_DV, HG_DK), F32)],
        compiler_params=pltpu.CompilerParams(
            dimension_semantics=("parallel", "arbitrary"),
            vmem_limit_bytes=V7X_VMEM_LIMIT),
        name="hgrn_bwd" if reverse else "hgrn_fwd",
    )(*args)


NA_BAND = 8
NA_BT = NA_BAND * GRID_W


def _na_bias_table(rpb):
    c = jnp.arange(GRID_W)
    c0 = jnp.clip(c - NA_WIN_C // 2, 0, GRID_W - NA_WIN_C)
    kc = jnp.arange(GRID_W)
    valid = (kc[None, :] >= c0[:, None]) & (kc[None, :] < c0[:, None] + NA_WIN_C)
    off = jnp.clip(kc[None, :] - c[:, None] + NA_WIN_C - 1, 0, 2 * NA_WIN_C - 2)
    t = jnp.where(valid[None, None], rpb.astype(F32)[:, :, off], NEG_BIG)
    t = t.reshape(NA_HEADS // 2, 2, 2 * NA_WIN_R - 1, GRID_W, GRID_W)
    t = t.transpose(0, 2, 4, 1, 3)
    return t.reshape(NA_HEADS // 2, (2 * NA_WIN_R - 1) * GRID_W, 2 * GRID_W)


def _na_kernel(q_ref, k0_ref, k1_ref, k2_ref, v0_ref, v1_ref, v2_ref, bias_ref, o_ref,
               kcat_ref, vcat_ref, *, rows):
    g = pl.program_id(1)
    for i, (kr, vr) in enumerate(((k0_ref, v0_ref), (k1_ref, v1_ref), (k2_ref, v2_ref))):
        kcat_ref[i * NA_BT:(i + 1) * NA_BT, :] = kr[...]
        vcat_ref[i * NA_BT:(i + 1) * NA_BT, :] = vr[...]
    lane = lax.broadcasted_iota(jnp.int32, (1, 2 * NA_HEAD_DIM), 1)
    left = lane < NA_HEAD_DIM
    nkeys = NA_WIN_R * GRID_W
    scale = NA_HEAD_DIM ** -0.5
    pairs = range(NA_HEADS // 2)

    def body(j, carry):
        r = g * NA_BAND + j
        r0 = jnp.clip(r - NA_WIN_R // 2, 0, rows - NA_WIN_R)
        koff = pl.multiple_of((r0 - g * NA_BAND + NA_BAND) * GRID_W, GRID_W)
        boff = pl.multiple_of((r0 - r + NA_WIN_R - 1) * GRID_W, GRID_W)
        qsl = pl.ds(pl.multiple_of(j * GRID_W, GRID_W), GRID_W)
        qb = q_ref[qsl, :] * scale
        cols = [slice(hp * 128, (hp + 1) * 128) for hp in pairs]
        s_t = []
        for cs in cols:
            q2 = qb[:, cs]
            zero = jnp.zeros_like(q2)
            rhs_t = jnp.concatenate([jnp.where(left, q2, zero), jnp.where(left, zero, q2)], axis=0)
            s_t.append(_dot_nt(kcat_ref[pl.ds(koff, nkeys), cs], rhs_t))
        s_t = [s + bias_ref[hp, pl.ds(boff, nkeys), :] for hp, s in zip(pairs, s_t)]
        p = [jnp.exp(s - jnp.max(s, axis=0, keepdims=True)) for s in s_t]
        p = [(x * (1.0 / jnp.sum(x, axis=0, keepdims=True))).astype(BF16) for x in p]
        o2 = [_dot_tn(x, vcat_ref[pl.ds(koff, nkeys), cs]) for x, cs in zip(p, cols)]
        for cs, o in zip(cols, o2):
            o_ref[qsl, cs] = jnp.where(left, o[:GRID_W], o[GRID_W:]).astype(o_ref.dtype)
        return carry

    lax.fori_loop(0, NA_BAND, body, 0)


def _na(z, bias_tbl, batch, seq):
    n = z.shape[0]
    rows = seq // GRID_W
    nb = rows // NA_BAND

    def kv(colblk, shift):
        return pl.BlockSpec(
            (NA_BT, NA_WIDTH),
            lambda b, g: (b * nb + jnp.clip(g + shift, 0, nb - 1), colblk))

    return pl.pallas_call(
        functools.partial(_na_kernel, rows=rows),
        grid=(batch, nb),
        in_specs=[pl.BlockSpec((NA_BT, NA_WIDTH), lambda b, g: (b * nb + g, 0)),
                  kv(1, -1), kv(1, 0), kv(1, 1), kv(2, -1), kv(2, 0), kv(2, 1),
                  pl.BlockSpec(bias_tbl.shape, lambda b, g: (0, 0, 0))],
        out_specs=pl.BlockSpec((NA_BT, NA_WIDTH), lambda b, g: (b * nb + g, 0)),
        out_shape=jax.ShapeDtypeStruct((n, NA_WIDTH), BF16),
        scratch_shapes=[pltpu.VMEM((3 * NA_BT, NA_WIDTH), BF16),
                        pltpu.VMEM((3 * NA_BT, NA_WIDTH), BF16)],
        compiler_params=pltpu.CompilerParams(
            dimension_semantics=("parallel", "arbitrary"), vmem_limit_bytes=V7X_VMEM_LIMIT),
        name="natten",
    )(z, z, z, z, z, z, z, bias_tbl)


MG_TM = 512


def _merge_kernel(x_ref, oa_ref, ob_ref, ga_ref, gb_ref, wa_ref, wb_ref, wo_ref, nf_ref,
                  x2_ref, xt_ref):
    a = _dot(oa_ref[...], wa_ref[...])
    b = _dot(ob_ref[...], wb_ref[...])
    mix = jax.nn.sigmoid(ga_ref[...]) * a + jax.nn.sigmoid(gb_ref[...]) * b
    x2 = x_ref[...] + _dot(mix.astype(BF16), wo_ref[...])
    x2_ref[...] = x2
    ms = jnp.mean(x2 * x2, axis=-1, keepdims=True)
    xt_ref[...] = (x2 * lax.rsqrt(ms + EPS) * nf_ref[...]).T.astype(BF16)


def _merge(x2d, o_a, o_b, z, wa, wb, wo, norm_ffn):
    n = x2d.shape[0]
    col_ga = (_COL_HG_G * 128 + HG_WIDTH) // D_MODEL
    full = lambda shape: pl.BlockSpec(shape, lambda i: (0, 0))
    tok = lambda w, c=0: pl.BlockSpec((MG_TM, w), lambda i: (i, c))
    return pl.pallas_call(
        _merge_kernel,
        grid=(n // MG_TM,),
        in_specs=[tok(D_MODEL), tok(HG_WIDTH), tok(NA_WIDTH), tok(D_MODEL, col_ga), tok(D_MODEL, col_ga + 1),
                  full((HG_WIDTH, D_MODEL)), full((NA_WIDTH, D_MODEL)), full((D_MODEL, D_MODEL)),
                  full((1, D_MODEL))],
        out_specs=[tok(D_MODEL), pl.BlockSpec((D_MODEL, MG_TM), lambda i: (0, i))],
        out_shape=[jax.ShapeDtypeStruct((n, D_MODEL), F32), jax.ShapeDtypeStruct((D_MODEL, n), BF16)],
        compiler_params=pltpu.CompilerParams(
            dimension_semantics=("parallel",), vmem_limit_bytes=V7X_VMEM_LIMIT),
        name="merge",
    )(x2d, o_a, o_b, z, z, wa, wb, wo, norm_ffn)


PEER_TB = 512
PEER_EB = 1024
PEER_LC = 256
PEER_KC = 256
PEER_UK = 256


def _peer_route(xt_ref, wq_ref, keys_ref, qt_ref, s_ref, toph_ref, top_ref, cand_ref, theta_ref,
                cnt_ref, coef_ref, r1_ref, e1_ref):
    K = PEER_TOPK
    half_rows = wq_ref.shape[0] // 2
    for part in range(2):
        rs = slice(part * half_rows, (part + 1) * half_rows)
        qt_ref[rs, :] = _dot(wq_ref[rs, :], xt_ref[...]).astype(BF16)

    for hp in range(2 * PEER_HEADS):
        s_ref[hp] = _dot(keys_ref[hp], qt_ref[hp * PEER_D_HALF:(hp + 1) * PEER_D_HALF, :])

    n_grp = PEER_N_KEYS // 8
    assert n_grp == K

    def exchange(rows, i, j):
        rows[i], rows[j] = jnp.maximum(rows[i], rows[j]), jnp.minimum(rows[i], rows[j])

    def extract(h, carry, *, half):
        for lc in range(xt_ref.shape[1] // 128):
            ls = slice(lc * 128, (lc + 1) * 128)
            keys = [s_ref[2 * h + half, 8 * r:8 * r + 8, ls] for r in range(n_grp)]
            rows = list(keys)
            for i, j in _SORT16:
                exchange(rows, i, j)
            for shift in (4, 2, 1):
                rows = [jnp.maximum(rows[i], pltpu.roll(rows[K - 1 - i], shift, axis=0)) for i in range(K)]
                for i, j in _BITONIC16:
                    exchange(rows, i, j)
            toph_ref[half, h, :, ls] = jnp.concatenate([rows[q][0:1] for q in range(K)], axis=0)
            if half:
                for r in range(0, n_grp, 2):
                    rank = []
                    for k in keys[r:r + 2]:
                        rk = jnp.zeros_like(k)
                        for q in range(K):
                            rk = jnp.where(k < rows[q], float(q + 1), rk)
                        rank.append(rk)
                    r1_ref[h, 8 * r:8 * r + 16, ls] = jnp.concatenate(rank, axis=0).astype(BF16)
        return carry

    lax.fori_loop(0, PEER_HEADS, functools.partial(extract, half=0), 0)
    lax.fori_loop(0, PEER_HEADS, functools.partial(extract, half=1), 0)
    for half in range(2):
        for q in range(K):
            top_ref[half, q] = jnp.concatenate(
                [toph_ref[half, h, q:q + 1, :] for h in range(PEER_HEADS)], axis=0)

    ncand = len(_CAND_PAIRS)
    for c, (p, q) in enumerate(_CAND_PAIRS):
        cand_ref[c] = top_ref[0, p] + top_ref[1, q]
    small = cand_ref.shape[1:]

    def tau_step(it, carry):
        tau, cnt = carry
        m = cand_ref[0]
        for c in range(1, ncand):
            m = jnp.maximum(m, cand_ref[c])
        n_eq = jnp.zeros(small, F32)
        for c in range(ncand):
            v = cand_ref[c]
            hit = v == m
            n_eq = n_eq + jnp.where(hit, 1.0, 0.0)
            cand_ref[c] = jnp.where(hit, -jnp.inf, v)
        return jnp.where(cnt < float(K), m, tau), cnt + n_eq

    tau, _ = lax.fori_loop(0, K, tau_step, (jnp.full(small, -jnp.inf, F32), jnp.zeros(small, F32)))
    best = top_ref[0, 0] + top_ref[1, 0]
    zsum = jnp.zeros(small, F32)
    for p, q in _CAND_PAIRS:
        v = top_ref[0, p] + top_ref[1, q]
        zsum = zsum + jnp.where(v >= tau, jnp.exp(v - best), 0.0)
    inv_z = 1.0 / zsum

    for q in range(K):
        th = jnp.full(small, jnp.inf, F32)
        b_q = top_ref[1, q]
        for p in range(K):
            a_p = top_ref[0, p]
            th = jnp.minimum(th, jnp.where(a_p + b_q >= tau, a_p, jnp.inf))
        theta_ref[q] = th

    for h in range(PEER_HEADS):
        s0 = s_ref[2 * h]
        cnt = jnp.zeros(s0.shape, F32)
        for q in range(K):
            cnt = jnp.where(s0 >= theta_ref[q, h:h + 1, :], float(q + 1), cnt)
        cnt_ref[h] = cnt
        coef_ref[h] = jnp.exp(s0 - top_ref[0, 0, h:h + 1, :])
        e1_ref[h] = (jnp.exp(s_ref[2 * h + 1] - top_ref[1, 0, h:h + 1, :]) * (0.5 * inv_z[h:h + 1])).astype(BF16)


def _peer_kernel(xt_ref, x2_ref, wq_ref, keys_ref, u_ref, vt_ref, o_ref,
                 qt_ref, s_ref, toph_ref, top_ref, cand_ref, theta_ref,
                 cnt_ref, coef_ref, r1_ref, e1_ref, g_ref, acc_ref):
    e = pl.program_id(1)
    ne = pl.num_programs(1)

    @pl.when(e == 0)
    def _():
        _peer_route(xt_ref, wq_ref, keys_ref, qt_ref, s_ref, toph_ref, top_ref, cand_ref, theta_ref,
                    cnt_ref, coef_ref, r1_ref, e1_ref)
        acc_ref[...] = jnp.zeros_like(acc_ref)

    n_i = PEER_EB // PEER_N_KEYS
    i_rows = pl.ds(pl.multiple_of(e * n_i, n_i), n_i)
    zero = jnp.zeros((PEER_N_KEYS, PEER_LC), BF16)
    n_lc = PEER_TB // PEER_LC
    cnt_rows = [[cnt_ref[h, i_rows, lc * PEER_LC:(lc + 1) * PEER_LC].astype(BF16) for h in range(PEER_HEADS)]
                for lc in range(n_lc)]
    coef_rows = [[coef_ref[h, i_rows, lc * PEER_LC:(lc + 1) * PEER_LC].astype(BF16) for h in range(PEER_HEADS)]
                 for lc in range(n_lc)]
    anchors = []
    for ii in range(n_i):
        for lc in range(n_lc):
            ls = slice(lc * PEER_LC, (lc + 1) * PEER_LC)
            gate = zero
            for h in range(PEER_HEADS):
                sel = r1_ref[h, :, ls] < cnt_rows[lc][h][ii:ii + 1]
                gate = gate + jnp.where(sel, e1_ref[h, :, ls], zero) * coef_rows[lc][h][ii:ii + 1]
            g_ref[ii * PEER_N_KEYS:(ii + 1) * PEER_N_KEYS, ls] = gate
            bits = pltpu.bitcast(gate, jnp.uint32)
            anchors.append(pltpu.bitcast((bits >> 16) >> 16, BF16))
    anchors = anchors[:len(anchors) - n_lc * PEER_KC // PEER_N_KEYS]

    n_k = D_MODEL // PEER_UK
    free_slots = (n_k - 1) * n_i
    anchor_of = {n_i + t * free_slots // len(anchors): a for t, a in enumerate(anchors)}
    cols = []
    for k in range(n_k):
        pieces = []
        for rg in range(n_i):
            piece = u_ref[rg * PEER_N_KEYS:(rg + 1) * PEER_N_KEYS, k * PEER_UK:(k + 1) * PEER_UK]
            if k * n_i + rg in anchor_of:
                piece = piece + anchor_of[k * n_i + rg]
            pieces.append(piece)
        cols.append(jnp.concatenate(pieces, axis=0))
    h_t = _dot(jnp.concatenate(cols, axis=1), xt_ref[...])
    y_t = None
    for c in range(PEER_EB // PEER_KC):
        rs = slice(c * PEER_KC, (c + 1) * PEER_KC)
        hh = h_t[rs]
        act = hh * (1.0 + lax.erf(hh * (2.0 ** -0.5)))
        part = _dot(vt_ref[:, rs], act.astype(BF16) * g_ref[rs, :])
        y_t = part if y_t is None else y_t + part
    acc_ref[...] += y_t

    @pl.when(e == ne - 1)
    def _():
        o_ref[...] = x2_ref[...] + acc_ref[...].T


def _peer(xt, x2, wq_t, keys, u_bf16, vt_bf16):
    n = x2.shape[0]
    tb, eb = PEER_TB, PEER_EB
    ne = PEER_N_EXPERTS // eb
    return pl.pallas_call(
        _peer_kernel,
        grid=(n // tb, PEER_N_EXPERTS // eb),
        in_specs=[
            pl.BlockSpec((D_MODEL, tb), lambda i, e: (0, i)),
            pl.BlockSpec((tb, D_MODEL), lambda i, e: (i, 0)),
            pl.BlockSpec(wq_t.shape, lambda i, e: (0, 0)),
            pl.BlockSpec(keys.shape, lambda i, e: (0, 0, 0)),
            pl.BlockSpec((eb, D_MODEL), lambda i, e: (e, 0)),
            pl.BlockSpec((D_MODEL, eb), lambda i, e: (0, e)),
        ],
        out_specs=pl.BlockSpec((tb, D_MODEL), lambda i, e: (i, 0)),
        out_shape=jax.ShapeDtypeStruct((n, D_MODEL), F32),
        scratch_shapes=[
            pltpu.VMEM((wq_t.shape[0], tb), BF16),
            pltpu.VMEM((2 * PEER_HEADS, PEER_N_KEYS, tb), F32),
            pltpu.VMEM((2, PEER_HEADS, PEER_TOPK, tb), F32),
            pltpu.VMEM((2, PEER_TOPK, PEER_HEADS, tb), F32),
            pltpu.VMEM((len(_CAND_PAIRS), PEER_HEADS, tb), F32),
            pltpu.VMEM((PEER_TOPK, PEER_HEADS, tb), F32),
            pltpu.VMEM((PEER_HEADS, PEER_N_KEYS, tb), F32),
            pltpu.VMEM((PEER_HEADS, PEER_N_KEYS, tb), F32),
            pltpu.VMEM((PEER_HEADS, PEER_N_KEYS, tb), BF16),
            pltpu.VMEM((PEER_HEADS, PEER_N_KEYS, tb), BF16),
            pltpu.VMEM((eb, tb), BF16),
            pltpu.VMEM((D_MODEL, tb), F32),
        ],
        compiler_params=pltpu.CompilerParams(
            dimension_semantics=("parallel", "arbitrary"), vmem_limit_bytes=V7X_VMEM_LIMIT),
        name="peer",
    )(xt, x2, wq_t, keys, u_bf16, vt_bf16)


def _trunk(x, p):
    batch, seq, _ = x.shape
    x2d = x.reshape(batch * seq, D_MODEL)
    z, qkv = _in_proj(x2d, p["norm_mix"], p["w_in"], p["qk_gain"], p["head_red"], p["head_exp"])
    o_f = _hgrn_pass(z, p["lb"], batch, seq, reverse=False)
    o_a = _hgrn_pass(z, p["lb"], batch, seq, reverse=True, o_fwd=o_f, gain=p["hg_out_norm"])
    o_b = _na(qkv, p["na_bias"], batch, seq)
    x2, xt = _merge(x2d, o_a, o_b, z, p["w_proj_a"], p["w_proj_b"], p["w_out"], p["norm_ffn"])
    y = _peer(xt, x2, p["wq_t"], p["keys"], p["expert_u"], p["expert_vt"])
    return y.reshape(batch, seq, D_MODEL)


def kernel(x_prompt, x_sample, norm_mix, w_in, lb_logits, hg_out_norm, q_norm, k_norm, rel_pos_bias,
           w_proj_a, w_proj_b, w_out, norm_ffn, w_query, sub_keys, expert_u, expert_v):
    l = 0
    lb_all = jnp.cumsum(jax.nn.softmax(lb_logits.astype(F32), axis=0), axis=0)
    head_id = np.arange(2 * NA_WIDTH) // NA_HEAD_DIM
    p = {
        "norm_mix": norm_mix[l].reshape(1, D_MODEL),
        "w_in": w_in[l].astype(BF16),
        "qk_gain": jnp.concatenate([jnp.tile(q_norm[l], NA_HEADS), jnp.tile(k_norm[l], NA_HEADS)]).reshape(1, -1),
        "head_red": jnp.asarray(head_id[:, None] == np.arange(128)[None, :], BF16),
        "head_exp": jnp.asarray(np.arange(128)[:, None] == head_id[None, :], BF16),
        "lb": lb_all[l].reshape(2, 1, HG_WIDTH),
        "hg_out_norm": hg_out_norm[l].reshape(1, HG_DV),
        "na_bias": _na_bias_table(rel_pos_bias[l]),
        "w_proj_a": w_proj_a[l].astype(BF16),
        "w_proj_b": w_proj_b[l].astype(BF16),
        "w_out": w_out[l].astype(BF16),
        "norm_ffn": norm_ffn[l].reshape(1, D_MODEL),
        "wq_t": w_query[l].T.astype(BF16),
        "keys": sub_keys[l].reshape(2 * PEER_HEADS, PEER_N_KEYS, PEER_D_HALF).astype(BF16),
        "expert_u": expert_u[l].astype(BF16),
        "expert_vt": expert_v[l].T.astype(BF16),
    }
    return (_trunk(x_prompt, p), _trunk(x_sample, p))
```

```python
import functools

import numpy as np
import jax
import jax.numpy as jnp
from jax import lax
from jax.experimental import pallas as pl
from jax.experimental.pallas import tpu as pltpu

F32 = jnp.float32
BF16 = jnp.bfloat16

D_MODEL = 1024
GRID_W = 64
EPS = 1e-6
HG_HEADS = 4
HG_DK = 128
HG_DV = 128
HG_WIDTH = HG_HEADS * HG_DV
HG_CHUNK = 64
HG_SUB = 16
HG_LEVELS = (32, 16)
assert HG_LEVELS[0] * 2 == HG_CHUNK and HG_LEVELS[-1] == HG_SUB
NA_HEADS = 8
NA_HEAD_DIM = 64
NA_WIDTH = NA_HEADS * NA_HEAD_DIM
NA_WIN_R = 8
NA_WIN_C = 16
PEER_HEADS = 8
PEER_N_KEYS = 128
PEER_N_EXPERTS = PEER_N_KEYS * PEER_N_KEYS
PEER_D_HALF = 128
PEER_TOPK = 16
D_IN = 3 * NA_WIDTH + 3 * 2 * HG_HEADS * HG_DK + HG_WIDTH + 2 * D_MODEL

_COL_HG_Q = (3 * NA_WIDTH) // 128
_COL_HG_F = _COL_HG_Q + 8
_COL_HG_I = _COL_HG_F + 8
_COL_HG_G = _COL_HG_I + 8

V7X_VMEM_LIMIT = 56 * 1024 * 1024
NEG_BIG = -1e30

_CAND_PAIRS = tuple((p, q) for p in range(PEER_TOPK) for q in range(PEER_TOPK)
                    if (p + 1) * (q + 1) <= PEER_TOPK)


def _sorting_network(n):
    def merge(lo, hi, r):
        step = r * 2
        if step < hi - lo:
            yield from merge(lo, hi, step)
            yield from merge(lo + r, hi, step)
            yield from ((i, i + r) for i in range(lo + r, hi - r, step))
        else:
            yield (lo, lo + r)

    def sort(lo, hi):
        if hi - lo >= 1:
            mid = lo + (hi - lo) // 2
            yield from sort(lo, mid)
            yield from sort(mid + 1, hi)
            yield from merge(lo, hi, 1)

    return tuple(sort(0, n - 1))


_SORT16 = _sorting_network(PEER_TOPK)
_BITONIC16 = tuple((i, i + d) for d in (8, 4, 2, 1) for i in range(PEER_TOPK) if not i & d)


def _dot(a, b):
    return jnp.dot(a, b, preferred_element_type=F32)


def _dot_nt(a, b):
    return lax.dot_general(a, b, (((1,), (1,)), ((), ())), preferred_element_type=F32)


def _dot_tn(a, b):
    return lax.dot_general(a, b, (((0,), (0,)), ((), ())), preferred_element_type=F32)


def _split_bf16(x):
    hi = x.astype(BF16)
    lo = (x - hi.astype(F32)).astype(BF16)
    return hi, lo


IN_TM = 256
IN_TN = 1024


def _in_proj_kernel(x_ref, g_ref, w_ref, qkg_ref, red_ref, exp_ref, z_ref, qkv_ref):
    x = x_ref[...]
    ms = jnp.mean(x * x, axis=-1, keepdims=True)
    xn = (x * lax.rsqrt(ms + EPS) * g_ref[...]).astype(BF16)
    for j in range(D_IN // IN_TN):
        cs = slice(j * IN_TN, (j + 1) * IN_TN)
        z = _dot(xn, w_ref[:, cs])
        if j == 0:
            hi, lo = _split_bf16(z * z)
            hi, lo = _split_bf16(_dot(hi, red_ref[...]) + _dot(lo, red_ref[...]))
            ss = _dot(hi, exp_ref[...]) + _dot(lo, exp_ref[...])
            z = z * lax.rsqrt(ss * (1.0 / NA_HEAD_DIM) + EPS) * qkg_ref[...]
        z_ref[:, cs] = z
        lo_c, hi_c = j * IN_TN, min((j + 1) * IN_TN, 3 * NA_WIDTH)
        if hi_c > lo_c:
            qkv_ref[:, lo_c:hi_c] = z[:, :hi_c - lo_c].astype(BF16)


def _in_proj(x2d, gain, w_bf16, qk_gain, head_red, head_exp):
    n = x2d.shape[0]
    const = lambda shape: pl.BlockSpec(shape, lambda i: (0, 0), pipeline_mode=pl.Buffered(1))
    return pl.pallas_call(
        _in_proj_kernel,
        grid=(n // IN_TM,),
        in_specs=[
            pl.BlockSpec((IN_TM, D_MODEL), lambda i: (i, 0)),
            const((1, D_MODEL)),
            const((D_MODEL, D_IN)),
            const((1, IN_TN)),
            const(head_red.shape),
            const(head_exp.shape),
        ],
        out_specs=[pl.BlockSpec((IN_TM, D_IN), lambda i: (i, 0)),
                   pl.BlockSpec((IN_TM, 3 * NA_WIDTH), lambda i: (i, 0))],
        out_shape=[jax.ShapeDtypeStruct((n, D_IN), F32), jax.ShapeDtypeStruct((n, 3 * NA_WIDTH), BF16)],
        compiler_params=pltpu.CompilerParams(
            dimension_semantics=("parallel",), vmem_limit_bytes=V7X_VMEM_LIMIT),
        name="in_proj",
    )(x2d, gain, w_bf16, qk_gain, head_red, head_exp)


HG_TT = 512


def _hgrn_chunk(q, f_logit, v, lb, states, tri, pair_mask, level_masks, ones, *, reverse):
    C, SB = HG_CHUNK, HG_SUB
    nblk = C // SB
    f = lb + (1.0 - lb) * jax.nn.sigmoid(f_logit)
    qs = q * jax.nn.sigmoid(q)
    hi, lo = _split_bf16(jnp.log(f))
    b = _dot(tri, hi) + _dot(tri, lo)
    c = b - jnp.log(1.0 - f)
    vb = v.astype(BF16)
    q_all = (qs * jnp.exp(b)).astype(BF16)
    b_end = b[0:1] if reverse else b[C - 1:C]
    k_d = jnp.exp(b_end - c).astype(BF16)
    carry_decay = jnp.exp(b_end)

    def boundary_refs(size):
        q_ref, k_ref = [], []
        for g0 in range(0, C, 2 * size):
            first, second = slice(g0, g0 + size), slice(g0 + size, g0 + 2 * size)
            if reverse:
                bound = jnp.broadcast_to(b[g0 + size:g0 + size + 1], (size, b.shape[1]))
                q_ref += [bound, b[second]]
                k_ref += [c[first], bound]
            else:
                bound = jnp.broadcast_to(b[g0 + size - 1:g0 + size], (size, b.shape[1]))
                q_ref += [b[first], bound]
                k_ref += [bound, c[second]]
        return jnp.concatenate(q_ref, axis=0), jnp.concatenate(k_ref, axis=0)

    level_q, level_k = [], []
    for size in HG_LEVELS:
        q_bound, k_bound = boundary_refs(size)
        level_q.append((qs * jnp.exp(b - q_bound)).astype(BF16))
        level_k.append(jnp.exp(k_bound - c).astype(BF16))

    outs, new_states = [], []
    for h, state in enumerate(states):
        cs = slice(h * HG_DK, (h + 1) * HG_DK)
        prods = []
        for blk in range(nblk):
            rows = slice(blk * SB, (blk + 1) * SB)
            prod = (qs[rows, cs][None, :, :] * jnp.exp(b[rows, cs][None, :, :] - c[rows, cs][:, None, :]))
            prods.append(jnp.where(pair_mask, prod, 0.0).reshape(SB * SB, HG_DK))
        rs = _dot(jnp.concatenate(prods, axis=0).astype(BF16), ones)
        rs = rs.reshape(nblk, SB, SB, HG_DV)
        o_diag = [jnp.sum(rs[blk] * v[blk * SB:(blk + 1) * SB, cs][:, None, :], axis=0) for blk in range(nblk)]
        s_off = sum(jnp.where(m, _dot_nt(lq[:, cs], lk[:, cs]), 0.0)
                    for m, lq, lk in zip(level_masks, level_q, level_k))
        o_off = _dot(s_off.astype(BF16), vb[:, cs])
        o_inter = _dot_nt(q_all[:, cs], state.astype(BF16))
        outs.append(o_inter + o_off + jnp.concatenate(o_diag, axis=0))
        new_states.append(state * carry_decay[:, cs] + _dot_tn(vb[:, cs], k_d[:, cs]))
    return outs, new_states


def _hgrn_kernel(*refs, reverse, finish):
    if finish:
        q_ref, f_ref, v_ref, lb_ref, of_ref, g_ref, gain_ref, o_ref, st_ref = refs
    else:
        q_ref, f_ref, v_ref, lb_ref, o_ref, st_ref = refs
    C, SB = HG_CHUNK, HG_SUB
    nchunk = HG_TT // C

    @pl.when(pl.program_id(1) == 0)
    def _():
        st_ref[...] = jnp.zeros_like(st_ref)

    row = lax.broadcasted_iota(jnp.int32, (C, C), 0)
    col = lax.broadcasted_iota(jnp.int32, (C, C), 1)
    tri = ((col >= row) if reverse else (col <= row)).astype(BF16)
    s_i = lax.broadcasted_iota(jnp.int32, (SB, SB, HG_DK), 0)
    t_i = lax.broadcasted_iota(jnp.int32, (SB, SB, HG_DK), 1)
    pair_mask = (s_i >= t_i) if reverse else (s_i <= t_i)
    ones = jnp.ones((HG_DK, HG_DV), BF16)
    level_masks = []
    for size in HG_LEVELS:
        tg, sg = row // size, col // size
        later, earlier = (0, 1) if reverse else (1, 0)
        level_masks.append((tg // 2 == sg // 2) & (tg % 2 == later) & (sg % 2 == earlier))

    def chunk(ci, carry):
        c = (nchunk - 1 - ci) if reverse else ci
        sl = pl.ds(pl.multiple_of(c * C, C), C)
        outs, states = _hgrn_chunk(q_ref[sl, :], f_ref[sl, :], v_ref[sl, :], lb_ref[0],
                                   [st_ref[h] for h in range(HG_HEADS)], tri, pair_mask, level_masks, ones,
                                   reverse=reverse)
        for h in range(HG_HEADS):
            cs = slice(h * HG_DK, (h + 1) * HG_DK)
            o, st_ref[h] = outs[h], states[h]
            if finish:
                tot = of_ref[sl, cs] + o
                ms = jnp.mean(tot * tot, axis=-1, keepdims=True)
                g = g_ref[sl, cs]
                y = tot * lax.rsqrt(ms + EPS) * gain_ref[...] * (g * jax.nn.sigmoid(g))
                o_ref[sl, cs] = y.astype(o_ref.dtype)
            else:
                o_ref[sl, cs] = o
        return carry

    lax.fori_loop(0, nchunk, chunk, 0, unroll=2)


def _hgrn_pass(z, lb, batch, seq, *, reverse, o_fwd=None, gain=None):
    n = z.shape[0]
    nt = seq // HG_TT
    finish = o_fwd is not None
    d = 1 if reverse else 0

    def tmap(t):
        return (nt - 1 - t) if reverse else t

    def zspec(col0, per_dir=True):
        blk = col0 // HG_HEADS + (d if per_dir else 0)
        return pl.BlockSpec((HG_TT, HG_WIDTH), lambda b, t: (b * nt + tmap(t), blk))

    ospec = pl.BlockSpec((HG_TT, HG_WIDTH), lambda b, t: (b * nt + tmap(t), 0))
    in_specs = [zspec(_COL_HG_Q), zspec(_COL_HG_F), zspec(_COL_HG_I),
                pl.BlockSpec((1, 1, HG_WIDTH), lambda b, t: (d, 0, 0))]
    args = [z, z, z, lb]
    if finish:
        in_specs += [ospec, zspec(_COL_HG_G, per_dir=False), pl.BlockSpec((1, HG_DV), lambda b, t: (0, 0))]
        args += [o_fwd, z, gain]
    return pl.pallas_call(
        functools.partial(_hgrn_kernel, reverse=reverse, finish=finish),
        grid=(batch, nt),
        in_specs=in_specs,
        out_specs=ospec,
        out_shape=jax.ShapeDtypeStruct((n, HG_WIDTH), BF16 if finish else F32),
        scratch_shapes=[pltpu.VMEM((HG_HEADS, HG_DV, HG_DK), F32)],
        compiler_params=pltpu.CompilerParams(
            dimension_semantics=("parallel", "arbitrary"),
            vmem_limit_bytes=V7X_VMEM_LIMIT),
        name="hgrn_bwd" if reverse else "hgrn_fwd",
    )(*args)


NA_BAND = 8
NA_BT = NA_BAND * GRID_W


def _na_bias_table(rpb):
    c = jnp.arange(GRID_W)
    c0 = jnp.clip(c - NA_WIN_C // 2, 0, GRID_W - NA_WIN_C)
    kc = jnp.arange(GRID_W)
    valid = (kc[None, :] >= c0[:, None]) & (kc[None, :] < c0[:, None] + NA_WIN_C)
    off = jnp.clip(kc[None, :] - c[:, None] + NA_WIN_C - 1, 0, 2 * NA_WIN_C - 2)
    t = jnp.where(valid[None, None], rpb.astype(F32)[:, :, off], NEG_BIG)
    t = t.reshape(NA_HEADS // 2, 2, 2 * NA_WIN_R - 1, GRID_W, GRID_W)
    t = t.transpose(0, 2, 4, 1, 3)
    return t.reshape(NA_HEADS // 2, (2 * NA_WIN_R - 1) * GRID_W, 2 * GRID_W)


def _na_kernel(q_ref, k0_ref, k1_ref, k2_ref, v0_ref, v1_ref, v2_ref, bias_ref, o_ref,
               kcat_ref, vcat_ref, *, rows):
    g = pl.program_id(1)
    for i, (kr, vr) in enumerate(((k0_ref, v0_ref), (k1_ref, v1_ref), (k2_ref, v2_ref))):
        kcat_ref[i * NA_BT:(i + 1) * NA_BT, :] = kr[...]
        vcat_ref[i * NA_BT:(i + 1) * NA_BT, :] = vr[...]
    lane = lax.broadcasted_iota(jnp.int32, (1, 2 * NA_HEAD_DIM), 1)
    left = lane < NA_HEAD_DIM
    nkeys = NA_WIN_R * GRID_W
    scale = NA_HEAD_DIM ** -0.5
    pairs = range(NA_HEADS // 2)

    def body(j, carry):
        r = g * NA_BAND + j
        r0 = jnp.clip(r - NA_WIN_R // 2, 0, rows - NA_WIN_R)
        koff = pl.multiple_of((r0 - g * NA_BAND + NA_BAND) * GRID_W, GRID_W)
        boff = pl.multiple_of((r0 - r + NA_WIN_R - 1) * GRID_W, GRID_W)
        qsl = pl.ds(pl.multiple_of(j * GRID_W, GRID_W), GRID_W)
        qb = q_ref[qsl, :] * scale
        cols = [slice(hp * 128, (hp + 1) * 128) for hp in pairs]
        s_t = []
        for cs in cols:
            q2 = qb[:, cs]
            zero = jnp.zeros_like(q2)
            rhs_t = jnp.concatenate([jnp.where(left, q2, zero), jnp.where(left, zero, q2)], axis=0)
            s_t.append(_dot_nt(kcat_ref[pl.ds(koff, nkeys), cs], rhs_t))
        s_t = [s + bias_ref[hp, pl.ds(boff, nkeys), :] for hp, s in zip(pairs, s_t)]
        p = [jnp.exp(s - jnp.max(s, axis=0, keepdims=True)) for s in s_t]
        p = [(x * (1.0 / jnp.sum(x, axis=0, keepdims=True))).astype(BF16) for x in p]
        o2 = [_dot_tn(x, vcat_ref[pl.ds(koff, nkeys), cs]) for x, cs in zip(p, cols)]
        for cs, o in zip(cols, o2):
            o_ref[qsl, cs] = jnp.where(left, o[:GRID_W], o[GRID_W:]).astype(o_ref.dtype)
        return carry

    lax.fori_loop(0, NA_BAND, body, 0, unroll=4)


def _na(z, bias_tbl, batch, seq):
    n = z.shape[0]
    rows = seq // GRID_W
    nb = rows // NA_BAND

    def kv(colblk, shift):
        return pl.BlockSpec(
            (NA_BT, NA_WIDTH),
            lambda b, g: (b * nb + jnp.clip(g + shift, 0, nb - 1), colblk))

    return pl.pallas_call(
        functools.partial(_na_kernel, rows=rows),
        grid=(batch, nb),
        in_specs=[pl.BlockSpec((NA_BT, NA_WIDTH), lambda b, g: (b * nb + g, 0)),
                  kv(1, -1), kv(1, 0), kv(1, 1), kv(2, -1), kv(2, 0), kv(2, 1),
                  pl.BlockSpec(bias_tbl.shape, lambda b, g: (0, 0, 0))],
        out_specs=pl.BlockSpec((NA_BT, NA_WIDTH), lambda b, g: (b * nb + g, 0)),
        out_shape=jax.ShapeDtypeStruct((n, NA_WIDTH), BF16),
        scratch_shapes=[pltpu.VMEM((3 * NA_BT, NA_WIDTH), BF16),
                        pltpu.VMEM((3 * NA_BT, NA_WIDTH), BF16)],
        compiler_params=pltpu.CompilerParams(
            dimension_semantics=("parallel", "arbitrary"), vmem_limit_bytes=V7X_VMEM_LIMIT),
        name="natten",
    )(z, z, z, z, z, z, z, bias_tbl)


MG_TM = 512


def _merge_kernel(x_ref, oa_ref, ob_ref, ga_ref, gb_ref, wa_ref, wb_ref, wo_ref, nf_ref,
                  x2_ref, xt_ref):
    a = _dot(oa_ref[...], wa_ref[...])
    b = _dot(ob_ref[...], wb_ref[...])
    mix = jax.nn.sigmoid(ga_ref[...]) * a + jax.nn.sigmoid(gb_ref[...]) * b
    x2 = x_ref[...] + _dot(mix.astype(BF16), wo_ref[...])
    x2_ref[...] = x2
    ms = jnp.mean(x2 * x2, axis=-1, keepdims=True)
    xt_ref[...] = (x2 * lax.rsqrt(ms + EPS) * nf_ref[...]).T.astype(BF16)


def _merge(x2d, o_a, o_b, z, wa, wb, wo, norm_ffn):
    n = x2d.shape[0]
    col_ga = (_COL_HG_G * 128 + HG_WIDTH) // D_MODEL
    full = lambda shape: pl.BlockSpec(shape, lambda i: (0, 0))
    tok = lambda w, c=0: pl.BlockSpec((MG_TM, w), lambda i: (i, c))
    return pl.pallas_call(
        _merge_kernel,
        grid=(n // MG_TM,),
        in_specs=[tok(D_MODEL), tok(HG_WIDTH), tok(NA_WIDTH), tok(D_MODEL, col_ga), tok(D_MODEL, col_ga + 1),
                  full((HG_WIDTH, D_MODEL)), full((NA_WIDTH, D_MODEL)), full((D_MODEL, D_MODEL)),
                  full((1, D_MODEL))],
        out_specs=[tok(D_MODEL), pl.BlockSpec((D_MODEL, MG_TM), lambda i: (0, i))],
        out_shape=[jax.ShapeDtypeStruct((n, D_MODEL), F32), jax.ShapeDtypeStruct((D_MODEL, n), BF16)],
        compiler_params=pltpu.CompilerParams(
            dimension_semantics=("parallel",), vmem_limit_bytes=V7X_VMEM_LIMIT),
        name="merge",
    )(x2d, o_a, o_b, z, z, wa, wb, wo, norm_ffn)


PEER_TB = 512
PEER_EB = 1024
PEER_LC = 256
PEER_KC = 256
PEER_UK = 256


def _peer_route(xt_ref, wq_ref, keys_ref, qt_ref, s_ref, toph_ref, top_ref, cand_ref, theta_ref,
                cnt_ref, coef_ref, r1_ref, e1_ref):
    K = PEER_TOPK
    half_rows = wq_ref.shape[0] // 2
    for part in range(2):
        rs = slice(part * half_rows, (part + 1) * half_rows)
        qt_ref[rs, :] = _dot(wq_ref[rs, :], xt_ref[...]).astype(BF16)

    for hp in range(2 * PEER_HEADS):
        s_ref[hp] = _dot(keys_ref[hp], qt_ref[hp * PEER_D_HALF:(hp + 1) * PEER_D_HALF, :])

    n_grp = PEER_N_KEYS // 8
    assert n_grp == K

    def exchange(rows, i, j):
        rows[i], rows[j] = jnp.maximum(rows[i], rows[j]), jnp.minimum(rows[i], rows[j])

    def extract(h, carry, *, half):
        for lc in range(xt_ref.shape[1] // 128):
            ls = slice(lc * 128, (lc + 1) * 128)
            keys = [s_ref[2 * h + half, 8 * r:8 * r + 8, ls] for r in range(n_grp)]
            rows = list(keys)
            for i, j in _SORT16:
                exchange(rows, i, j)
            for shift in (4, 2, 1):
                rows = [jnp.maximum(rows[i], pltpu.roll(rows[K - 1 - i], shift, axis=0)) for i in range(K)]
                for i, j in _BITONIC16:
                    exchange(rows, i, j)
            toph_ref[half, h, :, ls] = jnp.concatenate([rows[q][0:1] for q in range(K)], axis=0)
            if half:
                for r in range(0, n_grp, 2):
                    rank = []
                    for k in keys[r:r + 2]:
                        rk = jnp.zeros_like(k)
                        for q in range(K):
                            rk = jnp.where(k < rows[q], float(q + 1), rk)
                        rank.append(rk)
                    r1_ref[h, 8 * r:8 * r + 16, ls] = jnp.concatenate(rank, axis=0).astype(BF16)
        return carry

    lax.fori_loop(0, PEER_HEADS, functools.partial(extract, half=0), 0)
    lax.fori_loop(0, PEER_HEADS, functools.partial(extract, half=1), 0)
    for half in range(2):
        for q in range(K):
            top_ref[half, q] = jnp.concatenate(
                [toph_ref[half, h, q:q + 1, :] for h in range(PEER_HEADS)], axis=0)

    ncand = len(_CAND_PAIRS)
    for c, (p, q) in enumerate(_CAND_PAIRS):
        cand_ref[c] = top_ref[0, p] + top_ref[1, q]
    small = cand_ref.shape[1:]

    def tau_step(it, carry):
        tau, cnt = carry
        m = cand_ref[0]
        for c in range(1, ncand):
            m = jnp.maximum(m, cand_ref[c])
        n_eq = jnp.zeros(small, F32)
        for c in range(ncand):
            v = cand_ref[c]
            hit = v == m
            n_eq = n_eq + jnp.where(hit, 1.0, 0.0)
            cand_ref[c] = jnp.where(hit, -jnp.inf, v)
        return jnp.where(cnt < float(K), m, tau), cnt + n_eq

    tau, _ = lax.fori_loop(0, K, tau_step, (jnp.full(small, -jnp.inf, F32), jnp.zeros(small, F32)))
    best = top_ref[0, 0] + top_ref[1, 0]
    zsum = jnp.zeros(small, F32)
    for p, q in _CAND_PAIRS:
        v = top_ref[0, p] + top_ref[1, q]
        zsum = zsum + jnp.where(v >= tau, jnp.exp(v - best), 0.0)
    inv_z = 1.0 / zsum

    for q in range(K):
        th = jnp.full(small, jnp.inf, F32)
        b_q = top_ref[1, q]
        for p in range(K):
            a_p = top_ref[0, p]
            th = jnp.minimum(th, jnp.where(a_p + b_q >= tau, a_p, jnp.inf))
        theta_ref[q] = th

    for h in range(PEER_HEADS):
        s0 = s_ref[2 * h]
        cnt = jnp.zeros(s0.shape, F32)
        for q in range(K):
            cnt = jnp.where(s0 >= theta_ref[q, h:h + 1, :], float(q + 1), cnt)
        cnt_ref[h] = cnt
        coef_ref[h] = jnp.exp(s0 - top_ref[0, 0, h:h + 1, :])
        e1_ref[h] = (jnp.exp(s_ref[2 * h + 1] - top_ref[1, 0, h:h + 1, :]) * (0.5 * inv_z[h:h + 1])).astype(BF16)


def _peer_kernel(xt_ref, x2_ref, wq_ref, keys_ref, u_ref, vt_ref, o_ref,
                 qt_ref, s_ref, toph_ref, top_ref, cand_ref, theta_ref,
                 cnt_ref, coef_ref, r1_ref, e1_ref, g_ref, acc_ref):
    e = pl.program_id(1)
    ne = pl.num_programs(1)

    @pl.when(e == 0)
    def _():
        _peer_route(xt_ref, wq_ref, keys_ref, qt_ref, s_ref, toph_ref, top_ref, cand_ref, theta_ref,
                    cnt_ref, coef_ref, r1_ref, e1_ref)
        acc_ref[...] = jnp.zeros_like(acc_ref)

    n_i = PEER_EB // PEER_N_KEYS
    i_rows = pl.ds(pl.multiple_of(e * n_i, n_i), n_i)
    zero = jnp.zeros((PEER_N_KEYS, PEER_LC), BF16)
    n_lc = PEER_TB // PEER_LC
    cnt_rows = [[cnt_ref[h, i_rows, lc * PEER_LC:(lc + 1) * PEER_LC].astype(BF16) for h in range(PEER_HEADS)]
                for lc in range(n_lc)]
    coef_rows = [[coef_ref[h, i_rows, lc * PEER_LC:(lc + 1) * PEER_LC].astype(BF16) for h in range(PEER_HEADS)]
                 for lc in range(n_lc)]
    anchors = []
    for ii in range(n_i):
        for lc in range(n_lc):
            ls = slice(lc * PEER_LC, (lc + 1) * PEER_LC)
            gate = zero
            for h in range(PEER_HEADS):
                sel = r1_ref[h, :, ls] < cnt_rows[lc][h][ii:ii + 1]
                gate = gate + jnp.where(sel, e1_ref[h, :, ls], zero) * coef_rows[lc][h][ii:ii + 1]
            g_ref[ii * PEER_N_KEYS:(ii + 1) * PEER_N_KEYS, ls] = gate
            bits = pltpu.bitcast(gate, jnp.uint32)
            bits = functools.reduce(jnp.bitwise_or, [bits[r:r + 8, c:c + 128] for r in range(0, bits.shape[0], 8)
                                                     for c in range(0, PEER_LC, 128)])
            anchors.append(pltpu.bitcast((bits >> 16) >> 16, BF16))
    anchors = anchors[:len(anchors) - n_lc * PEER_KC // PEER_N_KEYS]

    n_k = D_MODEL // PEER_UK
    free_slots = (n_k - 1) * n_i
    anchor_of = {n_i + t * free_slots // len(anchors): a for t, a in enumerate(anchors)}
    cols = []
    for k in range(n_k):
        pieces = []
        for rg in range(n_i):
            piece = u_ref[rg * PEER_N_KEYS:(rg + 1) * PEER_N_KEYS, k * PEER_UK:(k + 1) * PEER_UK]
            if k * n_i + rg in anchor_of:
                top = jnp.concatenate([piece[:16, :128] + anchor_of[k * n_i + rg], piece[:16, 128:]], axis=1)
                piece = jnp.concatenate([top, piece[16:]], axis=0)
            pieces.append(piece)
        cols.append(jnp.concatenate(pieces, axis=0))
    h_t = _dot(jnp.concatenate(cols, axis=1), xt_ref[...])
    y_t = None
    for c in range(PEER_EB // PEER_KC):
        rs = slice(c * PEER_KC, (c + 1) * PEER_KC)
        hh = h_t[rs]
        act = hh * (1.0 + lax.erf(hh * (2.0 ** -0.5)))
        part = _dot(vt_ref[:, rs], act.astype(BF16) * g_ref[rs, :])
        y_t = part if y_t is None else y_t + part
    acc_ref[...] += y_t

    @pl.when(e == ne - 1)
    def _():
        o_ref[...] = x2_ref[...] + acc_ref[...].T


def _peer(xt, x2, wq_t, keys, u_bf16, vt_bf16):
    n = x2.shape[0]
    tb, eb = PEER_TB, PEER_EB
    ne = PEER_N_EXPERTS // eb
    return pl.pallas_call(
        _peer_kernel,
        grid=(n // tb, PEER_N_EXPERTS // eb),
        in_specs=[
            pl.BlockSpec((D_MODEL, tb), lambda i, e: (0, i)),
            pl.BlockSpec((tb, D_MODEL), lambda i, e: (i, 0)),
            pl.BlockSpec(wq_t.shape, lambda i, e: (0, 0)),
            pl.BlockSpec(keys.shape, lambda i, e: (0, 0, 0)),
            pl.BlockSpec((eb, D_MODEL), lambda i, e: (e, 0)),
            pl.BlockSpec((D_MODEL, eb), lambda i, e: (0, e)),
        ],
        out_specs=pl.BlockSpec((tb, D_MODEL), lambda i, e: (i, 0)),
        out_shape=jax.ShapeDtypeStruct((n, D_MODEL), F32),
        scratch_shapes=[
            pltpu.VMEM((wq_t.shape[0], tb), BF16),
            pltpu.VMEM((2 * PEER_HEADS, PEER_N_KEYS, tb), F32),
            pltpu.VMEM((2, PEER_HEADS, PEER_TOPK, tb), F32),
            pltpu.VMEM((2, PEER_TOPK, PEER_HEADS, tb), F32),
            pltpu.VMEM((len(_CAND_PAIRS), PEER_HEADS, tb), F32),
            pltpu.VMEM((PEER_TOPK, PEER_HEADS, tb), F32),
            pltpu.VMEM((PEER_HEADS, PEER_N_KEYS, tb), F32),
            pltpu.VMEM((PEER_HEADS, PEER_N_KEYS, tb), F32),
            pltpu.VMEM((PEER_HEADS, PEER_N_KEYS, tb), BF16),
            pltpu.VMEM((PEER_HEADS, PEER_N_KEYS, tb), BF16),
            pltpu.VMEM((eb, tb), BF16),
            pltpu.VMEM((D_MODEL, tb), F32),
        ],
        compiler_params=pltpu.CompilerParams(
            dimension_semantics=("parallel", "arbitrary"), vmem_limit_bytes=V7X_VMEM_LIMIT),
        name="peer",
    )(xt, x2, wq_t, keys, u_bf16, vt_bf16)


def _trunk(x, p):
    batch, seq, _ = x.shape
    x2d = x.reshape(batch * seq, D_MODEL)
    z, qkv = _in_proj(x2d, p["norm_mix"], p["w_in"], p["qk_gain"], p["head_red"], p["head_exp"])
    o_f = _hgrn_pass(z, p["lb"], batch, seq, reverse=False)
    o_a = _hgrn_pass(z, p["lb"], batch, seq, reverse=True, o_fwd=o_f, gain=p["hg_out_norm"])
    o_b = _na(qkv, p["na_bias"], batch, seq)
    x2, xt = _merge(x2d, o_a, o_b, z, p["w_proj_a"], p["w_proj_b"], p["w_out"], p["norm_ffn"])
    y = _peer(xt, x2, p["wq_t"], p["keys"], p["expert_u"], p["expert_vt"])
    return y.reshape(batch, seq, D_MODEL)


def kernel(x_prompt, x_sample, norm_mix, w_in, lb_logits, hg_out_norm, q_norm, k_norm, rel_pos_bias,
           w_proj_a, w_proj_b, w_out, norm_ffn, w_query, sub_keys, expert_u, expert_v):
    l = 0
    lb_all = jnp.cumsum(jax.nn.softmax(lb_logits.astype(F32), axis=0), axis=0)
    head_id = np.arange(2 * NA_WIDTH) // NA_HEAD_DIM
    p = {
        "norm_mix": norm_mix[l].reshape(1, D_MODEL),
        "w_in": w_in[l].astype(BF16),
        "qk_gain": jnp.concatenate([jnp.tile(q_norm[l], NA_HEADS), jnp.tile(k_norm[l], NA_HEADS)]).reshape(1, -1),
        "head_red": jnp.asarray(head_id[:, None] == np.arange(128)[None, :], BF16),
        "head_exp": jnp.asarray(np.arange(128)[:, None] == head_id[None, :], BF16),
        "lb": lb_all[l].reshape(2, 1, HG_WIDTH),
        "hg_out_norm": hg_out_norm[l].reshape(1, HG_DV),
        "na_bias": _na_bias_table(rel_pos_bias[l]),
        "w_proj_a": w_proj_a[l].astype(BF16),
        "w_proj_b": w_proj_b[l].astype(BF16),
        "w_out": w_out[l].astype(BF16),
        "norm_ffn": norm_ffn[l].reshape(1, D_MODEL),
        "wq_t": w_query[l].T.astype(BF16),
        "keys": sub_keys[l].reshape(2 * PEER_HEADS, PEER_N_KEYS, PEER_D_HALF).astype(BF16),
        "expert_u": expert_u[l].astype(BF16),
        "expert_vt": expert_v[l].T.astype(BF16),
    }
    return (_trunk(x_prompt, p), _trunk(x_sample, p))
```

```python
import functools

import numpy as np
import jax
import jax.numpy as jnp
from jax import lax
from jax.experimental import pallas as pl
from jax.experimental.pallas import tpu as pltpu

F32 = jnp.float32
BF16 = jnp.bfloat16

D_MODEL = 1024
GRID_W = 64
EPS = 1e-6
HG_HEADS = 4
HG_DK = 128
HG_DV = 128
HG_WIDTH = HG_HEADS * HG_DV
HG_CHUNK = 64
HG_SUB = 16
HG_LEVELS = (32, 16)
assert HG_LEVELS[0] * 2 == HG_CHUNK and HG_LEVELS[-1] == HG_SUB
NA_HEADS = 8
NA_HEAD_DIM = 64
NA_WIDTH = NA_HEADS * NA_HEAD_DIM
NA_WIN_R = 8
NA_WIN_C = 16
PEER_HEADS = 8
PEER_N_KEYS = 128
PEER_N_EXPERTS = PEER_N_KEYS * PEER_N_KEYS
PEER_D_HALF = 128
PEER_TOPK = 16
D_IN = 3 * NA_WIDTH + 3 * 2 * HG_HEADS * HG_DK + HG_WIDTH + 2 * D_MODEL

_COL_HG_Q = (3 * NA_WIDTH) // 128
_COL_HG_F = _COL_HG_Q + 8
_COL_HG_I = _COL_HG_F + 8
_COL_HG_G = _COL_HG_I + 8

V7X_VMEM_LIMIT = 56 * 1024 * 1024
NEG_BIG = -1e30

_CAND_PAIRS = tuple((p, q) for p in range(PEER_TOPK) for q in range(PEER_TOPK)
                    if (p + 1) * (q + 1) <= PEER_TOPK)


def _sorting_network(n):
    def merge(lo, hi, r):
        step = r * 2
        if step < hi - lo:
            yield from merge(lo, hi, step)
            yield from merge(lo + r, hi, step)
            yield from ((i, i + r) for i in range(lo + r, hi - r, step))
        else:
            yield (lo, lo + r)

    def sort(lo, hi):
        if hi - lo >= 1:
            mid = lo + (hi - lo) // 2
            yield from sort(lo, mid)
            yield from sort(mid + 1, hi)
            yield from merge(lo, hi, 1)

    return tuple(sort(0, n - 1))


_SORT16 = _sorting_network(PEER_TOPK)
_BITONIC16 = tuple((i, i + d) for d in (8, 4, 2, 1) for i in range(PEER_TOPK) if not i & d)


def _dot(a, b):
    return jnp.dot(a, b, preferred_element_type=F32)


def _dot_nt(a, b):
    return lax.dot_general(a, b, (((1,), (1,)), ((), ())), preferred_element_type=F32)


def _dot_tn(a, b):
    return lax.dot_general(a, b, (((0,), (0,)), ((), ())), preferred_element_type=F32)


def _split_bf16(x):
    hi = x.astype(BF16)
    lo = (x - hi.astype(F32)).astype(BF16)
    return hi, lo


IN_TM = 256
IN_TN = 1024


def _in_proj_kernel(x_ref, g_ref, w_ref, qkg_ref, red_ref, exp_ref, z_ref, qkv_ref):
    x = x_ref[...]
    ms = jnp.mean(x * x, axis=-1, keepdims=True)
    xn = (x * lax.rsqrt(ms + EPS) * g_ref[...]).astype(BF16)
    for j in range(D_IN // IN_TN):
        cs = slice(j * IN_TN, (j + 1) * IN_TN)
        z = _dot(xn, w_ref[:, cs])
        if j == 0:
            hi, lo = _split_bf16(z * z)
            hi, lo = _split_bf16(_dot(hi, red_ref[...]) + _dot(lo, red_ref[...]))
            ss = _dot(hi, exp_ref[...]) + _dot(lo, exp_ref[...])
            z = z * lax.rsqrt(ss * (1.0 / NA_HEAD_DIM) + EPS) * qkg_ref[...]
        z_ref[:, cs] = z
        lo_c, hi_c = j * IN_TN, min((j + 1) * IN_TN, 3 * NA_WIDTH)
        if hi_c > lo_c:
            qkv_ref[:, lo_c:hi_c] = z[:, :hi_c - lo_c].astype(BF16)


def _in_proj(x2d, gain, w_bf16, qk_gain, head_red, head_exp):
    n = x2d.shape[0]
    const = lambda shape: pl.BlockSpec(shape, lambda i: (0, 0), pipeline_mode=pl.Buffered(1))
    return pl.pallas_call(
        _in_proj_kernel,
        grid=(n // IN_TM,),
        in_specs=[
            pl.BlockSpec((IN_TM, D_MODEL), lambda i: (i, 0)),
            const((1, D_MODEL)),
            const((D_MODEL, D_IN)),
            const((1, IN_TN)),
            const(head_red.shape),
            const(head_exp.shape),
        ],
        out_specs=[pl.BlockSpec((IN_TM, D_IN), lambda i: (i, 0)),
                   pl.BlockSpec((IN_TM, 3 * NA_WIDTH), lambda i: (i, 0))],
        out_shape=[jax.ShapeDtypeStruct((n, D_IN), F32), jax.ShapeDtypeStruct((n, 3 * NA_WIDTH), BF16)],
        compiler_params=pltpu.CompilerParams(
            dimension_semantics=("parallel",), vmem_limit_bytes=V7X_VMEM_LIMIT),
        name="in_proj",
    )(x2d, gain, w_bf16, qk_gain, head_red, head_exp)


HG_TT = 512


def _hgrn_chunk(q, f_logit, v, lb, states, tri, pair_mask, level_masks, ones, *, reverse):
    C, SB = HG_CHUNK, HG_SUB
    nblk = C // SB
    f = lb + (1.0 - lb) * jax.nn.sigmoid(f_logit)
    qs = q * jax.nn.sigmoid(q)
    hi, lo = _split_bf16(jnp.log(f))
    b = _dot(tri, hi) + _dot(tri, lo)
    c = b - jnp.log(1.0 - f)
    vb = v.astype(BF16)
    q_all = (qs * jnp.exp(b)).astype(BF16)
    b_end = b[0:1] if reverse else b[C - 1:C]
    k_d = jnp.exp(b_end - c).astype(BF16)
    carry_decay = jnp.exp(b_end)

    def boundary_refs(size):
        q_ref, k_ref = [], []
        for g0 in range(0, C, 2 * size):
            first, second = slice(g0, g0 + size), slice(g0 + size, g0 + 2 * size)
            if reverse:
                bound = jnp.broadcast_to(b[g0 + size:g0 + size + 1], (size, b.shape[1]))
                q_ref += [bound, b[second]]
                k_ref += [c[first], bound]
            else:
                bound = jnp.broadcast_to(b[g0 + size - 1:g0 + size], (size, b.shape[1]))
                q_ref += [b[first], bound]
                k_ref += [bound, c[second]]
        return jnp.concatenate(q_ref, axis=0), jnp.concatenate(k_ref, axis=0)

    level_q, level_k = [], []
    for size in HG_LEVELS:
        q_bound, k_bound = boundary_refs(size)
        level_q.append((qs * jnp.exp(b - q_bound)).astype(BF16))
        level_k.append(jnp.exp(k_bound - c).astype(BF16))

    outs, new_states = [], []
    for h, state in enumerate(states):
        cs = slice(h * HG_DK, (h + 1) * HG_DK)
        prods = []
        for blk in range(nblk):
            rows = slice(blk * SB, (blk + 1) * SB)
            prod = (qs[rows, cs][None, :, :] * jnp.exp(b[rows, cs][None, :, :] - c[rows, cs][:, None, :]))
            prods.append(jnp.where(pair_mask, prod, 0.0).reshape(SB * SB, HG_DK))
        rs = _dot(jnp.concatenate(prods, axis=0).astype(BF16), ones)
        rs = rs.reshape(nblk, SB, SB, HG_DV)
        o_diag = [jnp.sum(rs[blk] * v[blk * SB:(blk + 1) * SB, cs][:, None, :], axis=0) for blk in range(nblk)]
        s_off = sum(jnp.where(m, _dot_nt(lq[:, cs], lk[:, cs]), 0.0)
                    for m, lq, lk in zip(level_masks, level_q, level_k))
        o_off = _dot(s_off.astype(BF16), vb[:, cs])
        o_inter = _dot_nt(q_all[:, cs], state.astype(BF16))
        outs.append(o_inter + o_off + jnp.concatenate(o_diag, axis=0))
        new_states.append(state * carry_decay[:, cs] + _dot_tn(vb[:, cs], k_d[:, cs]))
    return outs, new_states


def _hgrn_kernel(*refs, reverse, finish):
    if finish:
        q_ref, f_ref, v_ref, lb_ref, of_ref, g_ref, gain_ref, o_ref, st_ref = refs
    else:
        q_ref, f_ref, v_ref, lb_ref, o_ref, st_ref = refs
    C, SB = HG_CHUNK, HG_SUB
    nchunk = HG_TT // C

    @pl.when(pl.program_id(1) == 0)
    def _():
        st_ref[...] = jnp.zeros_like(st_ref)

    row = lax.broadcasted_iota(jnp.int32, (C, C), 0)
    col = lax.broadcasted_iota(jnp.int32, (C, C), 1)
    tri = ((col >= row) if reverse else (col <= row)).astype(BF16)
    s_i = lax.broadcasted_iota(jnp.int32, (SB, SB, HG_DK), 0)
    t_i = lax.broadcasted_iota(jnp.int32, (SB, SB, HG_DK), 1)
    pair_mask = (s_i >= t_i) if reverse else (s_i <= t_i)
    ones = jnp.ones((HG_DK, HG_DV), BF16)
    level_masks = []
    for size in HG_LEVELS:
        tg, sg = row // size, col // size
        later, earlier = (0, 1) if reverse else (1, 0)
        level_masks.append((tg // 2 == sg // 2) & (tg % 2 == later) & (sg % 2 == earlier))

    def chunk(ci, carry):
        c = (nchunk - 1 - ci) if reverse else ci
        sl = pl.ds(pl.multiple_of(c * C, C), C)
        outs, states = _hgrn_chunk(q_ref[sl, :], f_ref[sl, :], v_ref[sl, :], lb_ref[0],
                                   [st_ref[h] for h in range(HG_HEADS)], tri, pair_mask, level_masks, ones,
                                   reverse=reverse)
        for h in range(HG_HEADS):
            cs = slice(h * HG_DK, (h + 1) * HG_DK)
            o, st_ref[h] = outs[h], states[h]
            if finish:
                tot = of_ref[sl, cs] + o
                ms = jnp.mean(tot * tot, axis=-1, keepdims=True)
                g = g_ref[sl, cs]
                y = tot * lax.rsqrt(ms + EPS) * gain_ref[...] * (g * jax.nn.sigmoid(g))
                o_ref[sl, cs] = y.astype(o_ref.dtype)
            else:
                o_ref[sl, cs] = o
        return carry

    lax.fori_loop(0, nchunk, chunk, 0, unroll=2)


def _hgrn_pass(z, lb, batch, seq, *, reverse, o_fwd=None, gain=None):
    n = z.shape[0]
    nt = seq // HG_TT
    finish = o_fwd is not None
    d = 1 if reverse else 0

    def tmap(t):
        return (nt - 1 - t) if reverse else t

    def zspec(col0, per_dir=True):
        blk = col0 // HG_HEADS + (d if per_dir else 0)
        return pl.BlockSpec((HG_TT, HG_WIDTH), lambda b, t: (b * nt + tmap(t), blk))

    ospec = pl.BlockSpec((HG_TT, HG_WIDTH), lambda b, t: (b * nt + tmap(t), 0))
    in_specs = [zspec(_COL_HG_Q), zspec(_COL_HG_F), zspec(_COL_HG_I),
                pl.BlockSpec((1, 1, HG_WIDTH), lambda b, t: (d, 0, 0))]
    args = [z, z, z, lb]
    if finish:
        in_specs += [ospec, zspec(_COL_HG_G, per_dir=False), pl.BlockSpec((1, HG_DV), lambda b, t: (0, 0))]
        args += [o_fwd, z, gain]
    return pl.pallas_call(
        functools.partial(_hgrn_kernel, reverse=reverse, finish=finish),
        grid=(batch, nt),
        in_specs=in_specs,
        out_specs=ospec,
        out_shape=jax.ShapeDtypeStruct((n, HG_WIDTH), BF16 if finish else F32),
        scratch_shapes=[pltpu.VMEM((HG_HEADS, HG_DV, HG_DK), F32)],
        compiler_params=pltpu.CompilerParams(
            dimension_semantics=("parallel", "arbitrary"),
            vmem_limit_bytes=V7X_VMEM_LIMIT),
        name="hgrn_bwd" if reverse else "hgrn_fwd",
    )(*args)


NA_BAND = 8
NA_BT = NA_BAND * GRID_W


def _na_bias_table(rpb):
    c = jnp.arange(GRID_W)
    c0 = jnp.clip(c - NA_WIN_C // 2, 0, GRID_W - NA_WIN_C)
    kc = jnp.arange(GRID_W)
    valid = (kc[None, :] >= c0[:, None]) & (kc[None, :] < c0[:, None] + NA_WIN_C)
    off = jnp.clip(kc[None, :] - c[:, None] + NA_WIN_C - 1, 0, 2 * NA_WIN_C - 2)
    t = jnp.where(valid[None, None], rpb.astype(F32)[:, :, off], NEG_BIG)
    t = t.reshape(NA_HEADS // 2, 2, 2 * NA_WIN_R - 1, GRID_W, GRID_W)
    t = t.transpose(0, 2, 4, 1, 3)
    return t.reshape(NA_HEADS // 2, (2 * NA_WIN_R - 1) * GRID_W, 2 * GRID_W)


def _na_kernel(q_ref, k0_ref, k1_ref, k2_ref, v0_ref, v1_ref, v2_ref, bias_ref, o_ref,
               kcat_ref, vcat_ref, *, rows):
    g = pl.program_id(1)
    for i, (kr, vr) in enumerate(((k0_ref, v0_ref), (k1_ref, v1_ref), (k2_ref, v2_ref))):
        kcat_ref[i * NA_BT:(i + 1) * NA_BT, :] = kr[...]
        vcat_ref[i * NA_BT:(i + 1) * NA_BT, :] = vr[...]
    lane = lax.broadcasted_iota(jnp.int32, (1, 2 * NA_HEAD_DIM), 1)
    left = lane < NA_HEAD_DIM
    nkeys = NA_WIN_R * GRID_W
    scale = NA_HEAD_DIM ** -0.5
    pairs = range(NA_HEADS // 2)

    def body(j, carry):
        r = g * NA_BAND + j
        r0 = jnp.clip(r - NA_WIN_R // 2, 0, rows - NA_WIN_R)
        koff = pl.multiple_of((r0 - g * NA_BAND + NA_BAND) * GRID_W, GRID_W)
        boff = pl.multiple_of((r0 - r + NA_WIN_R - 1) * GRID_W, GRID_W)
        qsl = pl.ds(pl.multiple_of(j * GRID_W, GRID_W), GRID_W)
        qb = q_ref[qsl, :] * scale
        cols = [slice(hp * 128, (hp + 1) * 128) for hp in pairs]
        s_t = []
        for cs in cols:
            q2 = qb[:, cs]
            zero = jnp.zeros_like(q2)
            rhs_t = jnp.concatenate([jnp.where(left, q2, zero), jnp.where(left, zero, q2)], axis=0)
            s_t.append(_dot_nt(kcat_ref[pl.ds(koff, nkeys), cs], rhs_t))
        s_t = [s + bias_ref[hp, pl.ds(boff, nkeys), :] for hp, s in zip(pairs, s_t)]
        p = [jnp.exp(s - jnp.max(s, axis=0, keepdims=True)) for s in s_t]
        p = [(x * (1.0 / jnp.sum(x, axis=0, keepdims=True))).astype(BF16) for x in p]
        o2 = [_dot_tn(x, vcat_ref[pl.ds(koff, nkeys), cs]) for x, cs in zip(p, cols)]
        for cs, o in zip(cols, o2):
            o_ref[qsl, cs] = jnp.where(left, o[:GRID_W], o[GRID_W:]).astype(o_ref.dtype)
        return carry

    lax.fori_loop(0, NA_BAND, body, 0, unroll=4)


def _na(z, bias_tbl, batch, seq):
    n = z.shape[0]
    rows = seq // GRID_W
    nb = rows // NA_BAND

    def kv(colblk, shift):
        return pl.BlockSpec(
            (NA_BT, NA_WIDTH),
            lambda b, g: (b * nb + jnp.clip(g + shift, 0, nb - 1), colblk))

    return pl.pallas_call(
        functools.partial(_na_kernel, rows=rows),
        grid=(batch, nb),
        in_specs=[pl.BlockSpec((NA_BT, NA_WIDTH), lambda b, g: (b * nb + g, 0)),
                  kv(1, -1), kv(1, 0), kv(1, 1), kv(2, -1), kv(2, 0), kv(2, 1),
                  pl.BlockSpec(bias_tbl.shape, lambda b, g: (0, 0, 0))],
        out_specs=pl.BlockSpec((NA_BT, NA_WIDTH), lambda b, g: (b * nb + g, 0)),
        out_shape=jax.ShapeDtypeStruct((n, NA_WIDTH), BF16),
        scratch_shapes=[pltpu.VMEM((3 * NA_BT, NA_WIDTH), BF16),
                        pltpu.VMEM((3 * NA_BT, NA_WIDTH), BF16)],
        compiler_params=pltpu.CompilerParams(
            dimension_semantics=("parallel", "arbitrary"), vmem_limit_bytes=V7X_VMEM_LIMIT),
        name="natten",
    )(z, z, z, z, z, z, z, bias_tbl)


MG_TM = 512


def _merge_kernel(x_ref, oa_ref, ob_ref, ga_ref, gb_ref, wa_ref, wb_ref, wo_ref, nf_ref,
                  x2_ref, xt_ref):
    a = _dot(oa_ref[...], wa_ref[...])
    b = _dot(ob_ref[...], wb_ref[...])
    mix = jax.nn.sigmoid(ga_ref[...]) * a + jax.nn.sigmoid(gb_ref[...]) * b
    x2 = x_ref[...] + _dot(mix.astype(BF16), wo_ref[...])
    x2_ref[...] = x2
    ms = jnp.mean(x2 * x2, axis=-1, keepdims=True)
    xt_ref[...] = (x2 * lax.rsqrt(ms + EPS) * nf_ref[...]).T.astype(BF16)


def _merge(x2d, o_a, o_b, z, wa, wb, wo, norm_ffn):
    n = x2d.shape[0]
    col_ga = (_COL_HG_G * 128 + HG_WIDTH) // D_MODEL
    full = lambda shape: pl.BlockSpec(shape, lambda i: (0, 0))
    tok = lambda w, c=0: pl.BlockSpec((MG_TM, w), lambda i: (i, c))
    return pl.pallas_call(
        _merge_kernel,
        grid=(n // MG_TM,),
        in_specs=[tok(D_MODEL), tok(HG_WIDTH), tok(NA_WIDTH), tok(D_MODEL, col_ga), tok(D_MODEL, col_ga + 1),
                  full((HG_WIDTH, D_MODEL)), full((NA_WIDTH, D_MODEL)), full((D_MODEL, D_MODEL)),
                  full((1, D_MODEL))],
        out_specs=[tok(D_MODEL), pl.BlockSpec((D_MODEL, MG_TM), lambda i: (0, i))],
        out_shape=[jax.ShapeDtypeStruct((n, D_MODEL), F32), jax.ShapeDtypeStruct((D_MODEL, n), BF16)],
        compiler_params=pltpu.CompilerParams(
            dimension_semantics=("parallel",), vmem_limit_bytes=V7X_VMEM_LIMIT),
        name="merge",
    )(x2d, o_a, o_b, z, z, wa, wb, wo, norm_ffn)


PEER_TB = 512
PEER_EB = 1024
PEER_LC = 256
PEER_KC = 256
PEER_UK = 256


def _peer_route(xt_ref, wq_ref, keys_ref, qt_ref, s_ref, toph_ref, top_ref, cand_ref, theta_ref,
                cnt_ref, coef_ref, r1_ref, e1_ref):
    K = PEER_TOPK
    half_rows = wq_ref.shape[0] // 2
    for part in range(2):
        rs = slice(part * half_rows, (part + 1) * half_rows)
        qt_ref[rs, :] = _dot(wq_ref[rs, :], xt_ref[...]).astype(BF16)

    for hp in range(2 * PEER_HEADS):
        s_ref[hp] = _dot(keys_ref[hp], qt_ref[hp * PEER_D_HALF:(hp + 1) * PEER_D_HALF, :])

    n_grp = PEER_N_KEYS // 8
    assert n_grp == K

    def exchange(rows, i, j):
        rows[i], rows[j] = jnp.maximum(rows[i], rows[j]), jnp.minimum(rows[i], rows[j])

    def extract(h, carry, *, half):
        for lc in range(xt_ref.shape[1] // 128):
            ls = slice(lc * 128, (lc + 1) * 128)
            keys = [s_ref[2 * h + half, 8 * r:8 * r + 8, ls] for r in range(n_grp)]
            rows = list(keys)
            for i, j in _SORT16:
                exchange(rows, i, j)
            for shift in (4, 2, 1):
                rows = [jnp.maximum(rows[i], pltpu.roll(rows[K - 1 - i], shift, axis=0)) for i in range(K)]
                for i, j in _BITONIC16:
                    exchange(rows, i, j)
            toph_ref[half, h, :, ls] = jnp.concatenate([rows[q][0:1] for q in range(K)], axis=0)
            if half:
                for r in range(0, n_grp, 2):
                    rank = []
                    for k in keys[r:r + 2]:
                        rk = jnp.zeros_like(k)
                        for q in range(K):
                            rk = jnp.where(k < rows[q], float(q + 1), rk)
                        rank.append(rk)
                    r1_ref[h, 8 * r:8 * r + 16, ls] = jnp.concatenate(rank, axis=0).astype(BF16)
        return carry

    lax.fori_loop(0, PEER_HEADS, functools.partial(extract, half=0), 0)
    lax.fori_loop(0, PEER_HEADS, functools.partial(extract, half=1), 0)
    for half in range(2):
        for q in range(K):
            top_ref[half, q] = jnp.concatenate(
                [toph_ref[half, h, q:q + 1, :] for h in range(PEER_HEADS)], axis=0)

    ncand = len(_CAND_PAIRS)
    for c, (p, q) in enumerate(_CAND_PAIRS):
        cand_ref[c] = top_ref[0, p] + top_ref[1, q]
    small = cand_ref.shape[1:]

    def tau_step(it, carry):
        tau, cnt = carry
        m = cand_ref[0]
        for c in range(1, ncand):
            m = jnp.maximum(m, cand_ref[c])
        n_eq = jnp.zeros(small, F32)
        for c in range(ncand):
            v = cand_ref[c]
            hit = v == m
            n_eq = n_eq + jnp.where(hit, 1.0, 0.0)
            cand_ref[c] = jnp.where(hit, -jnp.inf, v)
        return jnp.where(cnt < float(K), m, tau), cnt + n_eq

    tau, _ = lax.fori_loop(0, K, tau_step, (jnp.full(small, -jnp.inf, F32), jnp.zeros(small, F32)))
    best = top_ref[0, 0] + top_ref[1, 0]
    zsum = jnp.zeros(small, F32)
    for p, q in _CAND_PAIRS:
        v = top_ref[0, p] + top_ref[1, q]
        zsum = zsum + jnp.where(v >= tau, jnp.exp(v - best), 0.0)
    inv_z = 1.0 / zsum

    for q in range(K):
        th = jnp.full(small, jnp.inf, F32)
        b_q = top_ref[1, q]
        for p in range(K):
            a_p = top_ref[0, p]
            th = jnp.minimum(th, jnp.where(a_p + b_q >= tau, a_p, jnp.inf))
        theta_ref[q] = th

    for h in range(PEER_HEADS):
        s0 = s_ref[2 * h]
        cnt = jnp.zeros(s0.shape, F32)
        for q in range(K):
            cnt = jnp.where(s0 >= theta_ref[q, h:h + 1, :], float(q + 1), cnt)
        cnt_ref[h] = cnt
        coef_ref[h] = jnp.exp(s0 - top_ref[0, 0, h:h + 1, :])
        e1_ref[h] = (jnp.exp(s_ref[2 * h + 1] - top_ref[1, 0, h:h + 1, :]) * (0.5 * inv_z[h:h + 1])).astype(BF16)


def _peer_kernel(xt_ref, x2_ref, wq_ref, keys_ref, u_ref, vt_ref, o_ref,
                 qt_ref, s_ref, toph_ref, top_ref, cand_ref, theta_ref,
                 cnt_ref, coef_ref, r1_ref, e1_ref, g_ref, acc_ref):
    e = pl.program_id(1)
    ne = pl.num_programs(1)

    @pl.when(e == 0)
    def _():
        _peer_route(xt_ref, wq_ref, keys_ref, qt_ref, s_ref, toph_ref, top_ref, cand_ref, theta_ref,
                    cnt_ref, coef_ref, r1_ref, e1_ref)
        acc_ref[...] = jnp.zeros_like(acc_ref)

    n_i = PEER_EB // PEER_N_KEYS
    i_rows = pl.ds(pl.multiple_of(e * n_i, n_i), n_i)
    zero = jnp.zeros((PEER_N_KEYS, PEER_LC), BF16)
    n_lc = PEER_TB // PEER_LC
    cnt_rows = [[cnt_ref[h, i_rows, lc * PEER_LC:(lc + 1) * PEER_LC].astype(BF16) for h in range(PEER_HEADS)]
                for lc in range(n_lc)]
    coef_rows = [[coef_ref[h, i_rows, lc * PEER_LC:(lc + 1) * PEER_LC].astype(BF16) for h in range(PEER_HEADS)]
                 for lc in range(n_lc)]
    anchors = []
    for ii in range(n_i):
        for lc in range(n_lc):
            ls = slice(lc * PEER_LC, (lc + 1) * PEER_LC)
            gate = zero
            for h in range(PEER_HEADS):
                sel = r1_ref[h, :, ls] < cnt_rows[lc][h][ii:ii + 1]
                gate = gate + jnp.where(sel, e1_ref[h, :, ls], zero) * coef_rows[lc][h][ii:ii + 1]
            g_ref[ii * PEER_N_KEYS:(ii + 1) * PEER_N_KEYS, ls] = gate
            bits = pltpu.bitcast(gate, jnp.uint32)
            anchors.append(pltpu.bitcast((bits >> 16) >> 16, BF16))
    anchors = anchors[:len(anchors) - n_lc * PEER_KC // PEER_N_KEYS]

    n_k = D_MODEL // PEER_UK
    free_slots = (n_k - 1) * n_i
    anchor_of = {n_i + t * free_slots // len(anchors): a for t, a in enumerate(anchors)}
    cols = []
    for k in range(n_k):
        pieces = []
        for rg in range(n_i):
            piece = u_ref[rg * PEER_N_KEYS:(rg + 1) * PEER_N_KEYS, k * PEER_UK:(k + 1) * PEER_UK]
            if k * n_i + rg in anchor_of:
                piece = piece + anchor_of[k * n_i + rg]
            pieces.append(piece)
        cols.append(jnp.concatenate(pieces, axis=0))
    h_t = _dot(jnp.concatenate(cols, axis=1), xt_ref[...])
    y_t = None
    for c in range(PEER_EB // PEER_KC):
        rs = slice(c * PEER_KC, (c + 1) * PEER_KC)
        hh = h_t[rs]
        act = hh * (1.0 + lax.erf(hh * (2.0 ** -0.5)))
        part = _dot(vt_ref[:, rs], act.astype(BF16) * g_ref[rs, :])
        y_t = part if y_t is None else y_t + part
    acc_ref[...] += y_t

    @pl.when(e == ne - 1)
    def _():
        o_ref[...] = x2_ref[...] + acc_ref[...].T


def _peer(xt, x2, wq_t, keys, u_bf16, vt_bf16):
    n = x2.shape[0]
    tb, eb = PEER_TB, PEER_EB
    ne = PEER_N_EXPERTS // eb
    return pl.pallas_call(
        _peer_kernel,
        grid=(n // tb, PEER_N_EXPERTS // eb),
        in_specs=[
            pl.BlockSpec((D_MODEL, tb), lambda i, e: (0, i)),
            pl.BlockSpec((tb, D_MODEL), lambda i, e: (i, 0)),
            pl.BlockSpec(wq_t.shape, lambda i, e: (0, 0)),
            pl.BlockSpec(keys.shape, lambda i, e: (0, 0, 0)),
            pl.BlockSpec((eb, D_MODEL), lambda i, e: (e, 0)),
            pl.BlockSpec((D_MODEL, eb), lambda i, e: (0, e)),
        ],
        out_specs=pl.BlockSpec((tb, D_MODEL), lambda i, e: (i, 0)),
        out_shape=jax.ShapeDtypeStruct((n, D_MODEL), F32),
        scratch_shapes=[
            pltpu.VMEM((wq_t.shape[0], tb), BF16),
            pltpu.VMEM((2 * PEER_HEADS, PEER_N_KEYS, tb), F32),
            pltpu.VMEM((2, PEER_HEADS, PEER_TOPK, tb), F32),
            pltpu.VMEM((2, PEER_TOPK, PEER_HEADS, tb), F32),
            pltpu.VMEM((len(_CAND_PAIRS), PEER_HEADS, tb), F32),
            pltpu.VMEM((PEER_TOPK, PEER_HEADS, tb), F32),
            pltpu.VMEM((PEER_HEADS, PEER_N_KEYS, tb), F32),
            pltpu.VMEM((PEER_HEADS, PEER_N_KEYS, tb), F32),
            pltpu.VMEM((PEER_HEADS, PEER_N_KEYS, tb), BF16),
            pltpu.VMEM((PEER_HEADS, PEER_N_KEYS, tb), BF16),
            pltpu.VMEM((eb, tb), BF16),
            pltpu.VMEM((D_MODEL, tb), F32),
        ],
        compiler_params=pltpu.CompilerParams(
            dimension_semantics=("parallel", "arbitrary"), vmem_limit_bytes=V7X_VMEM_LIMIT),
        name="peer",
    )(xt, x2, wq_t, keys, u_bf16, vt_bf16)


def _trunk(x, p):
    batch, seq, _ = x.shape
    x2d = x.reshape(batch * seq, D_MODEL)
    z, qkv = _in_proj(x2d, p["norm_mix"], p["w_in"], p["qk_gain"], p["head_red"], p["head_exp"])
    o_f = _hgrn_pass(z, p["lb"], batch, seq, reverse=False)
    o_a = _hgrn_pass(z, p["lb"], batch, seq, reverse=True, o_fwd=o_f, gain=p["hg_out_norm"])
    o_b = _na(qkv, p["na_bias"], batch, seq)
    x2, xt = _merge(x2d, o_a, o_b, z, p["w_proj_a"], p["w_proj_b"], p["w_out"], p["norm_ffn"])
    y = _peer(xt, x2, p["wq_t"], p["keys"], p["expert_u"], p["expert_vt"])
    return y.reshape(batch, seq, D_MODEL)


def kernel(x_prompt, x_sample, norm_mix, w_in, lb_logits, hg_out_norm, q_norm, k_norm, rel_pos_bias,
           w_proj_a, w_proj_b, w_out, norm_ffn, w_query, sub_keys, expert_u, expert_v):
    l = 0
    lb_all = jnp.cumsum(jax.nn.softmax(lb_logits.astype(F32), axis=0), axis=0)
    head_id = np.arange(2 * NA_WIDTH) // NA_HEAD_DIM
    p = {
        "norm_mix": norm_mix[l].reshape(1, D_MODEL),
        "w_in": w_in[l].astype(BF16),
        "qk_gain": jnp.concatenate([jnp.tile(q_norm[l], NA_HEADS), jnp.tile(k_norm[l], NA_HEADS)]).reshape(1, -1),
        "head_red": jnp.asarray(head_id[:, None] == np.arange(128)[None, :], BF16),
        "head_exp": jnp.asarray(np.arange(128)[:, None] == head_id[None, :], BF16),
        "lb": lb_all[l].reshape(2, 1, HG_WIDTH),
        "hg_out_norm": hg_out_norm[l].reshape(1, HG_DV),
        "na_bias": _na_bias_table(rel_pos_bias[l]),
        "w_proj_a": w_proj_a[l].astype(BF16),
        "w_proj_b": w_proj_b[l].astype(BF16),
        "w_out": w_out[l].astype(BF16),
        "norm_ffn": norm_ffn[l].reshape(1, D_MODEL),
        "wq_t": w_query[l].T.astype(BF16),
        "keys": sub_keys[l].reshape(2 * PEER_HEADS, PEER_N_KEYS, PEER_D_HALF).astype(BF16),
        "expert_u": expert_u[l].astype(BF16),
        "expert_vt": expert_v[l].T.astype(BF16),
    }
    return (_trunk(x_prompt, p), _trunk(x_sample, p))
```

```python
import functools

import numpy as np
import jax
import jax.numpy as jnp
from jax import lax
from jax.experimental import pallas as pl
from jax.experimental.pallas import tpu as pltpu

F32 = jnp.float32
BF16 = jnp.bfloat16

D_MODEL = 1024
GRID_W = 64
EPS = 1e-6
HG_HEADS = 4
HG_DK = 128
HG_DV = 128
HG_WIDTH = HG_HEADS * HG_DV
HG_CHUNK = 64
HG_SUB = 16
HG_LEVELS = (32, 16)
assert HG_LEVELS[0] * 2 == HG_CHUNK and HG_LEVELS[-1] == HG_SUB
NA_HEADS = 8
NA_HEAD_DIM = 64
NA_WIDTH = NA_HEADS * NA_HEAD_DIM
NA_WIN_R = 8
NA_WIN_C = 16
PEER_HEADS = 8
PEER_N_KEYS = 128
PEER_N_EXPERTS = PEER_N_KEYS * PEER_N_KEYS
PEER_D_HALF = 128
PEER_TOPK = 16
D_IN = 3 * NA_WIDTH + 3 * 2 * HG_HEADS * HG_DK + HG_WIDTH + 2 * D_MODEL

_COL_HG_Q = (3 * NA_WIDTH) // 128
_COL_HG_F = _COL_HG_Q + 8
_COL_HG_I = _COL_HG_F + 8
_COL_HG_G = _COL_HG_I + 8

V7X_VMEM_LIMIT = 56 * 1024 * 1024
NEG_BIG = -1e30

_CAND_PAIRS = tuple((p, q) for p in range(PEER_TOPK) for q in range(PEER_TOPK)
                    if (p + 1) * (q + 1) <= PEER_TOPK)


def _sorting_network(n):
    def merge(lo, hi, r):
        step = r * 2
        if step < hi - lo:
            yield from merge(lo, hi, step)
            yield from merge(lo + r, hi, step)
            yield from ((i, i + r) for i in range(lo + r, hi - r, step))
        else:
            yield (lo, lo + r)

    def sort(lo, hi):
        if hi - lo >= 1:
            mid = lo + (hi - lo) // 2
            yield from sort(lo, mid)
            yield from sort(mid + 1, hi)
            yield from merge(lo, hi, 1)

    return tuple(sort(0, n - 1))


_SORT16 = _sorting_network(PEER_TOPK)
_BITONIC16 = tuple((i, i + d) for d in (8, 4, 2, 1) for i in range(PEER_TOPK) if not i & d)


def _dot(a, b):
    return jnp.dot(a, b, preferred_element_type=F32)


def _dot_nt(a, b):
    return lax.dot_general(a, b, (((1,), (1,)), ((), ())), preferred_element_type=F32)


def _dot_tn(a, b):
    return lax.dot_general(a, b, (((0,), (0,)), ((), ())), preferred_element_type=F32)


def _split_bf16(x):
    hi = x.astype(BF16)
    lo = (x - hi.astype(F32)).astype(BF16)
    return hi, lo


IN_TM = 256
IN_TN = 1024


def _in_proj_kernel(x_ref, g_ref, w_ref, qkg_ref, red_ref, exp_ref, z_ref, qkv_ref):
    x = x_ref[...]
    ms = jnp.mean(x * x, axis=-1, keepdims=True)
    xn = (x * lax.rsqrt(ms + EPS) * g_ref[...]).astype(BF16)
    for j in range(D_IN // IN_TN):
        cs = slice(j * IN_TN, (j + 1) * IN_TN)
        z = _dot(xn, w_ref[:, cs])
        if j == 0:
            hi, lo = _split_bf16(z * z)
            hi, lo = _split_bf16(_dot(hi, red_ref[...]) + _dot(lo, red_ref[...]))
            ss = _dot(hi, exp_ref[...]) + _dot(lo, exp_ref[...])
            z = z * lax.rsqrt(ss * (1.0 / NA_HEAD_DIM) + EPS) * qkg_ref[...]
        z_ref[:, cs] = z
        lo_c, hi_c = j * IN_TN, min((j + 1) * IN_TN, 3 * NA_WIDTH)
        if hi_c > lo_c:
            qkv_ref[:, lo_c:hi_c] = z[:, :hi_c - lo_c].astype(BF16)


def _in_proj(x2d, gain, w_bf16, qk_gain, head_red, head_exp):
    n = x2d.shape[0]
    const = lambda shape: pl.BlockSpec(shape, lambda i: (0, 0), pipeline_mode=pl.Buffered(1))
    return pl.pallas_call(
        _in_proj_kernel,
        grid=(n // IN_TM,),
        in_specs=[
            pl.BlockSpec((IN_TM, D_MODEL), lambda i: (i, 0)),
            const((1, D_MODEL)),
            const((D_MODEL, D_IN)),
            const((1, IN_TN)),
            const(head_red.shape),
            const(head_exp.shape),
        ],
        out_specs=[pl.BlockSpec((IN_TM, D_IN), lambda i: (i, 0)),
                   pl.BlockSpec((IN_TM, 3 * NA_WIDTH), lambda i: (i, 0))],
        out_shape=[jax.ShapeDtypeStruct((n, D_IN), F32), jax.ShapeDtypeStruct((n, 3 * NA_WIDTH), BF16)],
        compiler_params=pltpu.CompilerParams(
            dimension_semantics=("parallel",), vmem_limit_bytes=V7X_VMEM_LIMIT),
        name="in_proj",
    )(x2d, gain, w_bf16, qk_gain, head_red, head_exp)


HG_TT = 512


def _hgrn_chunk(q, f_logit, v, lb, states, tri, pair_mask, level_masks, ones, *, reverse):
    C, SB = HG_CHUNK, HG_SUB
    nblk = C // SB
    f = lb + (1.0 - lb) * jax.nn.sigmoid(f_logit)
    qs = q * jax.nn.sigmoid(q)
    hi, lo = _split_bf16(jnp.log(f))
    b = _dot(tri, hi) + _dot(tri, lo)
    c = b - jnp.log(1.0 - f)
    vb = v.astype(BF16)
    q_all = (qs * jnp.exp(b)).astype(BF16)
    b_end = b[0:1] if reverse else b[C - 1:C]
    k_d = jnp.exp(b_end - c).astype(BF16)
    carry_decay = jnp.exp(b_end)

    def boundary_refs(size):
        q_ref, k_ref = [], []
        for g0 in range(0, C, 2 * size):
            first, second = slice(g0, g0 + size), slice(g0 + size, g0 + 2 * size)
            if reverse:
                bound = jnp.broadcast_to(b[g0 + size:g0 + size + 1], (size, b.shape[1]))
                q_ref += [bound, b[second]]
                k_ref += [c[first], bound]
            else:
                bound = jnp.broadcast_to(b[g0 + size - 1:g0 + size], (size, b.shape[1]))
                q_ref += [b[first], bound]
                k_ref += [bound, c[second]]
        return jnp.concatenate(q_ref, axis=0), jnp.concatenate(k_ref, axis=0)

    level_q, level_k = [], []
    for size in HG_LEVELS:
        q_bound, k_bound = boundary_refs(size)
        level_q.append((qs * jnp.exp(b - q_bound)).astype(BF16))
        level_k.append(jnp.exp(k_bound - c).astype(BF16))

    outs, new_states = [], []
    for h, state in enumerate(states):
        cs = slice(h * HG_DK, (h + 1) * HG_DK)
        prods = []
        for blk in range(nblk):
            rows = slice(blk * SB, (blk + 1) * SB)
            prod = (qs[rows, cs][None, :, :] * jnp.exp(b[rows, cs][None, :, :] - c[rows, cs][:, None, :]))
            prods.append(jnp.where(pair_mask, prod, 0.0).reshape(SB * SB, HG_DK))
        rs = _dot(jnp.concatenate(prods, axis=0).astype(BF16), ones)
        rs = rs.reshape(nblk, SB, SB, HG_DV)
        o_diag = [jnp.sum(rs[blk] * v[blk * SB:(blk + 1) * SB, cs][:, None, :], axis=0) for blk in range(nblk)]
        s_off = sum(jnp.where(m, _dot_nt(lq[:, cs], lk[:, cs]), 0.0)
                    for m, lq, lk in zip(level_masks, level_q, level_k))
        o_off = _dot(s_off.astype(BF16), vb[:, cs])
        o_inter = _dot_nt(q_all[:, cs], state.astype(BF16))
        outs.append(o_inter + o_off + jnp.concatenate(o_diag, axis=0))
        new_states.append(state * carry_decay[:, cs] + _dot_tn(vb[:, cs], k_d[:, cs]))
    return outs, new_states


def _hgrn_kernel(*refs, reverse, finish):
    if finish:
        q_ref, f_ref, v_ref, lb_ref, of_ref, g_ref, gain_ref, o_ref, st_ref = refs
    else:
        q_ref, f_ref, v_ref, lb_ref, o_ref, st_ref = refs
    C, SB = HG_CHUNK, HG_SUB
    nchunk = HG_TT // C

    @pl.when(pl.program_id(1) == 0)
    def _():
        st_ref[...] = jnp.zeros_like(st_ref)

    row = lax.broadcasted_iota(jnp.int32, (C, C), 0)
    col = lax.broadcasted_iota(jnp.int32, (C, C), 1)
    tri = ((col >= row) if reverse else (col <= row)).astype(BF16)
    s_i = lax.broadcasted_iota(jnp.int32, (SB, SB, HG_DK), 0)
    t_i = lax.broadcasted_iota(jnp.int32, (SB, SB, HG_DK), 1)
    pair_mask = (s_i >= t_i) if reverse else (s_i <= t_i)
    ones = jnp.ones((HG_DK, HG_DV), BF16)
    level_masks = []
    for size in HG_LEVELS:
        tg, sg = row // size, col // size
        later, earlier = (0, 1) if reverse else (1, 0)
        level_masks.append((tg // 2 == sg // 2) & (tg % 2 == later) & (sg % 2 == earlier))

    def chunk(ci, carry):
        c = (nchunk - 1 - ci) if reverse else ci
        sl = pl.ds(pl.multiple_of(c * C, C), C)
        outs, states = _hgrn_chunk(q_ref[sl, :], f_ref[sl, :], v_ref[sl, :], lb_ref[0],
                                   [st_ref[h] for h in range(HG_HEADS)], tri, pair_mask, level_masks, ones,
                                   reverse=reverse)
        for h in range(HG_HEADS):
            cs = slice(h * HG_DK, (h + 1) * HG_DK)
            o, st_ref[h] = outs[h], states[h]
            if finish:
                tot = of_ref[sl, cs] + o
                ms = jnp.mean(tot * tot, axis=-1, keepdims=True)
                g = g_ref[sl, cs]
                y = tot * lax.rsqrt(ms + EPS) * gain_ref[...] * (g * jax.nn.sigmoid(g))
                o_ref[sl, cs] = y.astype(o_ref.dtype)
            else:
                o_ref[sl, cs] = o
        return carry

    lax.fori_loop(0, nchunk, chunk, 0, unroll=4)


def _hgrn_pass(z, lb, batch, seq, *, reverse, o_fwd=None, gain=None):
    n = z.shape[0]
    nt = seq // HG_TT
    finish = o_fwd is not None
    d = 1 if reverse else 0

    def tmap(t):
        return (nt - 1 - t) if reverse else t

    def zspec(col0, per_dir=True):
        blk = col0 // HG_HEADS + (d if per_dir else 0)
        return pl.BlockSpec((HG_TT, HG_WIDTH), lambda b, t: (b * nt + tmap(t), blk))

    ospec = pl.BlockSpec((HG_TT, HG_WIDTH), lambda b, t: (b * nt + tmap(t), 0))
    in_specs = [zspec(_COL_HG_Q), zspec(_COL_HG_F), zspec(_COL_HG_I),
                pl.BlockSpec((1, 1, HG_WIDTH), lambda b, t: (d, 0, 0))]
    args = [z, z, z, lb]
    if finish:
        in_specs += [ospec, zspec(_COL_HG_G, per_dir=False), pl.BlockSpec((1, HG_DV), lambda b, t: (0, 0))]
        args += [o_fwd, z, gain]
    return pl.pallas_call(
        functools.partial(_hgrn_kernel, reverse=reverse, finish=finish),
        grid=(batch, nt),
        in_specs=in_specs,
        out_specs=ospec,
        out_shape=jax.ShapeDtypeStruct((n, HG_WIDTH), BF16 if finish else F32),
        scratch_shapes=[pltpu.VMEM((HG_HEADS, HG_DV, HG_DK), F32)],
        compiler_params=pltpu.CompilerParams(
            dimension_semantics=("parallel", "arbitrary"),
            vmem_limit_bytes=V7X_VMEM_LIMIT),
        name="hgrn_bwd" if reverse else "hgrn_fwd",
    )(*args)


NA_BAND = 8
NA_BT = NA_BAND * GRID_W


def _na_bias_table(rpb):
    c = jnp.arange(GRID_W)
    c0 = jnp.clip(c - NA_WIN_C // 2, 0, GRID_W - NA_WIN_C)
    kc = jnp.arange(GRID_W)
    valid = (kc[None, :] >= c0[:, None]) & (kc[None, :] < c0[:, None] + NA_WIN_C)
    off = jnp.clip(kc[None, :] - c[:, None] + NA_WIN_C - 1, 0, 2 * NA_WIN_C - 2)
    t = jnp.where(valid[None, None], rpb.astype(F32)[:, :, off], NEG_BIG)
    t = t.reshape(NA_HEADS // 2, 2, 2 * NA_WIN_R - 1, GRID_W, GRID_W)
    t = t.transpose(0, 2, 4, 1, 3)
    return t.reshape(NA_HEADS // 2, (2 * NA_WIN_R - 1) * GRID_W, 2 * GRID_W)


def _na_kernel(q_ref, k0_ref, k1_ref, k2_ref, v0_ref, v1_ref, v2_ref, bias_ref, o_ref,
               kcat_ref, vcat_ref, *, rows):
    g = pl.program_id(1)
    for i, (kr, vr) in enumerate(((k0_ref, v0_ref), (k1_ref, v1_ref), (k2_ref, v2_ref))):
        kcat_ref[i * NA_BT:(i + 1) * NA_BT, :] = kr[...]
        vcat_ref[i * NA_BT:(i + 1) * NA_BT, :] = vr[...]
    lane = lax.broadcasted_iota(jnp.int32, (1, 2 * NA_HEAD_DIM), 1)
    left = lane < NA_HEAD_DIM
    nkeys = NA_WIN_R * GRID_W
    scale = NA_HEAD_DIM ** -0.5
    pairs = range(NA_HEADS // 2)

    def body(j, carry):
        r = g * NA_BAND + j
        r0 = jnp.clip(r - NA_WIN_R // 2, 0, rows - NA_WIN_R)
        koff = pl.multiple_of((r0 - g * NA_BAND + NA_BAND) * GRID_W, GRID_W)
        boff = pl.multiple_of((r0 - r + NA_WIN_R - 1) * GRID_W, GRID_W)
        qsl = pl.ds(pl.multiple_of(j * GRID_W, GRID_W), GRID_W)
        qb = q_ref[qsl, :] * scale
        cols = [slice(hp * 128, (hp + 1) * 128) for hp in pairs]
        s_t = []
        for cs in cols:
            q2 = qb[:, cs]
            zero = jnp.zeros_like(q2)
            rhs_t = jnp.concatenate([jnp.where(left, q2, zero), jnp.where(left, zero, q2)], axis=0)
            s_t.append(_dot_nt(kcat_ref[pl.ds(koff, nkeys), cs], rhs_t))
        s_t = [s + bias_ref[hp, pl.ds(boff, nkeys), :] for hp, s in zip(pairs, s_t)]
        p = [jnp.exp(s - jnp.max(s, axis=0, keepdims=True)) for s in s_t]
        p = [(x * (1.0 / jnp.sum(x, axis=0, keepdims=True))).astype(BF16) for x in p]
        o2 = [_dot_tn(x, vcat_ref[pl.ds(koff, nkeys), cs]) for x, cs in zip(p, cols)]
        for cs, o in zip(cols, o2):
            o_ref[qsl, cs] = jnp.where(left, o[:GRID_W], o[GRID_W:]).astype(o_ref.dtype)
        return carry

    lax.fori_loop(0, NA_BAND, body, 0, unroll=8)


def _na(z, bias_tbl, batch, seq):
    n = z.shape[0]
    rows = seq // GRID_W
    nb = rows // NA_BAND

    def kv(colblk, shift):
        return pl.BlockSpec(
            (NA_BT, NA_WIDTH),
            lambda b, g: (b * nb + jnp.clip(g + shift, 0, nb - 1), colblk))

    return pl.pallas_call(
        functools.partial(_na_kernel, rows=rows),
        grid=(batch, nb),
        in_specs=[pl.BlockSpec((NA_BT, NA_WIDTH), lambda b, g: (b * nb + g, 0)),
                  kv(1, -1), kv(1, 0), kv(1, 1), kv(2, -1), kv(2, 0), kv(2, 1),
                  pl.BlockSpec(bias_tbl.shape, lambda b, g: (0, 0, 0))],
        out_specs=pl.BlockSpec((NA_BT, NA_WIDTH), lambda b, g: (b * nb + g, 0)),
        out_shape=jax.ShapeDtypeStruct((n, NA_WIDTH), BF16),
        scratch_shapes=[pltpu.VMEM((3 * NA_BT, NA_WIDTH), BF16),
                        pltpu.VMEM((3 * NA_BT, NA_WIDTH), BF16)],
        compiler_params=pltpu.CompilerParams(
            dimension_semantics=("parallel", "arbitrary"), vmem_limit_bytes=V7X_VMEM_LIMIT),
        name="natten",
    )(z, z, z, z, z, z, z, bias_tbl)


MG_TM = 512


def _merge_kernel(x_ref, oa_ref, ob_ref, ga_ref, gb_ref, wa_ref, wb_ref, wo_ref, nf_ref,
                  x2_ref, xt_ref):
    a = _dot(oa_ref[...], wa_ref[...])
    b = _dot(ob_ref[...], wb_ref[...])
    mix = jax.nn.sigmoid(ga_ref[...]) * a + jax.nn.sigmoid(gb_ref[...]) * b
    x2 = x_ref[...] + _dot(mix.astype(BF16), wo_ref[...])
    x2_ref[...] = x2
    ms = jnp.mean(x2 * x2, axis=-1, keepdims=True)
    xt_ref[...] = (x2 * lax.rsqrt(ms + EPS) * nf_ref[...]).T.astype(BF16)


def _merge(x2d, o_a, o_b, z, wa, wb, wo, norm_ffn):
    n = x2d.shape[0]
    col_ga = (_COL_HG_G * 128 + HG_WIDTH) // D_MODEL
    full = lambda shape: pl.BlockSpec(shape, lambda i: (0, 0))
    tok = lambda w, c=0: pl.BlockSpec((MG_TM, w), lambda i: (i, c))
    return pl.pallas_call(
        _merge_kernel,
        grid=(n // MG_TM,),
        in_specs=[tok(D_MODEL), tok(HG_WIDTH), tok(NA_WIDTH), tok(D_MODEL, col_ga), tok(D_MODEL, col_ga + 1),
                  full((HG_WIDTH, D_MODEL)), full((NA_WIDTH, D_MODEL)), full((D_MODEL, D_MODEL)),
                  full((1, D_MODEL))],
        out_specs=[tok(D_MODEL), pl.BlockSpec((D_MODEL, MG_TM), lambda i: (0, i))],
        out_shape=[jax.ShapeDtypeStruct((n, D_MODEL), F32), jax.ShapeDtypeStruct((D_MODEL, n), BF16)],
        compiler_params=pltpu.CompilerParams(
            dimension_semantics=("parallel",), vmem_limit_bytes=V7X_VMEM_LIMIT),
        name="merge",
    )(x2d, o_a, o_b, z, z, wa, wb, wo, norm_ffn)


PEER_TB = 512
PEER_EB = 1024
PEER_LC = 256
PEER_KC = 256
PEER_UK = 256


def _peer_route(xt_ref, wq_ref, keys_ref, qt_ref, s_ref, toph_ref, top_ref, cand_ref, theta_ref,
                cnt_ref, coef_ref, r1_ref, e1_ref):
    K = PEER_TOPK
    half_rows = wq_ref.shape[0] // 2
    for part in range(2):
        rs = slice(part * half_rows, (part + 1) * half_rows)
        qt_ref[rs, :] = _dot(wq_ref[rs, :], xt_ref[...]).astype(BF16)

    for hp in range(2 * PEER_HEADS):
        s_ref[hp] = _dot(keys_ref[hp], qt_ref[hp * PEER_D_HALF:(hp + 1) * PEER_D_HALF, :])

    n_grp = PEER_N_KEYS // 8
    assert n_grp == K

    def exchange(rows, i, j):
        rows[i], rows[j] = jnp.maximum(rows[i], rows[j]), jnp.minimum(rows[i], rows[j])

    def extract(h, carry, *, half):
        for lc in range(xt_ref.shape[1] // 128):
            ls = slice(lc * 128, (lc + 1) * 128)
            keys = [s_ref[2 * h + half, 8 * r:8 * r + 8, ls] for r in range(n_grp)]
            rows = list(keys)
            for i, j in _SORT16:
                exchange(rows, i, j)
            for shift in (4, 2, 1):
                rows = [jnp.maximum(rows[i], pltpu.roll(rows[K - 1 - i], shift, axis=0)) for i in range(K)]
                for i, j in _BITONIC16:
                    exchange(rows, i, j)
            toph_ref[half, h, :, ls] = jnp.concatenate([rows[q][0:1] for q in range(K)], axis=0)
            if half:
                for r in range(0, n_grp, 2):
                    rank = []
                    for k in keys[r:r + 2]:
                        rk = jnp.zeros_like(k)
                        for q in range(K):
                            rk = jnp.where(k < rows[q], float(q + 1), rk)
                        rank.append(rk)
                    r1_ref[h, 8 * r:8 * r + 16, ls] = jnp.concatenate(rank, axis=0).astype(BF16)
        return carry

    lax.fori_loop(0, PEER_HEADS, functools.partial(extract, half=0), 0)
    lax.fori_loop(0, PEER_HEADS, functools.partial(extract, half=1), 0)
    for half in range(2):
        for q in range(K):
            top_ref[half, q] = jnp.concatenate(
                [toph_ref[half, h, q:q + 1, :] for h in range(PEER_HEADS)], axis=0)

    ncand = len(_CAND_PAIRS)
    for c, (p, q) in enumerate(_CAND_PAIRS):
        cand_ref[c] = top_ref[0, p] + top_ref[1, q]
    small = cand_ref.shape[1:]

    def tau_step(it, carry):
        tau, cnt = carry
        m = cand_ref[0]
        for c in range(1, ncand):
            m = jnp.maximum(m, cand_ref[c])
        n_eq = jnp.zeros(small, F32)
        for c in range(ncand):
            v = cand_ref[c]
            hit = v == m
            n_eq = n_eq + jnp.where(hit, 1.0, 0.0)
            cand_ref[c] = jnp.where(hit, -jnp.inf, v)
        return jnp.where(cnt < float(K), m, tau), cnt + n_eq

    tau, _ = lax.fori_loop(0, K, tau_step, (jnp.full(small, -jnp.inf, F32), jnp.zeros(small, F32)))
    best = top_ref[0, 0] + top_ref[1, 0]
    zsum = jnp.zeros(small, F32)
    for p, q in _CAND_PAIRS:
        v = top_ref[0, p] + top_ref[1, q]
        zsum = zsum + jnp.where(v >= tau, jnp.exp(v - best), 0.0)
    inv_z = 1.0 / zsum

    for q in range(K):
        th = jnp.full(small, jnp.inf, F32)
        b_q = top_ref[1, q]
        for p in range(K):
            a_p = top_ref[0, p]
            th = jnp.minimum(th, jnp.where(a_p + b_q >= tau, a_p, jnp.inf))
        theta_ref[q] = th

    for h in range(PEER_HEADS):
        s0 = s_ref[2 * h]
        cnt = jnp.zeros(s0.shape, F32)
        for q in range(K):
            cnt = jnp.where(s0 >= theta_ref[q, h:h + 1, :], float(q + 1), cnt)
        cnt_ref[h] = cnt
        coef_ref[h] = jnp.exp(s0 - top_ref[0, 0, h:h + 1, :])
        e1_ref[h] = (jnp.exp(s_ref[2 * h + 1] - top_ref[1, 0, h:h + 1, :]) * (0.5 * inv_z[h:h + 1])).astype(BF16)


def _peer_kernel(xt_ref, x2_ref, wq_ref, keys_ref, u_ref, vt_ref, o_ref,
                 qt_ref, s_ref, toph_ref, top_ref, cand_ref, theta_ref,
                 cnt_ref, coef_ref, r1_ref, e1_ref, g_ref, acc_ref):
    e = pl.program_id(1)
    ne = pl.num_programs(1)

    @pl.when(e == 0)
    def _():
        _peer_route(xt_ref, wq_ref, keys_ref, qt_ref, s_ref, toph_ref, top_ref, cand_ref, theta_ref,
                    cnt_ref, coef_ref, r1_ref, e1_ref)
        acc_ref[...] = jnp.zeros_like(acc_ref)

    n_i = PEER_EB // PEER_N_KEYS
    i_rows = pl.ds(pl.multiple_of(e * n_i, n_i), n_i)
    zero = jnp.zeros((PEER_N_KEYS, PEER_LC), BF16)
    n_lc = PEER_TB // PEER_LC
    cnt_rows = [[cnt_ref[h, i_rows, lc * PEER_LC:(lc + 1) * PEER_LC].astype(BF16) for h in range(PEER_HEADS)]
                for lc in range(n_lc)]
    coef_rows = [[coef_ref[h, i_rows, lc * PEER_LC:(lc + 1) * PEER_LC].astype(BF16) for h in range(PEER_HEADS)]
                 for lc in range(n_lc)]
    anchors = []
    for ii in range(n_i):
        for lc in range(n_lc):
            ls = slice(lc * PEER_LC, (lc + 1) * PEER_LC)
            gate = zero
            for h in range(PEER_HEADS):
                sel = r1_ref[h, :, ls] < cnt_rows[lc][h][ii:ii + 1]
                gate = gate + jnp.where(sel, e1_ref[h, :, ls], zero) * coef_rows[lc][h][ii:ii + 1]
            g_ref[ii * PEER_N_KEYS:(ii + 1) * PEER_N_KEYS, ls] = gate
            bits = pltpu.bitcast(gate, jnp.uint32)
            anchors.append(pltpu.bitcast((bits >> 16) >> 16, BF16))
    anchors = anchors[:len(anchors) - n_lc * PEER_KC // PEER_N_KEYS]

    n_k = D_MODEL // PEER_UK
    free_slots = (n_k - 1) * n_i
    anchor_of = {n_i + t * free_slots // len(anchors): a for t, a in enumerate(anchors)}
    cols = []
    for k in range(n_k):
        pieces = []
        for rg in range(n_i):
            piece = u_ref[rg * PEER_N_KEYS:(rg + 1) * PEER_N_KEYS, k * PEER_UK:(k + 1) * PEER_UK]
            if k * n_i + rg in anchor_of:
                piece = piece + anchor_of[k * n_i + rg]
            pieces.append(piece)
        cols.append(jnp.concatenate(pieces, axis=0))
    h_t = _dot(jnp.concatenate(cols, axis=1), xt_ref[...])
    y_t = None
    for c in range(PEER_EB // PEER_KC):
        rs = slice(c * PEER_KC, (c + 1) * PEER_KC)
        hh = h_t[rs]
        act = hh * (1.0 + lax.erf(hh * (2.0 ** -0.5)))
        part = _dot(vt_ref[:, rs], act.astype(BF16) * g_ref[rs, :])
        y_t = part if y_t is None else y_t + part
    acc_ref[...] += y_t

    @pl.when(e == ne - 1)
    def _():
        o_ref[...] = x2_ref[...] + acc_ref[...].T


def _peer(xt, x2, wq_t, keys, u_bf16, vt_bf16):
    n = x2.shape[0]
    tb, eb = PEER_TB, PEER_EB
    ne = PEER_N_EXPERTS // eb
    return pl.pallas_call(
        _peer_kernel,
        grid=(n // tb, PEER_N_EXPERTS // eb),
        in_specs=[
            pl.BlockSpec((D_MODEL, tb), lambda i, e: (0, i)),
            pl.BlockSpec((tb, D_MODEL), lambda i, e: (i, 0)),
            pl.BlockSpec(wq_t.shape, lambda i, e: (0, 0)),
            pl.BlockSpec(keys.shape, lambda i, e: (0, 0, 0)),
            pl.BlockSpec((eb, D_MODEL), lambda i, e: (e, 0)),
            pl.BlockSpec((D_MODEL, eb), lambda i, e: (0, e)),
        ],
        out_specs=pl.BlockSpec((tb, D_MODEL), lambda i, e: (i, 0)),
        out_shape=jax.ShapeDtypeStruct((n, D_MODEL), F32),
        scratch_shapes=[
            pltpu.VMEM((wq_t.shape[0], tb), BF16),
            pltpu.VMEM((2 * PEER_HEADS, PEER_N_KEYS, tb), F32),
            pltpu.VMEM((2, PEER_HEADS, PEER_TOPK, tb), F32),
            pltpu.VMEM((2, PEER_TOPK, PEER_HEADS, tb), F32),
            pltpu.VMEM((len(_CAND_PAIRS), PEER_HEADS, tb), F32),
            pltpu.VMEM((PEER_TOPK, PEER_HEADS, tb), F32),
            pltpu.VMEM((PEER_HEADS, PEER_N_KEYS, tb), F32),
            pltpu.VMEM((PEER_HEADS, PEER_N_KEYS, tb), F32),
            pltpu.VMEM((PEER_HEADS, PEER_N_KEYS, tb), BF16),
            pltpu.VMEM((PEER_HEADS, PEER_N_KEYS, tb), BF16),
            pltpu.VMEM((eb, tb), BF16),
            pltpu.VMEM((D_MODEL, tb), F32),
        ],
        compiler_params=pltpu.CompilerParams(
            dimension_semantics=("parallel", "arbitrary"), vmem_limit_bytes=V7X_VMEM_LIMIT),
        name="peer",
    )(xt, x2, wq_t, keys, u_bf16, vt_bf16)


def _trunk(x, p):
    batch, seq, _ = x.shape
    x2d = x.reshape(batch * seq, D_MODEL)
    z, qkv = _in_proj(x2d, p["norm_mix"], p["w_in"], p["qk_gain"], p["head_red"], p["head_exp"])
    o_f = _hgrn_pass(z, p["lb"], batch, seq, reverse=False)
    o_a = _hgrn_pass(z, p["lb"], batch, seq, reverse=True, o_fwd=o_f, gain=p["hg_out_norm"])
    o_b = _na(qkv, p["na_bias"], batch, seq)
    x2, xt = _merge(x2d, o_a, o_b, z, p["w_proj_a"], p["w_proj_b"], p["w_out"], p["norm_ffn"])
    y = _peer(xt, x2, p["wq_t"], p["keys"], p["expert_u"], p["expert_vt"])
    return y.reshape(batch, seq, D_MODEL)


def kernel(x_prompt, x_sample, norm_mix, w_in, lb_logits, hg_out_norm, q_norm, k_norm, rel_pos_bias,
           w_proj_a, w_proj_b, w_out, norm_ffn, w_query, sub_keys, expert_u, expert_v):
    l = 0
    lb_all = jnp.cumsum(jax.nn.softmax(lb_logits.astype(F32), axis=0), axis=0)
    head_id = np.arange(2 * NA_WIDTH) // NA_HEAD_DIM
    p = {
        "norm_mix": norm_mix[l].reshape(1, D_MODEL),
        "w_in": w_in[l].astype(BF16),
        "qk_gain": jnp.concatenate([jnp.tile(q_norm[l], NA_HEADS), jnp.tile(k_norm[l], NA_HEADS)]).reshape(1, -1),
        "head_red": jnp.asarray(head_id[:, None] == np.arange(128)[None, :], BF16),
        "head_exp": jnp.asarray(np.arange(128)[:, None] == head_id[None, :], BF16),
        "lb": lb_all[l].reshape(2, 1, HG_WIDTH),
        "hg_out_norm": hg_out_norm[l].reshape(1, HG_DV),
        "na_bias": _na_bias_table(rel_pos_bias[l]),
        "w_proj_a": w_proj_a[l].astype(BF16),
        "w_proj_b": w_proj_b[l].astype(BF16),
        "w_out": w_out[l].astype(BF16),
        "norm_ffn": norm_ffn[l].reshape(1, D_MODEL),
        "wq_t": w_query[l].T.astype(BF16),
        "keys": sub_keys[l].reshape(2 * PEER_HEADS, PEER_N_KEYS, PEER_D_HALF).astype(BF16),
        "expert_u": expert_u[l].astype(BF16),
        "expert_vt": expert_v[l].T.astype(BF16),
    }
    return (_trunk(x_prompt, p), _trunk(x_sample, p))
```

```python
import functools

import numpy as np
import jax
import jax.numpy as jnp
from jax import lax
from jax.experimental import pallas as pl
from jax.experimental.pallas import tpu as pltpu

F32 = jnp.float32
BF16 = jnp.bfloat16

D_MODEL = 1024
GRID_W = 64
EPS = 1e-6
HG_HEADS = 4
HG_DK = 128
HG_DV = 128
HG_WIDTH = HG_HEADS * HG_DV
HG_CHUNK = 64
HG_SUB = 16
HG_LEVELS = (32, 16)
assert HG_LEVELS[0] * 2 == HG_CHUNK and HG_LEVELS[-1] == HG_SUB
NA_HEADS = 8
NA_HEAD_DIM = 64
NA_WIDTH = NA_HEADS * NA_HEAD_DIM
NA_WIN_R = 8
NA_WIN_C = 16
PEER_HEADS = 8
PEER_N_KEYS = 128
PEER_N_EXPERTS = PEER_N_KEYS * PEER_N_KEYS
PEER_D_HALF = 128
PEER_TOPK = 16
D_IN = 3 * NA_WIDTH + 3 * 2 * HG_HEADS * HG_DK + HG_WIDTH + 2 * D_MODEL

V7X_LANES = 128
V7X_SUBLANES = 8

_COL_HG_Q = (3 * NA_WIDTH) // V7X_LANES
_COL_HG_F = _COL_HG_Q + 8
_COL_HG_I = _COL_HG_F + 8
_COL_HG_G = _COL_HG_I + 8

V7X_VMEM_LIMIT = 56 * 1024 * 1024
NEG_BIG = -1e30

_CAND_PAIRS = tuple((p, q) for p in range(PEER_TOPK) for q in range(PEER_TOPK)
                    if (p + 1) * (q + 1) <= PEER_TOPK)


def _sorting_network(n):
    def merge(lo, hi, r):
        step = r * 2
        if step < hi - lo:
            yield from merge(lo, hi, step)
            yield from merge(lo + r, hi, step)
            yield from ((i, i + r) for i in range(lo + r, hi - r, step))
        else:
            yield (lo, lo + r)

    def sort(lo, hi):
        if hi - lo >= 1:
            mid = lo + (hi - lo) // 2
            yield from sort(lo, mid)
            yield from sort(mid + 1, hi)
            yield from merge(lo, hi, 1)

    return tuple(sort(0, n - 1))


_SORT16 = _sorting_network(PEER_TOPK)
_BITONIC16 = tuple((i, i + d) for d in (8, 4, 2, 1) for i in range(PEER_TOPK) if not i & d)


def _dot(a, b):
    return jnp.dot(a, b, preferred_element_type=F32)


def _dot_nt(a, b):
    return lax.dot_general(a, b, (((1,), (1,)), ((), ())), preferred_element_type=F32)


def _dot_tn(a, b):
    return lax.dot_general(a, b, (((0,), (0,)), ((), ())), preferred_element_type=F32)


def _split_bf16(x):
    hi = x.astype(BF16)
    lo = (x - hi.astype(F32)).astype(BF16)
    return hi, lo


IN_TM = 256
IN_TN = 1024


def _in_proj_kernel(x_ref, g_ref, w_ref, qkg_ref, red_ref, exp_ref, z_ref, qkv_ref):
    x = x_ref[...]
    ms = jnp.mean(x * x, axis=-1, keepdims=True)
    xn = (x * lax.rsqrt(ms + EPS) * g_ref[...]).astype(BF16)
    for j in range(D_IN // IN_TN):
        cs = slice(j * IN_TN, (j + 1) * IN_TN)
        z = _dot(xn, w_ref[:, cs])
        if j == 0:
            hi, lo = _split_bf16(z * z)
            hi, lo = _split_bf16(_dot(hi, red_ref[...]) + _dot(lo, red_ref[...]))
            ss = _dot(hi, exp_ref[...]) + _dot(lo, exp_ref[...])
            z = z * lax.rsqrt(ss * (1.0 / NA_HEAD_DIM) + EPS) * qkg_ref[...]
        z_ref[:, cs] = z
        lo_c, hi_c = j * IN_TN, min((j + 1) * IN_TN, 3 * NA_WIDTH)
        if hi_c > lo_c:
            qkv_ref[:, lo_c:hi_c] = z[:, :hi_c - lo_c].astype(BF16)


def _in_proj(x2d, gain, w_bf16, qk_gain, head_red, head_exp):
    n = x2d.shape[0]
    const = lambda shape: pl.BlockSpec(shape, lambda i: (0, 0), pipeline_mode=pl.Buffered(1))
    return pl.pallas_call(
        _in_proj_kernel,
        grid=(n // IN_TM,),
        in_specs=[
            pl.BlockSpec((IN_TM, D_MODEL), lambda i: (i, 0)),
            const((1, D_MODEL)),
            const((D_MODEL, D_IN)),
            const((1, IN_TN)),
            const(head_red.shape),
            const(head_exp.shape),
        ],
        out_specs=[pl.BlockSpec((IN_TM, D_IN), lambda i: (i, 0)),
                   pl.BlockSpec((IN_TM, 3 * NA_WIDTH), lambda i: (i, 0))],
        out_shape=[jax.ShapeDtypeStruct((n, D_IN), F32), jax.ShapeDtypeStruct((n, 3 * NA_WIDTH), BF16)],
        compiler_params=pltpu.CompilerParams(
            dimension_semantics=("parallel",), vmem_limit_bytes=V7X_VMEM_LIMIT),
        name="in_proj",
    )(x2d, gain, w_bf16, qk_gain, head_red, head_exp)


HG_TT = 512


def _hgrn_chunk(q, f_logit, v, lb, states, tri, pair_mask, level_masks, ones, *, reverse):
    C, SB = HG_CHUNK, HG_SUB
    nblk = C // SB
    f = lb + (1.0 - lb) * jax.nn.sigmoid(f_logit)
    qs = q * jax.nn.sigmoid(q)
    hi, lo = _split_bf16(jnp.log(f))
    b = _dot(tri, hi) + _dot(tri, lo)
    c = b - jnp.log(1.0 - f)
    vb = v.astype(BF16)
    q_all = (qs * jnp.exp(b)).astype(BF16)
    b_end = b[0:1] if reverse else b[C - 1:C]
    k_d = jnp.exp(b_end - c).astype(BF16)
    carry_decay = jnp.exp(b_end)

    def boundary_refs(size):
        q_ref, k_ref = [], []
        for g0 in range(0, C, 2 * size):
            first, second = slice(g0, g0 + size), slice(g0 + size, g0 + 2 * size)
            if reverse:
                bound = jnp.broadcast_to(b[g0 + size:g0 + size + 1], (size, b.shape[1]))
                q_ref += [bound, b[second]]
                k_ref += [c[first], bound]
            else:
                bound = jnp.broadcast_to(b[g0 + size - 1:g0 + size], (size, b.shape[1]))
                q_ref += [b[first], bound]
                k_ref += [bound, c[second]]
        return jnp.concatenate(q_ref, axis=0), jnp.concatenate(k_ref, axis=0)

    level_q, level_k = [], []
    for size in HG_LEVELS:
        q_bound, k_bound = boundary_refs(size)
        level_q.append((qs * jnp.exp(b - q_bound)).astype(BF16))
        level_k.append(jnp.exp(k_bound - c).astype(BF16))

    outs, new_states = [], []
    for h, state in enumerate(states):
        cs = slice(h * HG_DK, (h + 1) * HG_DK)
        prods = []
        for blk in range(nblk):
            rows = slice(blk * SB, (blk + 1) * SB)
            prod = (qs[rows, cs][None, :, :] * jnp.exp(b[rows, cs][None, :, :] - c[rows, cs][:, None, :]))
            prods.append(jnp.where(pair_mask, prod, 0.0).reshape(SB * SB, HG_DK))
        rs = _dot(jnp.concatenate(prods, axis=0).astype(BF16), ones)
        rs = rs.reshape(nblk, SB, SB, HG_DV)
        o_diag = [jnp.sum(rs[blk] * v[blk * SB:(blk + 1) * SB, cs][:, None, :], axis=0) for blk in range(nblk)]
        s_off = sum(jnp.where(m, _dot_nt(lq[:, cs], lk[:, cs]), 0.0)
                    for m, lq, lk in zip(level_masks, level_q, level_k))
        o_off = _dot(s_off.astype(BF16), vb[:, cs])
        o_inter = _dot_nt(q_all[:, cs], state.astype(BF16))
        outs.append(o_inter + o_off + jnp.concatenate(o_diag, axis=0))
        new_states.append(state * carry_decay[:, cs] + _dot_tn(vb[:, cs], k_d[:, cs]))
    return outs, new_states


def _hgrn_kernel(*refs, reverse, finish):
    if finish:
        q_ref, f_ref, v_ref, lb_ref, of_ref, g_ref, gain_ref, o_ref, st_ref = refs
    else:
        q_ref, f_ref, v_ref, lb_ref, o_ref, st_ref = refs
    C, SB = HG_CHUNK, HG_SUB
    nchunk = HG_TT // C

    @pl.when(pl.program_id(1) == 0)
    def _():
        st_ref[...] = jnp.zeros_like(st_ref)

    row = lax.broadcasted_iota(jnp.int32, (C, C), 0)
    col = lax.broadcasted_iota(jnp.int32, (C, C), 1)
    tri = ((col >= row) if reverse else (col <= row)).astype(BF16)
    s_i = lax.broadcasted_iota(jnp.int32, (SB, SB, HG_DK), 0)
    t_i = lax.broadcasted_iota(jnp.int32, (SB, SB, HG_DK), 1)
    pair_mask = (s_i >= t_i) if reverse else (s_i <= t_i)
    ones = jnp.ones((HG_DK, HG_DV), BF16)
    level_masks = []
    for size in HG_LEVELS:
        tg, sg = row // size, col // size
        later, earlier = (0, 1) if reverse else (1, 0)
        level_masks.append((tg // 2 == sg // 2) & (tg % 2 == later) & (sg % 2 == earlier))

    def chunk(ci, carry):
        c = (nchunk - 1 - ci) if reverse else ci
        sl = pl.ds(pl.multiple_of(c * C, C), C)
        outs, states = _hgrn_chunk(q_ref[sl, :], f_ref[sl, :], v_ref[sl, :], lb_ref[0],
                                   [st_ref[h] for h in range(HG_HEADS)], tri, pair_mask, level_masks, ones,
                                   reverse=reverse)
        for h in range(HG_HEADS):
            cs = slice(h * HG_DK, (h + 1) * HG_DK)
            o, st_ref[h] = outs[h], states[h]
            if finish:
                tot = of_ref[sl, cs] + o
                ms = jnp.mean(tot * tot, axis=-1, keepdims=True)
                g = g_ref[sl, cs]
                y = tot * lax.rsqrt(ms + EPS) * gain_ref[...] * (g * jax.nn.sigmoid(g))
                o_ref[sl, cs] = y.astype(o_ref.dtype)
            else:
                o_ref[sl, cs] = o
        return carry

    lax.fori_loop(0, nchunk, chunk, 0, unroll=4)


def _hgrn_pass(z, lb, batch, seq, *, reverse, o_fwd=None, gain=None):
    n = z.shape[0]
    nt = seq // HG_TT
    finish = o_fwd is not None
    d = 1 if reverse else 0

    def tmap(t):
        return (nt - 1 - t) if reverse else t

    def zspec(col0, per_dir=True):
        blk = col0 // HG_HEADS + (d if per_dir else 0)
        return pl.BlockSpec((HG_TT, HG_WIDTH), lambda b, t: (b * nt + tmap(t), blk))

    ospec = pl.BlockSpec((HG_TT, HG_WIDTH), lambda b, t: (b * nt + tmap(t), 0))
    in_specs = [zspec(_COL_HG_Q), zspec(_COL_HG_F), zspec(_COL_HG_I),
                pl.BlockSpec((1, 1, HG_WIDTH), lambda b, t: (d, 0, 0))]
    args = [z, z, z, lb]
    if finish:
        in_specs += [ospec, zspec(_COL_HG_G, per_dir=False), pl.BlockSpec((1, HG_DV), lambda b, t: (0, 0))]
        args += [o_fwd, z, gain]
    return pl.pallas_call(
        functools.partial(_hgrn_kernel, reverse=reverse, finish=finish),
        grid=(batch, nt),
        in_specs=in_specs,
        out_specs=ospec,
        out_shape=jax.ShapeDtypeStruct((n, HG_WIDTH), BF16 if finish else F32),
        scratch_shapes=[pltpu.VMEM((HG_HEADS, HG_DV, HG_DK), F32)],
        compiler_params=pltpu.CompilerParams(
            dimension_semantics=("parallel", "arbitrary"),
            vmem_limit_bytes=V7X_VMEM_LIMIT),
        name="hgrn_bwd" if reverse else "hgrn_fwd",
    )(*args)


NA_BAND = 8
NA_BT = NA_BAND * GRID_W


def _na_bias_table(rpb):
    c = jnp.arange(GRID_W)
    c0 = jnp.clip(c - NA_WIN_C // 2, 0, GRID_W - NA_WIN_C)
    kc = jnp.arange(GRID_W)
    valid = (kc[None, :] >= c0[:, None]) & (kc[None, :] < c0[:, None] + NA_WIN_C)
    off = jnp.clip(kc[None, :] - c[:, None] + NA_WIN_C - 1, 0, 2 * NA_WIN_C - 2)
    t = jnp.where(valid[None, None], rpb.astype(F32)[:, :, off], NEG_BIG)
    t = t.reshape(NA_HEADS // 2, 2, 2 * NA_WIN_R - 1, GRID_W, GRID_W)
    t = t.transpose(0, 2, 4, 1, 3)
    return t.reshape(NA_HEADS // 2, (2 * NA_WIN_R - 1) * GRID_W, 2 * GRID_W)


def _na_kernel(q_ref, k0_ref, k1_ref, k2_ref, v0_ref, v1_ref, v2_ref, bias_ref, o_ref,
               kcat_ref, vcat_ref, *, rows):
    g = pl.program_id(1)
    for i, (kr, vr) in enumerate(((k0_ref, v0_ref), (k1_ref, v1_ref), (k2_ref, v2_ref))):
        kcat_ref[i * NA_BT:(i + 1) * NA_BT, :] = kr[...]
        vcat_ref[i * NA_BT:(i + 1) * NA_BT, :] = vr[...]
    lane = lax.broadcasted_iota(jnp.int32, (1, 2 * NA_HEAD_DIM), 1)
    left = lane < NA_HEAD_DIM
    nkeys = NA_WIN_R * GRID_W
    scale = NA_HEAD_DIM ** -0.5
    pairs = range(NA_HEADS // 2)

    def body(j, carry):
        r = g * NA_BAND + j
        r0 = jnp.clip(r - NA_WIN_R // 2, 0, rows - NA_WIN_R)
        koff = pl.multiple_of((r0 - g * NA_BAND + NA_BAND) * GRID_W, GRID_W)
        boff = pl.multiple_of((r0 - r + NA_WIN_R - 1) * GRID_W, GRID_W)
        qsl = pl.ds(pl.multiple_of(j * GRID_W, GRID_W), GRID_W)
        qb = q_ref[qsl, :] * scale
        cols = [slice(hp * 2 * NA_HEAD_DIM, (hp + 1) * 2 * NA_HEAD_DIM) for hp in pairs]
        s_t = []
        for cs in cols:
            q2 = qb[:, cs]
            zero = jnp.zeros_like(q2)
            rhs_t = jnp.concatenate([jnp.where(left, q2, zero), jnp.where(left, zero, q2)], axis=0)
            s_t.append(_dot_nt(kcat_ref[pl.ds(koff, nkeys), cs], rhs_t))
        s_t = [s + bias_ref[hp, pl.ds(boff, nkeys), :] for hp, s in zip(pairs, s_t)]
        p = [jnp.exp(s - jnp.max(s, axis=0, keepdims=True)) for s in s_t]
        p = [(x * (1.0 / jnp.sum(x, axis=0, keepdims=True))).astype(BF16) for x in p]
        o2 = [_dot_tn(x, vcat_ref[pl.ds(koff, nkeys), cs]) for x, cs in zip(p, cols)]
        for cs, o in zip(cols, o2):
            o_ref[qsl, cs] = jnp.where(left, o[:GRID_W], o[GRID_W:]).astype(o_ref.dtype)
        return carry

    lax.fori_loop(0, NA_BAND, body, 0, unroll=8)


def _na(z, bias_tbl, batch, seq):
    n = z.shape[0]
    rows = seq // GRID_W
    nb = rows // NA_BAND

    def kv(colblk, shift):
        return pl.BlockSpec(
            (NA_BT, NA_WIDTH),
            lambda b, g: (b * nb + jnp.clip(g + shift, 0, nb - 1), colblk))

    return pl.pallas_call(
        functools.partial(_na_kernel, rows=rows),
        grid=(batch, nb),
        in_specs=[pl.BlockSpec((NA_BT, NA_WIDTH), lambda b, g: (b * nb + g, 0)),
                  kv(1, -1), kv(1, 0), kv(1, 1), kv(2, -1), kv(2, 0), kv(2, 1),
                  pl.BlockSpec(bias_tbl.shape, lambda b, g: (0, 0, 0))],
        out_specs=pl.BlockSpec((NA_BT, NA_WIDTH), lambda b, g: (b * nb + g, 0)),
        out_shape=jax.ShapeDtypeStruct((n, NA_WIDTH), BF16),
        scratch_shapes=[pltpu.VMEM((3 * NA_BT, NA_WIDTH), BF16),
                        pltpu.VMEM((3 * NA_BT, NA_WIDTH), BF16)],
        compiler_params=pltpu.CompilerParams(
            dimension_semantics=("parallel", "arbitrary"), vmem_limit_bytes=V7X_VMEM_LIMIT),
        name="natten",
    )(z, z, z, z, z, z, z, bias_tbl)


MG_TM = 512


def _merge_kernel(x_ref, oa_ref, ob_ref, ga_ref, gb_ref, wa_ref, wb_ref, wo_ref, nf_ref,
                  x2_ref, xt_ref):
    a = _dot(oa_ref[...], wa_ref[...])
    b = _dot(ob_ref[...], wb_ref[...])
    mix = jax.nn.sigmoid(ga_ref[...]) * a + jax.nn.sigmoid(gb_ref[...]) * b
    x2 = x_ref[...] + _dot(mix.astype(BF16), wo_ref[...])
    x2_ref[...] = x2
    ms = jnp.mean(x2 * x2, axis=-1, keepdims=True)
    xt_ref[...] = (x2 * lax.rsqrt(ms + EPS) * nf_ref[...]).T.astype(BF16)


def _merge(x2d, o_a, o_b, z, wa, wb, wo, norm_ffn):
    n = x2d.shape[0]
    col_ga = (_COL_HG_G * V7X_LANES + HG_WIDTH) // D_MODEL
    full = lambda shape: pl.BlockSpec(shape, lambda i: (0, 0))
    tok = lambda w, c=0: pl.BlockSpec((MG_TM, w), lambda i: (i, c))
    return pl.pallas_call(
        _merge_kernel,
        grid=(n // MG_TM,),
        in_specs=[tok(D_MODEL), tok(HG_WIDTH), tok(NA_WIDTH), tok(D_MODEL, col_ga), tok(D_MODEL, col_ga + 1),
                  full((HG_WIDTH, D_MODEL)), full((NA_WIDTH, D_MODEL)), full((D_MODEL, D_MODEL)),
                  full((1, D_MODEL))],
        out_specs=[tok(D_MODEL), pl.BlockSpec((D_MODEL, MG_TM), lambda i: (0, i))],
        out_shape=[jax.ShapeDtypeStruct((n, D_MODEL), F32), jax.ShapeDtypeStruct((D_MODEL, n), BF16)],
        compiler_params=pltpu.CompilerParams(
            dimension_semantics=("parallel",), vmem_limit_bytes=V7X_VMEM_LIMIT),
        name="merge",
    )(x2d, o_a, o_b, z, z, wa, wb, wo, norm_ffn)


PEER_TB = 512
PEER_EB = 1024
PEER_LC = 256
PEER_KC = 256
PEER_UK = 256


def _peer_route(xt_ref, wq_ref, keys_ref, qt_ref, s_ref, toph_ref, top_ref, theta_ref,
                cnt_ref, coef_ref, r1_ref, e1_ref):
    K = PEER_TOPK
    half_rows = wq_ref.shape[0] // 2
    for part in range(2):
        rs = slice(part * half_rows, (part + 1) * half_rows)
        qt_ref[rs, :] = _dot(wq_ref[rs, :], xt_ref[...]).astype(BF16)

    for hp in range(2 * PEER_HEADS):
        s_ref[hp] = _dot(keys_ref[hp], qt_ref[hp * PEER_D_HALF:(hp + 1) * PEER_D_HALF, :])

    n_grp = PEER_N_KEYS // V7X_SUBLANES
    assert n_grp == K

    def exchange(rows, i, j):
        rows[i], rows[j] = jnp.maximum(rows[i], rows[j]), jnp.minimum(rows[i], rows[j])

    def extract(h, carry, *, half):
        for lc in range(xt_ref.shape[1] // V7X_LANES):
            ls = slice(lc * V7X_LANES, (lc + 1) * V7X_LANES)
            keys = [s_ref[2 * h + half, V7X_SUBLANES * r:V7X_SUBLANES * (r + 1), ls] for r in range(n_grp)]
            rows = list(keys)
            for i, j in _SORT16:
                exchange(rows, i, j)
            for shift in (4, 2, 1):
                rows = [jnp.maximum(rows[i], pltpu.roll(rows[K - 1 - i], shift, axis=0)) for i in range(K)]
                for i, j in _BITONIC16:
                    exchange(rows, i, j)
            toph_ref[half, h, :, ls] = jnp.concatenate([rows[q][0:1] for q in range(K)], axis=0)
            if half:
                for r in range(0, n_grp, 2):
                    rank = []
                    for k in keys[r:r + 2]:
                        rk = jnp.zeros_like(k)
                        for q in range(K):
                            rk = jnp.where(k < rows[q], float(q + 1), rk)
                        rank.append(rk)
                    r1_ref[h, V7X_SUBLANES * r:V7X_SUBLANES * (r + 2), ls] = (
                        jnp.concatenate(rank, axis=0).astype(BF16))
        return carry

    lax.fori_loop(0, PEER_HEADS, functools.partial(extract, half=0), 0)
    lax.fori_loop(0, PEER_HEADS, functools.partial(extract, half=1), 0)
    for half in range(2):
        for q in range(K):
            top_ref[half, q] = jnp.concatenate(
                [toph_ref[half, h, q:q + 1, :] for h in range(PEER_HEADS)], axis=0)

    small = top_ref.shape[2:]
    neg = jnp.full(small, -jnp.inf, F32)
    lists = [[top_ref[0, p] + top_ref[1, q] if (p + 1) * (q + 1) <= K else neg for q in range(K)]
             for p in range(K // 2)]
    lists.append([top_ref[0, p] + top_ref[1, 0] if p < K else neg for p in range(K // 2, K + K // 2)])
    assert sum(x is not neg for lst in lists for x in lst) == len(_CAND_PAIRS)
    merged = lists[0]
    for other in lists[1:]:
        merged = [jnp.maximum(merged[i], other[K - 1 - i]) for i in range(K)]
        for i, j in _BITONIC16:
            exchange(merged, i, j)
    tau = merged[K - 1]
    best = top_ref[0, 0] + top_ref[1, 0]
    zsum = jnp.zeros(small, F32)
    for p, q in _CAND_PAIRS:
        v = top_ref[0, p] + top_ref[1, q]
        zsum = zsum + jnp.where(v >= tau, jnp.exp(v - best), 0.0)
    inv_z = 1.0 / zsum

    for q in range(K):
        th = jnp.full(small, jnp.inf, F32)
        b_q = top_ref[1, q]
        for p in range(K):
            a_p = top_ref[0, p]
            th = jnp.minimum(th, jnp.where(a_p + b_q >= tau, a_p, jnp.inf))
        theta_ref[q] = th

    for h in range(PEER_HEADS):
        s0 = s_ref[2 * h]
        cnt = jnp.zeros(s0.shape, F32)
        for q in range(K):
            cnt = jnp.where(s0 >= theta_ref[q, h:h + 1, :], float(q + 1), cnt)
        cnt_ref[h] = cnt
        coef_ref[h] = jnp.exp(s0 - top_ref[0, 0, h:h + 1, :])
        e1_ref[h] = (jnp.exp(s_ref[2 * h + 1] - top_ref[1, 0, h:h + 1, :]) * (0.5 * inv_z[h:h + 1])).astype(BF16)


def _peer_kernel(xt_ref, x2_ref, wq_ref, keys_ref, u_ref, vt_ref, o_ref,
                 qt_ref, s_ref, toph_ref, top_ref, theta_ref,
                 cnt_ref, coef_ref, r1_ref, e1_ref, g_ref, acc_ref):
    e = pl.program_id(1)
    ne = pl.num_programs(1)

    @pl.when(e == 0)
    def _():
        _peer_route(xt_ref, wq_ref, keys_ref, qt_ref, s_ref, toph_ref, top_ref, theta_ref,
                    cnt_ref, coef_ref, r1_ref, e1_ref)
        acc_ref[...] = jnp.zeros_like(acc_ref)

    n_i = PEER_EB // PEER_N_KEYS
    i_rows = pl.ds(pl.multiple_of(e * n_i, n_i), n_i)
    zero = jnp.zeros((PEER_N_KEYS, PEER_LC), BF16)
    n_lc = PEER_TB // PEER_LC
    cnt_rows = [[cnt_ref[h, i_rows, lc * PEER_LC:(lc + 1) * PEER_LC].astype(BF16) for h in range(PEER_HEADS)]
                for lc in range(n_lc)]
    coef_rows = [[coef_ref[h, i_rows, lc * PEER_LC:(lc + 1) * PEER_LC].astype(BF16) for h in range(PEER_HEADS)]
                 for lc in range(n_lc)]
    anchors = []
    for ii in range(n_i):
        for lc in range(n_lc):
            ls = slice(lc * PEER_LC, (lc + 1) * PEER_LC)
            gate = zero
            for h in range(PEER_HEADS):
                sel = r1_ref[h, :, ls] < cnt_rows[lc][h][ii:ii + 1]
                gate = gate + jnp.where(sel, e1_ref[h, :, ls], zero) * coef_rows[lc][h][ii:ii + 1]
            g_ref[ii * PEER_N_KEYS:(ii + 1) * PEER_N_KEYS, ls] = gate
            bits = pltpu.bitcast(gate, jnp.uint32)
            anchors.append(pltpu.bitcast((bits >> 16) >> 16, BF16))
    anchors = anchors[:len(anchors) - n_lc * PEER_KC // PEER_N_KEYS]

    n_k = D_MODEL // PEER_UK
    free_slots = (n_k - 1) * n_i
    anchor_of = {n_i + t * free_slots // len(anchors): a for t, a in enumerate(anchors)}
    cols = []
    for k in range(n_k):
        pieces = []
        for rg in range(n_i):
            piece = u_ref[rg * PEER_N_KEYS:(rg + 1) * PEER_N_KEYS, k * PEER_UK:(k + 1) * PEER_UK]
            if k * n_i + rg in anchor_of:
                piece = piece + anchor_of[k * n_i + rg]
            pieces.append(piece)
        cols.append(jnp.concatenate(pieces, axis=0))
    h_t = _dot(jnp.concatenate(cols, axis=1), xt_ref[...])
    y_t = None
    for c in range(PEER_EB // PEER_KC):
        rs = slice(c * PEER_KC, (c + 1) * PEER_KC)
        hh = h_t[rs]
        act = hh * (1.0 + lax.erf(hh * (2.0 ** -0.5)))
        part = _dot(vt_ref[:, rs], act.astype(BF16) * g_ref[rs, :])
        y_t = part if y_t is None else y_t + part
    acc_ref[...] += y_t

    @pl.when(e == ne - 1)
    def _():
        o_ref[...] = x2_ref[...] + acc_ref[...].T


def _peer(xt, x2, wq_t, keys, u_bf16, vt_bf16):
    n = x2.shape[0]
    tb, eb = PEER_TB, PEER_EB
    return pl.pallas_call(
        _peer_kernel,
        grid=(n // tb, PEER_N_EXPERTS // eb),
        in_specs=[
            pl.BlockSpec((D_MODEL, tb), lambda i, e: (0, i)),
            pl.BlockSpec((tb, D_MODEL), lambda i, e: (i, 0)),
            pl.BlockSpec(wq_t.shape, lambda i, e: (0, 0)),
            pl.BlockSpec(keys.shape, lambda i, e: (0, 0, 0)),
            pl.BlockSpec((eb, D_MODEL), lambda i, e: (e, 0)),
            pl.BlockSpec((D_MODEL, eb), lambda i, e: (0, e)),
        ],
        out_specs=pl.BlockSpec((tb, D_MODEL), lambda i, e: (i, 0)),
        out_shape=jax.ShapeDtypeStruct((n, D_MODEL), F32),
        scratch_shapes=[
            pltpu.VMEM((wq_t.shape[0], tb), BF16),
            pltpu.VMEM((2 * PEER_HEADS, PEER_N_KEYS, tb), F32),
            pltpu.VMEM((2, PEER_HEADS, PEER_TOPK, tb), F32),
            pltpu.VMEM((2, PEER_TOPK, PEER_HEADS, tb), F32),
            pltpu.VMEM((PEER_TOPK, PEER_HEADS, tb), F32),
            pltpu.VMEM((PEER_HEADS, PEER_N_KEYS, tb), F32),
            pltpu.VMEM((PEER_HEADS, PEER_N_KEYS, tb), F32),
            pltpu.VMEM((PEER_HEADS, PEER_N_KEYS, tb), BF16),
            pltpu.VMEM((PEER_HEADS, PEER_N_KEYS, tb), BF16),
            pltpu.VMEM((eb, tb), BF16),
            pltpu.VMEM((D_MODEL, tb), F32),
        ],
        compiler_params=pltpu.CompilerParams(
            dimension_semantics=("parallel", "arbitrary"), vmem_limit_bytes=V7X_VMEM_LIMIT),
        name="peer",
    )(xt, x2, wq_t, keys, u_bf16, vt_bf16)


def _trunk(x, p):
    batch, seq, _ = x.shape
    x2d = x.reshape(batch * seq, D_MODEL)
    z, qkv = _in_proj(x2d, p["norm_mix"], p["w_in"], p["qk_gain"], p["head_red"], p["head_exp"])
    o_f = _hgrn_pass(z, p["lb"], batch, seq, reverse=False)
    o_a = _hgrn_pass(z, p["lb"], batch, seq, reverse=True, o_fwd=o_f, gain=p["hg_out_norm"])
    o_b = _na(qkv, p["na_bias"], batch, seq)
    x2, xt = _merge(x2d, o_a, o_b, z, p["w_proj_a"], p["w_proj_b"], p["w_out"], p["norm_ffn"])
    y = _peer(xt, x2, p["wq_t"], p["keys"], p["expert_u"], p["expert_vt"])
    return y.reshape(batch, seq, D_MODEL)


def kernel(x_prompt, x_sample, norm_mix, w_in, lb_logits, hg_out_norm, q_norm, k_norm, rel_pos_bias,
           w_proj_a, w_proj_b, w_out, norm_ffn, w_query, sub_keys, expert_u, expert_v):
    l = 0
    lb_all = jnp.cumsum(jax.nn.softmax(lb_logits.astype(F32), axis=0), axis=0)
    head_id = np.arange(2 * NA_WIDTH) // NA_HEAD_DIM
    p = {
        "norm_mix": norm_mix[l].reshape(1, D_MODEL),
        "w_in": w_in[l].astype(BF16),
        "qk_gain": jnp.concatenate([jnp.tile(q_norm[l], NA_HEADS), jnp.tile(k_norm[l], NA_HEADS)]).reshape(1, -1),
        "head_red": jnp.asarray(head_id[:, None] == np.arange(V7X_LANES)[None, :], BF16),
        "head_exp": jnp.asarray(np.arange(V7X_LANES)[:, None] == head_id[None, :], BF16),
        "lb": lb_all[l].reshape(2, 1, HG_WIDTH),
        "hg_out_norm": hg_out_norm[l].reshape(1, HG_DV),
        "na_bias": _na_bias_table(rel_pos_bias[l]),
        "w_proj_a": w_proj_a[l].astype(BF16),
        "w_proj_b": w_proj_b[l].astype(BF16),
        "w_out": w_out[l].astype(BF16),
        "norm_ffn": norm_ffn[l].reshape(1, D_MODEL),
        "wq_t": w_query[l].T.astype(BF16),
        "keys": sub_keys[l].reshape(2 * PEER_HEADS, PEER_N_KEYS, PEER_D_HALF).astype(BF16),
        "expert_u": expert_u[l].astype(BF16),
        "expert_vt": expert_v[l].T.astype(BF16),
    }
    return (_trunk(x_prompt, p), _trunk(x_sample, p))
```

```python
import functools

import numpy as np
import jax
import jax.numpy as jnp
from jax import lax
from jax.experimental import pallas as pl
from jax.experimental.pallas import tpu as pltpu

F32 = jnp.float32
BF16 = jnp.bfloat16

D_MODEL = 1024
GRID_W = 64
EPS = 1e-6
HG_HEADS = 4
HG_DK = 128
HG_DV = 128
HG_WIDTH = HG_HEADS * HG_DV
HG_CHUNK = 64
HG_SUB = 16
HG_LEVELS = (32, 16)
assert HG_LEVELS[0] * 2 == HG_CHUNK and HG_LEVELS[-1] == HG_SUB
NA_HEADS = 8
NA_HEAD_DIM = 64
NA_WIDTH = NA_HEADS * NA_HEAD_DIM
NA_WIN_R = 8
NA_WIN_C = 16
PEER_HEADS = 8
PEER_N_KEYS = 128
PEER_N_EXPERTS = PEER_N_KEYS * PEER_N_KEYS
PEER_D_HALF = 128
PEER_TOPK = 16
D_IN = 3 * NA_WIDTH + 3 * 2 * HG_HEADS * HG_DK + HG_WIDTH + 2 * D_MODEL

V7X_LANES = 128
V7X_SUBLANES = 8

_COL_HG_Q = (3 * NA_WIDTH) // V7X_LANES
_COL_HG_F = _COL_HG_Q + 8
_COL_HG_I = _COL_HG_F + 8
_COL_HG_G = _COL_HG_I + 8

V7X_VMEM_LIMIT = 56 * 1024 * 1024
NEG_BIG = -1e30

_CAND_PAIRS = tuple((p, q) for p in range(PEER_TOPK) for q in range(PEER_TOPK)
                    if (p + 1) * (q + 1) <= PEER_TOPK)


def _sorting_network(n):
    def merge(lo, hi, r):
        step = r * 2
        if step < hi - lo:
            yield from merge(lo, hi, step)
            yield from merge(lo + r, hi, step)
            yield from ((i, i + r) for i in range(lo + r, hi - r, step))
        else:
            yield (lo, lo + r)

    def sort(lo, hi):
        if hi - lo >= 1:
            mid = lo + (hi - lo) // 2
            yield from sort(lo, mid)
            yield from sort(mid + 1, hi)
            yield from merge(lo, hi, 1)

    return tuple(sort(0, n - 1))


_SORT16 = _sorting_network(PEER_TOPK)
_BITONIC16 = tuple((i, i + d) for d in (8, 4, 2, 1) for i in range(PEER_TOPK) if not i & d)


def _dot(a, b):
    return jnp.dot(a, b, preferred_element_type=F32)


def _dot_nt(a, b):
    return lax.dot_general(a, b, (((1,), (1,)), ((), ())), preferred_element_type=F32)


def _dot_tn(a, b):
    return lax.dot_general(a, b, (((0,), (0,)), ((), ())), preferred_element_type=F32)


def _split_bf16(x):
    hi = x.astype(BF16)
    lo = (x - hi.astype(F32)).astype(BF16)
    return hi, lo


IN_TM = 256
IN_TN = 1024


def _in_proj_kernel(x_ref, g_ref, w_ref, qkg_ref, red_ref, exp_ref, z_ref, qkv_ref):
    x = x_ref[...]
    ms = jnp.mean(x * x, axis=-1, keepdims=True)
    xn = (x * lax.rsqrt(ms + EPS) * g_ref[...]).astype(BF16)
    for j in range(D_IN // IN_TN):
        cs = slice(j * IN_TN, (j + 1) * IN_TN)
        z = _dot(xn, w_ref[:, cs])
        if j == 0:
            hi, lo = _split_bf16(z * z)
            hi, lo = _split_bf16(_dot(hi, red_ref[...]) + _dot(lo, red_ref[...]))
            ss = _dot(hi, exp_ref[...]) + _dot(lo, exp_ref[...])
            z = z * lax.rsqrt(ss * (1.0 / NA_HEAD_DIM) + EPS) * qkg_ref[...]
        z_ref[:, cs] = z
        lo_c, hi_c = j * IN_TN, min((j + 1) * IN_TN, 3 * NA_WIDTH)
        if hi_c > lo_c:
            qkv_ref[:, lo_c:hi_c] = z[:, :hi_c - lo_c].astype(BF16)


def _in_proj(x2d, gain, w_bf16, qk_gain, head_red, head_exp):
    n = x2d.shape[0]
    const = lambda shape: pl.BlockSpec(shape, lambda i: (0, 0), pipeline_mode=pl.Buffered(1))
    return pl.pallas_call(
        _in_proj_kernel,
        grid=(n // IN_TM,),
        in_specs=[
            pl.BlockSpec((IN_TM, D_MODEL), lambda i: (i, 0)),
            const((1, D_MODEL)),
            const((D_MODEL, D_IN)),
            const((1, IN_TN)),
            const(head_red.shape),
            const(head_exp.shape),
        ],
        out_specs=[pl.BlockSpec((IN_TM, D_IN), lambda i: (i, 0)),
                   pl.BlockSpec((IN_TM, 3 * NA_WIDTH), lambda i: (i, 0))],
        out_shape=[jax.ShapeDtypeStruct((n, D_IN), F32), jax.ShapeDtypeStruct((n, 3 * NA_WIDTH), BF16)],
        compiler_params=pltpu.CompilerParams(
            dimension_semantics=("parallel",), vmem_limit_bytes=V7X_VMEM_LIMIT),
        name="in_proj",
    )(x2d, gain, w_bf16, qk_gain, head_red, head_exp)


HG_TT = 512


def _hgrn_chunk(q, f_logit, v, lb, states, tri, pair_mask, level_masks, ones, *, reverse):
    C, SB = HG_CHUNK, HG_SUB
    nblk = C // SB
    f = lb + (1.0 - lb) * jax.nn.sigmoid(f_logit)
    qs = q * jax.nn.sigmoid(q)
    hi, lo = _split_bf16(jnp.log(f))
    b = _dot(tri, hi) + _dot(tri, lo)
    c = b - jnp.log(1.0 - f)
    vb = v.astype(BF16)
    q_all = (qs * jnp.exp(b)).astype(BF16)
    b_end = b[0:1] if reverse else b[C - 1:C]
    k_d = jnp.exp(b_end - c).astype(BF16)
    carry_decay = jnp.exp(b_end)

    def boundary_refs(size):
        q_ref, k_ref = [], []
        for g0 in range(0, C, 2 * size):
            first, second = slice(g0, g0 + size), slice(g0 + size, g0 + 2 * size)
            if reverse:
                bound = jnp.broadcast_to(b[g0 + size:g0 + size + 1], (size, b.shape[1]))
                q_ref += [bound, b[second]]
                k_ref += [c[first], bound]
            else:
                bound = jnp.broadcast_to(b[g0 + size - 1:g0 + size], (size, b.shape[1]))
                q_ref += [b[first], bound]
                k_ref += [bound, c[second]]
        return jnp.concatenate(q_ref, axis=0), jnp.concatenate(k_ref, axis=0)

    level_q, level_k = [], []
    for size in HG_LEVELS:
        q_bound, k_bound = boundary_refs(size)
        level_q.append((qs * jnp.exp(b - q_bound)).astype(BF16))
        level_k.append(jnp.exp(k_bound - c).astype(BF16))

    outs, new_states = [], []
    for h, state in enumerate(states):
        cs = slice(h * HG_DK, (h + 1) * HG_DK)
        prods = []
        for blk in range(nblk):
            rows = slice(blk * SB, (blk + 1) * SB)
            prod = (qs[rows, cs][None, :, :] * jnp.exp(b[rows, cs][None, :, :] - c[rows, cs][:, None, :]))
            prods.append(jnp.where(pair_mask, prod, 0.0).reshape(SB * SB, HG_DK))
        rs = _dot(jnp.concatenate(prods, axis=0).astype(BF16), ones)
        rs = rs.reshape(nblk, SB, SB, HG_DV)
        o_diag = [jnp.sum(rs[blk] * v[blk * SB:(blk + 1) * SB, cs][:, None, :], axis=0) for blk in range(nblk)]
        s_off = sum(jnp.where(m, _dot_nt(lq[:, cs], lk[:, cs]), 0.0)
                    for m, lq, lk in zip(level_masks, level_q, level_k))
        o_off = _dot(s_off.astype(BF16), vb[:, cs])
        o_inter = _dot_nt(q_all[:, cs], state.astype(BF16))
        outs.append(o_inter + o_off + jnp.concatenate(o_diag, axis=0))
        new_states.append(state * carry_decay[:, cs] + _dot_tn(vb[:, cs], k_d[:, cs]))
    return outs, new_states


def _hgrn_kernel(*refs, reverse, finish):
    if finish:
        q_ref, f_ref, v_ref, lb_ref, of_ref, g_ref, gain_ref, o_ref, st_ref = refs
    else:
        q_ref, f_ref, v_ref, lb_ref, o_ref, st_ref = refs
    C, SB = HG_CHUNK, HG_SUB
    nchunk = HG_TT // C

    @pl.when(pl.program_id(1) == 0)
    def _():
        st_ref[...] = jnp.zeros_like(st_ref)

    row = lax.broadcasted_iota(jnp.int32, (C, C), 0)
    col = lax.broadcasted_iota(jnp.int32, (C, C), 1)
    tri = ((col >= row) if reverse else (col <= row)).astype(BF16)
    s_i = lax.broadcasted_iota(jnp.int32, (SB, SB, HG_DK), 0)
    t_i = lax.broadcasted_iota(jnp.int32, (SB, SB, HG_DK), 1)
    pair_mask = (s_i >= t_i) if reverse else (s_i <= t_i)
    ones = jnp.ones((HG_DK, HG_DV), BF16)
    level_masks = []
    for size in HG_LEVELS:
        tg, sg = row // size, col // size
        later, earlier = (0, 1) if reverse else (1, 0)
        level_masks.append((tg // 2 == sg // 2) & (tg % 2 == later) & (sg % 2 == earlier))

    def chunk(ci, carry):
        c = (nchunk - 1 - ci) if reverse else ci
        sl = pl.ds(pl.multiple_of(c * C, C), C)
        outs, states = _hgrn_chunk(q_ref[sl, :], f_ref[sl, :], v_ref[sl, :], lb_ref[0],
                                   [st_ref[h] for h in range(HG_HEADS)], tri, pair_mask, level_masks, ones,
                                   reverse=reverse)
        for h in range(HG_HEADS):
            cs = slice(h * HG_DK, (h + 1) * HG_DK)
            o, st_ref[h] = outs[h], states[h]
            if finish:
                tot = of_ref[sl, cs] + o
                ms = jnp.mean(tot * tot, axis=-1, keepdims=True)
                g = g_ref[sl, cs]
                y = tot * lax.rsqrt(ms + EPS) * gain_ref[...] * (g * jax.nn.sigmoid(g))
                o_ref[sl, cs] = y.astype(o_ref.dtype)
            else:
                o_ref[sl, cs] = o
        return carry

    lax.fori_loop(0, nchunk, chunk, 0, unroll=4)


def _hgrn_pass(z, lb, batch, seq, *, reverse, o_fwd=None, gain=None):
    n = z.shape[0]
    nt = seq // HG_TT
    finish = o_fwd is not None
    d = 1 if reverse else 0

    def tmap(t):
        return (nt - 1 - t) if reverse else t

    def zspec(col0, per_dir=True):
        blk = col0 // HG_HEADS + (d if per_dir else 0)
        return pl.BlockSpec((HG_TT, HG_WIDTH), lambda b, t: (b * nt + tmap(t), blk))

    ospec = pl.BlockSpec((HG_TT, HG_WIDTH), lambda b, t: (b * nt + tmap(t), 0))
    in_specs = [zspec(_COL_HG_Q), zspec(_COL_HG_F), zspec(_COL_HG_I),
                pl.BlockSpec((1, 1, HG_WIDTH), lambda b, t: (d, 0, 0))]
    args = [z, z, z, lb]
    if finish:
        in_specs += [ospec, zspec(_COL_HG_G, per_dir=False), pl.BlockSpec((1, HG_DV), lambda b, t: (0, 0))]
        args += [o_fwd, z, gain]
    return pl.pallas_call(
        functools.partial(_hgrn_kernel, reverse=reverse, finish=finish),
        grid=(batch, nt),
        in_specs=in_specs,
        out_specs=ospec,
        out_shape=jax.ShapeDtypeStruct((n, HG_WIDTH), BF16 if finish else F32),
        scratch_shapes=[pltpu.VMEM((HG_HEADS, HG_DV, HG_DK), F32)],
        compiler_params=pltpu.CompilerParams(
            dimension_semantics=("parallel", "arbitrary"),
            vmem_limit_bytes=V7X_VMEM_LIMIT),
        name="hgrn_bwd" if reverse else "hgrn_fwd",
    )(*args)


NA_BAND = 8
NA_BT = NA_BAND * GRID_W


def _na_bias_table(rpb):
    c = jnp.arange(GRID_W)
    c0 = jnp.clip(c - NA_WIN_C // 2, 0, GRID_W - NA_WIN_C)
    kc = jnp.arange(GRID_W)
    valid = (kc[None, :] >= c0[:, None]) & (kc[None, :] < c0[:, None] + NA_WIN_C)
    off = jnp.clip(kc[None, :] - c[:, None] + NA_WIN_C - 1, 0, 2 * NA_WIN_C - 2)
    t = jnp.where(valid[None, None], rpb.astype(F32)[:, :, off], NEG_BIG)
    t = t.reshape(NA_HEADS // 2, 2, 2 * NA_WIN_R - 1, GRID_W, GRID_W)
    t = t.transpose(0, 2, 4, 1, 3)
    return t.reshape(NA_HEADS // 2, (2 * NA_WIN_R - 1) * GRID_W, 2 * GRID_W)


def _na_kernel(q_ref, k0_ref, k1_ref, k2_ref, v0_ref, v1_ref, v2_ref, bias_ref, o_ref,
               kcat_ref, vcat_ref, *, rows):
    g = pl.program_id(1)
    for i, (kr, vr) in enumerate(((k0_ref, v0_ref), (k1_ref, v1_ref), (k2_ref, v2_ref))):
        kcat_ref[i * NA_BT:(i + 1) * NA_BT, :] = kr[...]
        vcat_ref[i * NA_BT:(i + 1) * NA_BT, :] = vr[...]
    lane = lax.broadcasted_iota(jnp.int32, (1, 2 * NA_HEAD_DIM), 1)
    left = lane < NA_HEAD_DIM
    nkeys = NA_WIN_R * GRID_W
    scale = NA_HEAD_DIM ** -0.5
    pairs = range(NA_HEADS // 2)

    def body(j, carry):
        r = g * NA_BAND + j
        r0 = jnp.clip(r - NA_WIN_R // 2, 0, rows - NA_WIN_R)
        koff = pl.multiple_of((r0 - g * NA_BAND + NA_BAND) * GRID_W, GRID_W)
        boff = pl.multiple_of((r0 - r + NA_WIN_R - 1) * GRID_W, GRID_W)
        qsl = pl.ds(pl.multiple_of(j * GRID_W, GRID_W), GRID_W)
        qb = q_ref[qsl, :] * scale
        cols = [slice(hp * 2 * NA_HEAD_DIM, (hp + 1) * 2 * NA_HEAD_DIM) for hp in pairs]
        s_t = []
        for cs in cols:
            q2 = qb[:, cs]
            zero = jnp.zeros_like(q2)
            rhs_t = jnp.concatenate([jnp.where(left, q2, zero), jnp.where(left, zero, q2)], axis=0)
            s_t.append(_dot_nt(kcat_ref[pl.ds(koff, nkeys), cs], rhs_t))
        s_t = [s + bias_ref[hp, pl.ds(boff, nkeys), :] for hp, s in zip(pairs, s_t)]
        p = [jnp.exp(s - jnp.max(s, axis=0, keepdims=True)) for s in s_t]
        p = [(x * (1.0 / jnp.sum(x, axis=0, keepdims=True))).astype(BF16) for x in p]
        o2 = [_dot_tn(x, vcat_ref[pl.ds(koff, nkeys), cs]) for x, cs in zip(p, cols)]
        for cs, o in zip(cols, o2):
            o_ref[qsl, cs] = jnp.where(left, o[:GRID_W], o[GRID_W:]).astype(o_ref.dtype)
        return carry

    lax.fori_loop(0, NA_BAND, body, 0, unroll=8)


def _na(z, bias_tbl, batch, seq):
    n = z.shape[0]
    rows = seq // GRID_W
    nb = rows // NA_BAND

    def kv(colblk, shift):
        return pl.BlockSpec(
            (NA_BT, NA_WIDTH),
            lambda b, g: (b * nb + jnp.clip(g + shift, 0, nb - 1), colblk))

    return pl.pallas_call(
        functools.partial(_na_kernel, rows=rows),
        grid=(batch, nb),
        in_specs=[pl.BlockSpec((NA_BT, NA_WIDTH), lambda b, g: (b * nb + g, 0)),
                  kv(1, -1), kv(1, 0), kv(1, 1), kv(2, -1), kv(2, 0), kv(2, 1),
                  pl.BlockSpec(bias_tbl.shape, lambda b, g: (0, 0, 0))],
        out_specs=pl.BlockSpec((NA_BT, NA_WIDTH), lambda b, g: (b * nb + g, 0)),
        out_shape=jax.ShapeDtypeStruct((n, NA_WIDTH), BF16),
        scratch_shapes=[pltpu.VMEM((3 * NA_BT, NA_WIDTH), BF16),
                        pltpu.VMEM((3 * NA_BT, NA_WIDTH), BF16)],
        compiler_params=pltpu.CompilerParams(
            dimension_semantics=("parallel", "arbitrary"), vmem_limit_bytes=V7X_VMEM_LIMIT),
        name="natten",
    )(z, z, z, z, z, z, z, bias_tbl)


MG_TM = 512


def _merge_kernel(x_ref, oa_ref, ob_ref, ga_ref, gb_ref, wa_ref, wb_ref, wo_ref, nf_ref,
                  x2_ref, xt_ref):
    a = _dot(oa_ref[...], wa_ref[...])
    b = _dot(ob_ref[...], wb_ref[...])
    mix = jax.nn.sigmoid(ga_ref[...]) * a + jax.nn.sigmoid(gb_ref[...]) * b
    x2 = x_ref[...] + _dot(mix.astype(BF16), wo_ref[...])
    x2_ref[...] = x2
    ms = jnp.mean(x2 * x2, axis=-1, keepdims=True)
    xt_ref[...] = (x2 * lax.rsqrt(ms + EPS) * nf_ref[...]).T.astype(BF16)


def _merge(x2d, o_a, o_b, z, wa, wb, wo, norm_ffn):
    n = x2d.shape[0]
    col_ga = (_COL_HG_G * V7X_LANES + HG_WIDTH) // D_MODEL
    full = lambda shape: pl.BlockSpec(shape, lambda i: (0, 0))
    tok = lambda w, c=0: pl.BlockSpec((MG_TM, w), lambda i: (i, c))
    return pl.pallas_call(
        _merge_kernel,
        grid=(n // MG_TM,),
        in_specs=[tok(D_MODEL), tok(HG_WIDTH), tok(NA_WIDTH), tok(D_MODEL, col_ga), tok(D_MODEL, col_ga + 1),
                  full((HG_WIDTH, D_MODEL)), full((NA_WIDTH, D_MODEL)), full((D_MODEL, D_MODEL)),
                  full((1, D_MODEL))],
        out_specs=[tok(D_MODEL), pl.BlockSpec((D_MODEL, MG_TM), lambda i: (0, i))],
        out_shape=[jax.ShapeDtypeStruct((n, D_MODEL), F32), jax.ShapeDtypeStruct((D_MODEL, n), BF16)],
        compiler_params=pltpu.CompilerParams(
            dimension_semantics=("parallel",), vmem_limit_bytes=V7X_VMEM_LIMIT),
        name="merge",
    )(x2d, o_a, o_b, z, z, wa, wb, wo, norm_ffn)


PEER_TB = 512
PEER_EB = 1024
PEER_LC = 256
PEER_KC = 256
PEER_UK = 256
PEER_FIRST_SLOT = 2


def _peer_route(xt_ref, wq_ref, keys_ref, qt_ref, s_ref, toph_ref, top_ref, theta_ref,
                cnt_ref, coef_ref, r1_ref, e1_ref):
    K = PEER_TOPK
    half_rows = wq_ref.shape[0] // 2
    for part in range(2):
        rs = slice(part * half_rows, (part + 1) * half_rows)
        qt_ref[rs, :] = _dot(wq_ref[rs, :], xt_ref[...]).astype(BF16)

    for hp in range(2 * PEER_HEADS):
        s_ref[hp] = _dot(keys_ref[hp], qt_ref[hp * PEER_D_HALF:(hp + 1) * PEER_D_HALF, :])

    n_grp = PEER_N_KEYS // V7X_SUBLANES
    assert n_grp == K

    def exchange(rows, i, j):
        rows[i], rows[j] = jnp.maximum(rows[i], rows[j]), jnp.minimum(rows[i], rows[j])

    def extract(h, carry, *, half):
        for lc in range(xt_ref.shape[1] // V7X_LANES):
            ls = slice(lc * V7X_LANES, (lc + 1) * V7X_LANES)
            keys = [s_ref[2 * h + half, V7X_SUBLANES * r:V7X_SUBLANES * (r + 1), ls] for r in range(n_grp)]
            rows = list(keys)
            for i, j in _SORT16:
                exchange(rows, i, j)
            for shift in (4, 2, 1):
                rows = [jnp.maximum(rows[i], pltpu.roll(rows[K - 1 - i], shift, axis=0)) for i in range(K)]
                for i, j in _BITONIC16:
                    exchange(rows, i, j)
            toph_ref[half, h, :, ls] = jnp.concatenate([rows[q][0:1] for q in range(K)], axis=0)
            if half:
                for r in range(0, n_grp, 2):
                    rank = []
                    for k in keys[r:r + 2]:
                        rk = jnp.zeros_like(k)
                        for q in range(K):
                            rk = jnp.where(k < rows[q], float(q + 1), rk)
                        rank.append(rk)
                    r1_ref[h, V7X_SUBLANES * r:V7X_SUBLANES * (r + 2), ls] = (
                        jnp.concatenate(rank, axis=0).astype(BF16))
        return carry

    lax.fori_loop(0, PEER_HEADS, functools.partial(extract, half=0), 0)
    lax.fori_loop(0, PEER_HEADS, functools.partial(extract, half=1), 0)
    for half in range(2):
        for q in range(K):
            top_ref[half, q] = jnp.concatenate(
                [toph_ref[half, h, q:q + 1, :] for h in range(PEER_HEADS)], axis=0)

    small = top_ref.shape[2:]
    neg = jnp.full(small, -jnp.inf, F32)
    lists = [[top_ref[0, p] + top_ref[1, q] if (p + 1) * (q + 1) <= K else neg for q in range(K)]
             for p in range(K // 2)]
    lists.append([top_ref[0, p] + top_ref[1, 0] if p < K else neg for p in range(K // 2, K + K // 2)])
    assert sum(x is not neg for lst in lists for x in lst) == len(_CAND_PAIRS)
    merged = lists[0]
    for other in lists[1:]:
        merged = [jnp.maximum(merged[i], other[K - 1 - i]) for i in range(K)]
        for i, j in _BITONIC16:
            exchange(merged, i, j)
    tau = merged[K - 1]
    best = top_ref[0, 0] + top_ref[1, 0]
    zsum = jnp.zeros(small, F32)
    for p, q in _CAND_PAIRS:
        v = top_ref[0, p] + top_ref[1, q]
        zsum = zsum + jnp.where(v >= tau, jnp.exp(v - best), 0.0)
    inv_z = 1.0 / zsum

    for q in range(K):
        th = jnp.full(small, jnp.inf, F32)
        b_q = top_ref[1, q]
        for p in range(K):
            a_p = top_ref[0, p]
            th = jnp.minimum(th, jnp.where(a_p + b_q >= tau, a_p, jnp.inf))
        theta_ref[q] = th

    for h in range(PEER_HEADS):
        s0 = s_ref[2 * h]
        cnt = jnp.zeros(s0.shape, F32)
        for q in range(K):
            cnt = jnp.where(s0 >= theta_ref[q, h:h + 1, :], float(q + 1), cnt)
        cnt_ref[h] = cnt
        coef_ref[h] = jnp.exp(s0 - top_ref[0, 0, h:h + 1, :])
        e1_ref[h] = (jnp.exp(s_ref[2 * h + 1] - top_ref[1, 0, h:h + 1, :]) * (0.5 * inv_z[h:h + 1])).astype(BF16)


def _peer_kernel(xt_ref, x2_ref, wq_ref, keys_ref, u_ref, vt_ref, o_ref,
                 qt_ref, s_ref, toph_ref, top_ref, theta_ref,
                 cnt_ref, coef_ref, r1_ref, e1_ref, g_ref, acc_ref):
    e = pl.program_id(1)
    ne = pl.num_programs(1)

    @pl.when(e == 0)
    def _():
        _peer_route(xt_ref, wq_ref, keys_ref, qt_ref, s_ref, toph_ref, top_ref, theta_ref,
                    cnt_ref, coef_ref, r1_ref, e1_ref)
        acc_ref[...] = jnp.zeros_like(acc_ref)

    n_i = PEER_EB // PEER_N_KEYS
    i_rows = pl.ds(pl.multiple_of(e * n_i, n_i), n_i)
    zero = jnp.zeros((PEER_N_KEYS, PEER_LC), BF16)
    n_lc = PEER_TB // PEER_LC
    cnt_rows = [[cnt_ref[h, i_rows, lc * PEER_LC:(lc + 1) * PEER_LC].astype(BF16) for h in range(PEER_HEADS)]
                for lc in range(n_lc)]
    coef_rows = [[coef_ref[h, i_rows, lc * PEER_LC:(lc + 1) * PEER_LC].astype(BF16) for h in range(PEER_HEADS)]
                 for lc in range(n_lc)]
    anchors = []
    for ii in range(n_i):
        for lc in range(n_lc):
            ls = slice(lc * PEER_LC, (lc + 1) * PEER_LC)
            gate = zero
            for h in range(PEER_HEADS):
                sel = r1_ref[h, :, ls] < cnt_rows[lc][h][ii:ii + 1]
                gate = gate + jnp.where(sel, e1_ref[h, :, ls], zero) * coef_rows[lc][h][ii:ii + 1]
            g_ref[ii * PEER_N_KEYS:(ii + 1) * PEER_N_KEYS, ls] = gate
            bits = pltpu.bitcast(gate, jnp.uint32)
            anchors.append(pltpu.bitcast((bits >> 16) >> 16, BF16))
    anchors = anchors[:len(anchors) - n_lc * PEER_KC // PEER_N_KEYS]

    n_k = D_MODEL // PEER_UK
    free_slots = n_k * n_i - PEER_FIRST_SLOT
    anchor_of = {PEER_FIRST_SLOT + t * free_slots // len(anchors): a for t, a in enumerate(anchors)}
    cols = []
    for k in range(n_k):
        pieces = []
        for rg in range(n_i):
            piece = u_ref[rg * PEER_N_KEYS:(rg + 1) * PEER_N_KEYS, k * PEER_UK:(k + 1) * PEER_UK]
            if k * n_i + rg in anchor_of:
                piece = piece + anchor_of[k * n_i + rg]
            pieces.append(piece)
        cols.append(jnp.concatenate(pieces, axis=0))
    h_t = _dot(jnp.concatenate(cols, axis=1), xt_ref[...])
    y_t = None
    for c in range(PEER_EB // PEER_KC):
        rs = slice(c * PEER_KC, (c + 1) * PEER_KC)
        hh = h_t[rs]
        act = hh * (1.0 + lax.erf(hh * (2.0 ** -0.5)))
        part = _dot(vt_ref[:, rs], act.astype(BF16) * g_ref[rs, :])
        y_t = part if y_t is None else y_t + part
    acc_ref[...] += y_t

    @pl.when(e == ne - 1)
    def _():
        o_ref[...] = x2_ref[...] + acc_ref[...].T


def _peer(xt, x2, wq_t, keys, u_bf16, vt_bf16):
    n = x2.shape[0]
    tb, eb = PEER_TB, PEER_EB
    return pl.pallas_call(
        _peer_kernel,
        grid=(n // tb, PEER_N_EXPERTS // eb),
        in_specs=[
            pl.BlockSpec((D_MODEL, tb), lambda i, e: (0, i)),
            pl.BlockSpec((tb, D_MODEL), lambda i, e: (i, 0)),
            pl.BlockSpec(wq_t.shape, lambda i, e: (0, 0)),
            pl.BlockSpec(keys.shape, lambda i, e: (0, 0, 0)),
            pl.BlockSpec((eb, D_MODEL), lambda i, e: (e, 0)),
            pl.BlockSpec((D_MODEL, eb), lambda i, e: (0, e)),
        ],
        out_specs=pl.BlockSpec((tb, D_MODEL), lambda i, e: (i, 0)),
        out_shape=jax.ShapeDtypeStruct((n, D_MODEL), F32),
        scratch_shapes=[
            pltpu.VMEM((wq_t.shape[0], tb), BF16),
            pltpu.VMEM((2 * PEER_HEADS, PEER_N_KEYS, tb), F32),
            pltpu.VMEM((2, PEER_HEADS, PEER_TOPK, tb), F32),
            pltpu.VMEM((2, PEER_TOPK, PEER_HEADS, tb), F32),
            pltpu.VMEM((PEER_TOPK, PEER_HEADS, tb), F32),
            pltpu.VMEM((PEER_HEADS, PEER_N_KEYS, tb), F32),
            pltpu.VMEM((PEER_HEADS, PEER_N_KEYS, tb), F32),
            pltpu.VMEM((PEER_HEADS, PEER_N_KEYS, tb), BF16),
            pltpu.VMEM((PEER_HEADS, PEER_N_KEYS, tb), BF16),
            pltpu.VMEM((eb, tb), BF16),
            pltpu.VMEM((D_MODEL, tb), F32),
        ],
        compiler_params=pltpu.CompilerParams(
            dimension_semantics=("parallel", "arbitrary"), vmem_limit_bytes=V7X_VMEM_LIMIT),
        name="peer",
    )(xt, x2, wq_t, keys, u_bf16, vt_bf16)


def _trunk(x, p):
    batch, seq, _ = x.shape
    x2d = x.reshape(batch * seq, D_MODEL)
    z, qkv = _in_proj(x2d, p["norm_mix"], p["w_in"], p["qk_gain"], p["head_red"], p["head_exp"])
    o_f = _hgrn_pass(z, p["lb"], batch, seq, reverse=False)
    o_a = _hgrn_pass(z, p["lb"], batch, seq, reverse=True, o_fwd=o_f, gain=p["hg_out_norm"])
    o_b = _na(qkv, p["na_bias"], batch, seq)
    x2, xt = _merge(x2d, o_a, o_b, z, p["w_proj_a"], p["w_proj_b"], p["w_out"], p["norm_ffn"])
    y = _peer(xt, x2, p["wq_t"], p["keys"], p["expert_u"], p["expert_vt"])
    return y.reshape(batch, seq, D_MODEL)


def kernel(x_prompt, x_sample, norm_mix, w_in, lb_logits, hg_out_norm, q_norm, k_norm, rel_pos_bias,
           w_proj_a, w_proj_b, w_out, norm_ffn, w_query, sub_keys, expert_u, expert_v):
    l = 0
    lb_all = jnp.cumsum(jax.nn.softmax(lb_logits.astype(F32), axis=0), axis=0)
    head_id = np.arange(2 * NA_WIDTH) // NA_HEAD_DIM
    p = {
        "norm_mix": norm_mix[l].reshape(1, D_MODEL),
        "w_in": w_in[l].astype(BF16),
        "qk_gain": jnp.concatenate([jnp.tile(q_norm[l], NA_HEADS), jnp.tile(k_norm[l], NA_HEADS)]).reshape(1, -1),
        "head_red": jnp.asarray(head_id[:, None] == np.arange(V7X_LANES)[None, :], BF16),
        "head_exp": jnp.asarray(np.arange(V7X_LANES)[:, None] == head_id[None, :], BF16),
        "lb": lb_all[l].reshape(2, 1, HG_WIDTH),
        "hg_out_norm": hg_out_norm[l].reshape(1, HG_DV),
        "na_bias": _na_bias_table(rel_pos_bias[l]),
        "w_proj_a": w_proj_a[l].astype(BF16),
        "w_proj_b": w_proj_b[l].astype(BF16),
        "w_out": w_out[l].astype(BF16),
        "norm_ffn": norm_ffn[l].reshape(1, D_MODEL),
        "wq_t": w_query[l].T.astype(BF16),
        "keys": sub_keys[l].reshape(2 * PEER_HEADS, PEER_N_KEYS, PEER_D_HALF).astype(BF16),
        "expert_u": expert_u[l].astype(BF16),
        "expert_vt": expert_v[l].T.astype(BF16),
    }
    return (_trunk(x_prompt, p), _trunk(x_sample, p))
```

```python
import functools

import numpy as np
import jax
import jax.numpy as jnp
from jax import lax
from jax.experimental import pallas as pl
from jax.experimental.pallas import tpu as pltpu

F32 = jnp.float32
BF16 = jnp.bfloat16

D_MODEL = 1024
GRID_W = 64
EPS = 1e-6
HG_HEADS = 4
HG_DK = 128
HG_DV = 128
HG_WIDTH = HG_HEADS * HG_DV
HG_CHUNK = 64
HG_SUB = 16
HG_LEVELS = (32, 16)
assert HG_LEVELS[0] * 2 == HG_CHUNK and HG_LEVELS[-1] == HG_SUB
NA_HEADS = 8
NA_HEAD_DIM = 64
NA_WIDTH = NA_HEADS * NA_HEAD_DIM
NA_WIN_R = 8
NA_WIN_C = 16
PEER_HEADS = 8
PEER_N_KEYS = 128
PEER_N_EXPERTS = PEER_N_KEYS * PEER_N_KEYS
PEER_D_HALF = 128
PEER_TOPK = 16
D_IN = 3 * NA_WIDTH + 3 * 2 * HG_HEADS * HG_DK + HG_WIDTH + 2 * D_MODEL

V7X_LANES = 128
V7X_SUBLANES = 8

_COL_HG_Q = (3 * NA_WIDTH) // V7X_LANES
_COL_HG_F = _COL_HG_Q + 8
_COL_HG_I = _COL_HG_F + 8
_COL_HG_G = _COL_HG_I + 8

V7X_VMEM_LIMIT = 56 * 1024 * 1024
NEG_BIG = -1e30

_CAND_PAIRS = tuple((p, q) for p in range(PEER_TOPK) for q in range(PEER_TOPK)
                    if (p + 1) * (q + 1) <= PEER_TOPK)


def _sorting_network(n):
    def merge(lo, hi, r):
        step = r * 2
        if step < hi - lo:
            yield from merge(lo, hi, step)
            yield from merge(lo + r, hi, step)
            yield from ((i, i + r) for i in range(lo + r, hi - r, step))
        else:
            yield (lo, lo + r)

    def sort(lo, hi):
        if hi - lo >= 1:
            mid = lo + (hi - lo) // 2
            yield from sort(lo, mid)
            yield from sort(mid + 1, hi)
            yield from merge(lo, hi, 1)

    return tuple(sort(0, n - 1))


_SORT16 = _sorting_network(PEER_TOPK)
_BITONIC16 = tuple((i, i + d) for d in (8, 4, 2, 1) for i in range(PEER_TOPK) if not i & d)


def _dot(a, b):
    return jnp.dot(a, b, preferred_element_type=F32)


def _dot_nt(a, b):
    return lax.dot_general(a, b, (((1,), (1,)), ((), ())), preferred_element_type=F32)


def _dot_tn(a, b):
    return lax.dot_general(a, b, (((0,), (0,)), ((), ())), preferred_element_type=F32)


def _split_bf16(x):
    hi = x.astype(BF16)
    lo = (x - hi.astype(F32)).astype(BF16)
    return hi, lo


IN_TM = 256
IN_TN = 1024


def _in_proj_kernel(x_ref, g_ref, w_ref, qkg_ref, red_ref, exp_ref, z_ref, qkv_ref):
    x = x_ref[...]
    ms = jnp.mean(x * x, axis=-1, keepdims=True)
    xn = (x * lax.rsqrt(ms + EPS) * g_ref[...]).astype(BF16)
    for j in range(D_IN // IN_TN):
        cs = slice(j * IN_TN, (j + 1) * IN_TN)
        z = _dot(xn, w_ref[:, cs])
        if j == 0:
            hi, lo = _split_bf16(z * z)
            hi, lo = _split_bf16(_dot(hi, red_ref[...]) + _dot(lo, red_ref[...]))
            ss = _dot(hi, exp_ref[...]) + _dot(lo, exp_ref[...])
            z = z * lax.rsqrt(ss * (1.0 / NA_HEAD_DIM) + EPS) * qkg_ref[...]
        z_ref[:, cs] = z
        lo_c, hi_c = j * IN_TN, min((j + 1) * IN_TN, 3 * NA_WIDTH)
        if hi_c > lo_c:
            qkv_ref[:, lo_c:hi_c] = z[:, :hi_c - lo_c].astype(BF16)


def _in_proj(x2d, gain, w_bf16, qk_gain, head_red, head_exp):
    n = x2d.shape[0]
    const = lambda shape: pl.BlockSpec(shape, lambda i: (0, 0), pipeline_mode=pl.Buffered(1))
    return pl.pallas_call(
        _in_proj_kernel,
        grid=(n // IN_TM,),
        in_specs=[
            pl.BlockSpec((IN_TM, D_MODEL), lambda i: (i, 0)),
            const((1, D_MODEL)),
            const((D_MODEL, D_IN)),
            const((1, IN_TN)),
            const(head_red.shape),
            const(head_exp.shape),
        ],
        out_specs=[pl.BlockSpec((IN_TM, D_IN), lambda i: (i, 0)),
                   pl.BlockSpec((IN_TM, 3 * NA_WIDTH), lambda i: (i, 0))],
        out_shape=[jax.ShapeDtypeStruct((n, D_IN), F32), jax.ShapeDtypeStruct((n, 3 * NA_WIDTH), BF16)],
        compiler_params=pltpu.CompilerParams(
            dimension_semantics=("parallel",), vmem_limit_bytes=V7X_VMEM_LIMIT),
        name="in_proj",
    )(x2d, gain, w_bf16, qk_gain, head_red, head_exp)


HG_TT = 512


def _hgrn_chunk(q, f_logit, v, lb, states, tri, pair_mask, level_masks, ones, *, reverse):
    C, SB = HG_CHUNK, HG_SUB
    nblk = C // SB
    f = lb + (1.0 - lb) * jax.nn.sigmoid(f_logit)
    qs = q * jax.nn.sigmoid(q)
    hi, lo = _split_bf16(jnp.log(f))
    b = _dot(tri, hi) + _dot(tri, lo)
    c = b - jnp.log(1.0 - f)
    vb = v.astype(BF16)
    q_all = (qs * jnp.exp(b)).astype(BF16)
    b_end = b[0:1] if reverse else b[C - 1:C]
    k_d = jnp.exp(b_end - c).astype(BF16)
    carry_decay = jnp.exp(b_end)

    def boundary_refs(size):
        q_ref, k_ref = [], []
        for g0 in range(0, C, 2 * size):
            first, second = slice(g0, g0 + size), slice(g0 + size, g0 + 2 * size)
            if reverse:
                bound = jnp.broadcast_to(b[g0 + size:g0 + size + 1], (size, b.shape[1]))
                q_ref += [bound, b[second]]
                k_ref += [c[first], bound]
            else:
                bound = jnp.broadcast_to(b[g0 + size - 1:g0 + size], (size, b.shape[1]))
                q_ref += [b[first], bound]
                k_ref += [bound, c[second]]
        return jnp.concatenate(q_ref, axis=0), jnp.concatenate(k_ref, axis=0)

    level_q, level_k = [], []
    for size in HG_LEVELS:
        q_bound, k_bound = boundary_refs(size)
        level_q.append((qs * jnp.exp(b - q_bound)).astype(BF16))
        level_k.append(jnp.exp(k_bound - c).astype(BF16))

    outs, new_states = [], []
    for h, state in enumerate(states):
        cs = slice(h * HG_DK, (h + 1) * HG_DK)
        prods = []
        for blk in range(nblk):
            rows = slice(blk * SB, (blk + 1) * SB)
            prod = (qs[rows, cs][None, :, :] * jnp.exp(b[rows, cs][None, :, :] - c[rows, cs][:, None, :]))
            prods.append(jnp.where(pair_mask, prod, 0.0).reshape(SB * SB, HG_DK))
        rs = _dot(jnp.concatenate(prods, axis=0).astype(BF16), ones)
        rs = rs.reshape(nblk, SB, SB, HG_DV)
        o_diag = [jnp.sum(rs[blk] * v[blk * SB:(blk + 1) * SB, cs][:, None, :], axis=0) for blk in range(nblk)]
        s_off = sum(jnp.where(m, _dot_nt(lq[:, cs], lk[:, cs]), 0.0)
                    for m, lq, lk in zip(level_masks, level_q, level_k))
        o_off = _dot(s_off.astype(BF16), vb[:, cs])
        o_inter = _dot_nt(q_all[:, cs], state.astype(BF16))
        outs.append(o_inter + o_off + jnp.concatenate(o_diag, axis=0))
        new_states.append(state * carry_decay[:, cs] + _dot_tn(vb[:, cs], k_d[:, cs]))
    return outs, new_states


def _hgrn_kernel(*refs, reverse, finish):
    if finish:
        q_ref, f_ref, v_ref, lb_ref, of_ref, g_ref, gain_ref, o_ref, st_ref = refs
    else:
        q_ref, f_ref, v_ref, lb_ref, o_ref, st_ref = refs
    C, SB = HG_CHUNK, HG_SUB
    nchunk = HG_TT // C

    @pl.when(pl.program_id(1) == 0)
    def _():
        st_ref[...] = jnp.zeros_like(st_ref)

    row = lax.broadcasted_iota(jnp.int32, (C, C), 0)
    col = lax.broadcasted_iota(jnp.int32, (C, C), 1)
    tri = ((col >= row) if reverse else (col <= row)).astype(BF16)
    s_i = lax.broadcasted_iota(jnp.int32, (SB, SB, HG_DK), 0)
    t_i = lax.broadcasted_iota(jnp.int32, (SB, SB, HG_DK), 1)
    pair_mask = (s_i >= t_i) if reverse else (s_i <= t_i)
    ones = jnp.ones((HG_DK, HG_DV), BF16)
    level_masks = []
    for size in HG_LEVELS:
        tg, sg = row // size, col // size
        later, earlier = (0, 1) if reverse else (1, 0)
        level_masks.append((tg // 2 == sg // 2) & (tg % 2 == later) & (sg % 2 == earlier))

    def chunk(ci, carry):
        c = (nchunk - 1 - ci) if reverse else ci
        sl = pl.ds(pl.multiple_of(c * C, C), C)
        outs, states = _hgrn_chunk(q_ref[sl, :], f_ref[sl, :], v_ref[sl, :], lb_ref[0],
                                   [st_ref[h] for h in range(HG_HEADS)], tri, pair_mask, level_masks, ones,
                                   reverse=reverse)
        for h in range(HG_HEADS):
            cs = slice(h * HG_DK, (h + 1) * HG_DK)
            o, st_ref[h] = outs[h], states[h]
            if finish:
                tot = of_ref[sl, cs] + o
                ms = jnp.mean(tot * tot, axis=-1, keepdims=True)
                g = g_ref[sl, cs]
                y = tot * lax.rsqrt(ms + EPS) * gain_ref[...] * (g * jax.nn.sigmoid(g))
                o_ref[sl, cs] = y.astype(o_ref.dtype)
            else:
                o_ref[sl, cs] = o
        return carry

    lax.fori_loop(0, nchunk, chunk, 0, unroll=8)


def _hgrn_pass(z, lb, batch, seq, *, reverse, o_fwd=None, gain=None):
    n = z.shape[0]
    nt = seq // HG_TT
    finish = o_fwd is not None
    d = 1 if reverse else 0

    def tmap(t):
        return (nt - 1 - t) if reverse else t

    def zspec(col0, per_dir=True):
        blk = col0 // HG_HEADS + (d if per_dir else 0)
        return pl.BlockSpec((HG_TT, HG_WIDTH), lambda b, t: (b * nt + tmap(t), blk))

    ospec = pl.BlockSpec((HG_TT, HG_WIDTH), lambda b, t: (b * nt + tmap(t), 0))
    in_specs = [zspec(_COL_HG_Q), zspec(_COL_HG_F), zspec(_COL_HG_I),
                pl.BlockSpec((1, 1, HG_WIDTH), lambda b, t: (d, 0, 0))]
    args = [z, z, z, lb]
    if finish:
        in_specs += [ospec, zspec(_COL_HG_G, per_dir=False), pl.BlockSpec((1, HG_DV), lambda b, t: (0, 0))]
        args += [o_fwd, z, gain]
    return pl.pallas_call(
        functools.partial(_hgrn_kernel, reverse=reverse, finish=finish),
        grid=(batch, nt),
        in_specs=in_specs,
        out_specs=ospec,
        out_shape=jax.ShapeDtypeStruct((n, HG_WIDTH), BF16 if finish else F32),
        scratch_shapes=[pltpu.VMEM((HG_HEADS, HG_DV, HG_DK), F32)],
        compiler_params=pltpu.CompilerParams(
            dimension_semantics=("parallel", "arbitrary"),
            vmem_limit_bytes=V7X_VMEM_LIMIT),
        name="hgrn_bwd" if reverse else "hgrn_fwd",
    )(*args)


NA_BAND = 8
NA_BT = NA_BAND * GRID_W


def _na_bias_table(rpb):
    c = jnp.arange(GRID_W)
    c0 = jnp.clip(c - NA_WIN_C // 2, 0, GRID_W - NA_WIN_C)
    kc = jnp.arange(GRID_W)
    valid = (kc[None, :] >= c0[:, None]) & (kc[None, :] < c0[:, None] + NA_WIN_C)
    off = jnp.clip(kc[None, :] - c[:, None] + NA_WIN_C - 1, 0, 2 * NA_WIN_C - 2)
    t = jnp.where(valid[None, None], rpb.astype(F32)[:, :, off], NEG_BIG)
    t = t.reshape(NA_HEADS // 2, 2, 2 * NA_WIN_R - 1, GRID_W, GRID_W)
    t = t.transpose(0, 2, 4, 1, 3)
    return t.reshape(NA_HEADS // 2, (2 * NA_WIN_R - 1) * GRID_W, 2 * GRID_W)


def _na_kernel(q_ref, k0_ref, k1_ref, k2_ref, v0_ref, v1_ref, v2_ref, bias_ref, o_ref,
               kcat_ref, vcat_ref, *, rows):
    g = pl.program_id(1)
    for i, (kr, vr) in enumerate(((k0_ref, v0_ref), (k1_ref, v1_ref), (k2_ref, v2_ref))):
        kcat_ref[i * NA_BT:(i + 1) * NA_BT, :] = kr[...]
        vcat_ref[i * NA_BT:(i + 1) * NA_BT, :] = vr[...]
    lane = lax.broadcasted_iota(jnp.int32, (1, 2 * NA_HEAD_DIM), 1)
    left = lane < NA_HEAD_DIM
    nkeys = NA_WIN_R * GRID_W
    scale = NA_HEAD_DIM ** -0.5
    pairs = range(NA_HEADS // 2)

    def body(j, carry):
        r = g * NA_BAND + j
        r0 = jnp.clip(r - NA_WIN_R // 2, 0, rows - NA_WIN_R)
        koff = pl.multiple_of((r0 - g * NA_BAND + NA_BAND) * GRID_W, GRID_W)
        boff = pl.multiple_of((r0 - r + NA_WIN_R - 1) * GRID_W, GRID_W)
        qsl = pl.ds(pl.multiple_of(j * GRID_W, GRID_W), GRID_W)
        qb = q_ref[qsl, :] * scale
        cols = [slice(hp * 2 * NA_HEAD_DIM, (hp + 1) * 2 * NA_HEAD_DIM) for hp in pairs]
        s_t = []
        for cs in cols:
            q2 = qb[:, cs]
            zero = jnp.zeros_like(q2)
            rhs_t = jnp.concatenate([jnp.where(left, q2, zero), jnp.where(left, zero, q2)], axis=0)
            s_t.append(_dot_nt(kcat_ref[pl.ds(koff, nkeys), cs], rhs_t))
        s_t = [s + bias_ref[hp, pl.ds(boff, nkeys), :] for hp, s in zip(pairs, s_t)]
        p = [jnp.exp(s - jnp.max(s, axis=0, keepdims=True)) for s in s_t]
        p = [(x * (1.0 / jnp.sum(x, axis=0, keepdims=True))).astype(BF16) for x in p]
        o2 = [_dot_tn(x, vcat_ref[pl.ds(koff, nkeys), cs]) for x, cs in zip(p, cols)]
        for cs, o in zip(cols, o2):
            o_ref[qsl, cs] = jnp.where(left, o[:GRID_W], o[GRID_W:]).astype(o_ref.dtype)
        return carry

    lax.fori_loop(0, NA_BAND, body, 0, unroll=8)


def _na(z, bias_tbl, batch, seq):
    n = z.shape[0]
    rows = seq // GRID_W
    nb = rows // NA_BAND

    def kv(colblk, shift):
        return pl.BlockSpec(
            (NA_BT, NA_WIDTH),
            lambda b, g: (b * nb + jnp.clip(g + shift, 0, nb - 1), colblk))

    return pl.pallas_call(
        functools.partial(_na_kernel, rows=rows),
        grid=(batch, nb),
        in_specs=[pl.BlockSpec((NA_BT, NA_WIDTH), lambda b, g: (b * nb + g, 0)),
                  kv(1, -1), kv(1, 0), kv(1, 1), kv(2, -1), kv(2, 0), kv(2, 1),
                  pl.BlockSpec(bias_tbl.shape, lambda b, g: (0, 0, 0))],
        out_specs=pl.BlockSpec((NA_BT, NA_WIDTH), lambda b, g: (b * nb + g, 0)),
        out_shape=jax.ShapeDtypeStruct((n, NA_WIDTH), BF16),
        scratch_shapes=[pltpu.VMEM((3 * NA_BT, NA_WIDTH), BF16),
                        pltpu.VMEM((3 * NA_BT, NA_WIDTH), BF16)],
        compiler_params=pltpu.CompilerParams(
            dimension_semantics=("parallel", "arbitrary"), vmem_limit_bytes=V7X_VMEM_LIMIT),
        name="natten",
    )(z, z, z, z, z, z, z, bias_tbl)


MG_TM = 512


def _merge_kernel(x_ref, oa_ref, ob_ref, ga_ref, gb_ref, wa_ref, wb_ref, wo_ref, nf_ref,
                  x2_ref, xt_ref):
    a = _dot(oa_ref[...], wa_ref[...])
    b = _dot(ob_ref[...], wb_ref[...])
    mix = jax.nn.sigmoid(ga_ref[...]) * a + jax.nn.sigmoid(gb_ref[...]) * b
    x2 = x_ref[...] + _dot(mix.astype(BF16), wo_ref[...])
    x2_ref[...] = x2
    ms = jnp.mean(x2 * x2, axis=-1, keepdims=True)
    xt_ref[...] = (x2 * lax.rsqrt(ms + EPS) * nf_ref[...]).T.astype(BF16)


def _merge(x2d, o_a, o_b, z, wa, wb, wo, norm_ffn):
    n = x2d.shape[0]
    col_ga = (_COL_HG_G * V7X_LANES + HG_WIDTH) // D_MODEL
    full = lambda shape: pl.BlockSpec(shape, lambda i: (0, 0))
    tok = lambda w, c=0: pl.BlockSpec((MG_TM, w), lambda i: (i, c))
    return pl.pallas_call(
        _merge_kernel,
        grid=(n // MG_TM,),
        in_specs=[tok(D_MODEL), tok(HG_WIDTH), tok(NA_WIDTH), tok(D_MODEL, col_ga), tok(D_MODEL, col_ga + 1),
                  full((HG_WIDTH, D_MODEL)), full((NA_WIDTH, D_MODEL)), full((D_MODEL, D_MODEL)),
                  full((1, D_MODEL))],
        out_specs=[tok(D_MODEL), pl.BlockSpec((D_MODEL, MG_TM), lambda i: (0, i))],
        out_shape=[jax.ShapeDtypeStruct((n, D_MODEL), F32), jax.ShapeDtypeStruct((D_MODEL, n), BF16)],
        compiler_params=pltpu.CompilerParams(
            dimension_semantics=("parallel",), vmem_limit_bytes=V7X_VMEM_LIMIT),
        name="merge",
    )(x2d, o_a, o_b, z, z, wa, wb, wo, norm_ffn)


PEER_TB = 512
PEER_EB = 1024
PEER_LC = 256
PEER_KC = 256
PEER_UK = 256
PEER_FIRST_SLOT = 2


def _peer_route(xt_ref, wq_ref, keys_ref, qt_ref, s_ref, toph_ref, top_ref, theta_ref,
                cnt_ref, coef_ref, r1_ref, e1_ref):
    K = PEER_TOPK
    half_rows = wq_ref.shape[0] // 2
    for part in range(2):
        rs = slice(part * half_rows, (part + 1) * half_rows)
        qt_ref[rs, :] = _dot(wq_ref[rs, :], xt_ref[...]).astype(BF16)

    for hp in range(2 * PEER_HEADS):
        s_ref[hp] = _dot(keys_ref[hp], qt_ref[hp * PEER_D_HALF:(hp + 1) * PEER_D_HALF, :])

    n_grp = PEER_N_KEYS // V7X_SUBLANES
    assert n_grp == K

    def exchange(rows, i, j):
        rows[i], rows[j] = jnp.maximum(rows[i], rows[j]), jnp.minimum(rows[i], rows[j])

    def extract(h, carry, *, half):
        for lc in range(xt_ref.shape[1] // V7X_LANES):
            ls = slice(lc * V7X_LANES, (lc + 1) * V7X_LANES)
            keys = [s_ref[2 * h + half, V7X_SUBLANES * r:V7X_SUBLANES * (r + 1), ls] for r in range(n_grp)]
            rows = list(keys)
            for i, j in _SORT16:
                exchange(rows, i, j)
            for shift in (4, 2, 1):
                rows = [jnp.maximum(rows[i], pltpu.roll(rows[K - 1 - i], shift, axis=0)) for i in range(K)]
                for i, j in _BITONIC16:
                    exchange(rows, i, j)
            toph_ref[half, h, :, ls] = jnp.concatenate([rows[q][0:1] for q in range(K)], axis=0)
            if half:
                for r in range(0, n_grp, 2):
                    rank = []
                    for k in keys[r:r + 2]:
                        rk = jnp.zeros_like(k)
                        for q in range(K):
                            rk = jnp.where(k < rows[q], float(q + 1), rk)
                        rank.append(rk)
                    r1_ref[h, V7X_SUBLANES * r:V7X_SUBLANES * (r + 2), ls] = (
                        jnp.concatenate(rank, axis=0).astype(BF16))
        return carry

    lax.fori_loop(0, PEER_HEADS, functools.partial(extract, half=0), 0)
    lax.fori_loop(0, PEER_HEADS, functools.partial(extract, half=1), 0)
    for half in range(2):
        for q in range(K):
            top_ref[half, q] = jnp.concatenate(
                [toph_ref[half, h, q:q + 1, :] for h in range(PEER_HEADS)], axis=0)

    small = top_ref.shape[2:]
    neg = jnp.full(small, -jnp.inf, F32)
    lists = [[top_ref[0, p] + top_ref[1, q] if (p + 1) * (q + 1) <= K else neg for q in range(K)]
             for p in range(K // 2)]
    lists.append([top_ref[0, p] + top_ref[1, 0] if p < K else neg for p in range(K // 2, K + K // 2)])
    assert sum(x is not neg for lst in lists for x in lst) == len(_CAND_PAIRS)
    merged = lists[0]
    for other in lists[1:]:
        merged = [jnp.maximum(merged[i], other[K - 1 - i]) for i in range(K)]
        for i, j in _BITONIC16:
            exchange(merged, i, j)
    tau = merged[K - 1]
    best = top_ref[0, 0] + top_ref[1, 0]
    zsum = jnp.zeros(small, F32)
    for p, q in _CAND_PAIRS:
        v = top_ref[0, p] + top_ref[1, q]
        zsum = zsum + jnp.where(v >= tau, jnp.exp(v - best), 0.0)
    inv_z = 1.0 / zsum

    for q in range(K):
        th = jnp.full(small, jnp.inf, F32)
        b_q = top_ref[1, q]
        for p in range(K):
            a_p = top_ref[0, p]
            th = jnp.minimum(th, jnp.where(a_p + b_q >= tau, a_p, jnp.inf))
        theta_ref[q] = th

    for h in range(PEER_HEADS):
        s0 = s_ref[2 * h]
        cnt = jnp.zeros(s0.shape, F32)
        for q in range(K):
            cnt = jnp.where(s0 >= theta_ref[q, h:h + 1, :], float(q + 1), cnt)
        cnt_ref[h] = cnt
        coef_ref[h] = jnp.exp(s0 - top_ref[0, 0, h:h + 1, :])
        e1_ref[h] = (jnp.exp(s_ref[2 * h + 1] - top_ref[1, 0, h:h + 1, :]) * (0.5 * inv_z[h:h + 1])).astype(BF16)


def _peer_kernel(xt_ref, x2_ref, wq_ref, keys_ref, u_ref, vt_ref, o_ref,
                 qt_ref, s_ref, toph_ref, top_ref, theta_ref,
                 cnt_ref, coef_ref, r1_ref, e1_ref, g_ref, acc_ref):
    e = pl.program_id(1)
    ne = pl.num_programs(1)

    @pl.when(e == 0)
    def _():
        _peer_route(xt_ref, wq_ref, keys_ref, qt_ref, s_ref, toph_ref, top_ref, theta_ref,
                    cnt_ref, coef_ref, r1_ref, e1_ref)
        acc_ref[...] = jnp.zeros_like(acc_ref)

    n_i = PEER_EB // PEER_N_KEYS
    i_rows = pl.ds(pl.multiple_of(e * n_i, n_i), n_i)
    zero = jnp.zeros((PEER_N_KEYS, PEER_LC), BF16)
    n_lc = PEER_TB // PEER_LC
    cnt_rows = [[cnt_ref[h, i_rows, lc * PEER_LC:(lc + 1) * PEER_LC].astype(BF16) for h in range(PEER_HEADS)]
                for lc in range(n_lc)]
    coef_rows = [[coef_ref[h, i_rows, lc * PEER_LC:(lc + 1) * PEER_LC].astype(BF16) for h in range(PEER_HEADS)]
                 for lc in range(n_lc)]
    anchors = []
    for ii in range(n_i):
        for lc in range(n_lc):
            ls = slice(lc * PEER_LC, (lc + 1) * PEER_LC)
            gate = zero
            for h in range(PEER_HEADS):
                sel = r1_ref[h, :, ls] < cnt_rows[lc][h][ii:ii + 1]
                gate = gate + jnp.where(sel, e1_ref[h, :, ls], zero) * coef_rows[lc][h][ii:ii + 1]
            g_ref[ii * PEER_N_KEYS:(ii + 1) * PEER_N_KEYS, ls] = gate
            bits = pltpu.bitcast(gate, jnp.uint32)
            anchors.append(pltpu.bitcast((bits >> 16) >> 16, BF16))
    anchors = anchors[:len(anchors) - n_lc * PEER_KC // PEER_N_KEYS]

    n_k = D_MODEL // PEER_UK
    free_slots = n_k * n_i - PEER_FIRST_SLOT
    anchor_of = {PEER_FIRST_SLOT + t * free_slots // len(anchors): a for t, a in enumerate(anchors)}
    cols = []
    for k in range(n_k):
        pieces = []
        for rg in range(n_i):
            piece = u_ref[rg * PEER_N_KEYS:(rg + 1) * PEER_N_KEYS, k * PEER_UK:(k + 1) * PEER_UK]
            if k * n_i + rg in anchor_of:
                piece = piece + anchor_of[k * n_i + rg]
            pieces.append(piece)
        cols.append(jnp.concatenate(pieces, axis=0))
    h_t = _dot(jnp.concatenate(cols, axis=1), xt_ref[...])
    y_t = None
    for c in range(PEER_EB // PEER_KC):
        rs = slice(c * PEER_KC, (c + 1) * PEER_KC)
        hh = h_t[rs]
        act = hh * (1.0 + lax.erf(hh * (2.0 ** -0.5)))
        part = _dot(vt_ref[:, rs], act.astype(BF16) * g_ref[rs, :])
        y_t = part if y_t is None else y_t + part
    acc_ref[...] += y_t

    @pl.when(e == ne - 1)
    def _():
        o_ref[...] = x2_ref[...] + acc_ref[...].T


def _peer(xt, x2, wq_t, keys, u_bf16, vt_bf16):
    n = x2.shape[0]
    tb, eb = PEER_TB, PEER_EB
    return pl.pallas_call(
        _peer_kernel,
        grid=(n // tb, PEER_N_EXPERTS // eb),
        in_specs=[
            pl.BlockSpec((D_MODEL, tb), lambda i, e: (0, i)),
            pl.BlockSpec((tb, D_MODEL), lambda i, e: (i, 0)),
            pl.BlockSpec(wq_t.shape, lambda i, e: (0, 0)),
            pl.BlockSpec(keys.shape, lambda i, e: (0, 0, 0)),
            pl.BlockSpec((eb, D_MODEL), lambda i, e: (e, 0)),
            pl.BlockSpec((D_MODEL, eb), lambda i, e: (0, e)),
        ],
        out_specs=pl.BlockSpec((tb, D_MODEL), lambda i, e: (i, 0)),
        out_shape=jax.ShapeDtypeStruct((n, D_MODEL), F32),
        scratch_shapes=[
            pltpu.VMEM((wq_t.shape[0], tb), BF16),
            pltpu.VMEM((2 * PEER_HEADS, PEER_N_KEYS, tb), F32),
            pltpu.VMEM((2, PEER_HEADS, PEER_TOPK, tb), F32),
            pltpu.VMEM((2, PEER_TOPK, PEER_HEADS, tb), F32),
            pltpu.VMEM((PEER_TOPK, PEER_HEADS, tb), F32),
            pltpu.VMEM((PEER_HEADS, PEER_N_KEYS, tb), F32),
            pltpu.VMEM((PEER_HEADS, PEER_N_KEYS, tb), F32),
            pltpu.VMEM((PEER_HEADS, PEER_N_KEYS, tb), BF16),
            pltpu.VMEM((PEER_HEADS, PEER_N_KEYS, tb), BF16),
            pltpu.VMEM((eb, tb), BF16),
            pltpu.VMEM((D_MODEL, tb), F32),
        ],
        compiler_params=pltpu.CompilerParams(
            dimension_semantics=("parallel", "arbitrary"), vmem_limit_bytes=V7X_VMEM_LIMIT),
        name="peer",
    )(xt, x2, wq_t, keys, u_bf16, vt_bf16)


def _trunk(x, p):
    batch, seq, _ = x.shape
    x2d = x.reshape(batch * seq, D_MODEL)
    z, qkv = _in_proj(x2d, p["norm_mix"], p["w_in"], p["qk_gain"], p["head_red"], p["head_exp"])
    o_f = _hgrn_pass(z, p["lb"], batch, seq, reverse=False)
    o_a = _hgrn_pass(z, p["lb"], batch, seq, reverse=True, o_fwd=o_f, gain=p["hg_out_norm"])
    o_b = _na(qkv, p["na_bias"], batch, seq)
    x2, xt = _merge(x2d, o_a, o_b, z, p["w_proj_a"], p["w_proj_b"], p["w_out"], p["norm_ffn"])
    y = _peer(xt, x2, p["wq_t"], p["keys"], p["expert_u"], p["expert_vt"])
    return y.reshape(batch, seq, D_MODEL)


def kernel(x_prompt, x_sample, norm_mix, w_in, lb_logits, hg_out_norm, q_norm, k_norm, rel_pos_bias,
           w_proj_a, w_proj_b, w_out, norm_ffn, w_query, sub_keys, expert_u, expert_v):
    l = 0
    lb_all = jnp.cumsum(jax.nn.softmax(lb_logits.astype(F32), axis=0), axis=0)
    head_id = np.arange(2 * NA_WIDTH) // NA_HEAD_DIM
    p = {
        "norm_mix": norm_mix[l].reshape(1, D_MODEL),
        "w_in": w_in[l].astype(BF16),
        "qk_gain": jnp.concatenate([jnp.tile(q_norm[l], NA_HEADS), jnp.tile(k_norm[l], NA_HEADS)]).reshape(1, -1),
        "head_red": jnp.asarray(head_id[:, None] == np.arange(V7X_LANES)[None, :], BF16),
        "head_exp": jnp.asarray(np.arange(V7X_LANES)[:, None] == head_id[None, :], BF16),
        "lb": lb_all[l].reshape(2, 1, HG_WIDTH),
        "hg_out_norm": hg_out_norm[l].reshape(1, HG_DV),
        "na_bias": _na_bias_table(rel_pos_bias[l]),
        "w_proj_a": w_proj_a[l].astype(BF16),
        "w_proj_b": w_proj_b[l].astype(BF16),
        "w_out": w_out[l].astype(BF16),
        "norm_ffn": norm_ffn[l].reshape(1, D_MODEL),
        "wq_t": w_query[l].T.astype(BF16),
        "keys": sub_keys[l].reshape(2 * PEER_HEADS, PEER_N_KEYS, PEER_D_HALF).astype(BF16),
        "expert_u": expert_u[l].astype(BF16),
        "expert_vt": expert_v[l].T.astype(BF16),
    }
    return (_trunk(x_prompt, p), _trunk(x_sample, p))
```
